```python
import math
import jax, jax.numpy as jnp
from jax import lax
import numpy as np

D_MODEL = 1024
BATCH = 4
SEQ = 8192
DEPTH = 1
DEC_BATCH = 16
DEC_SEQ = 32
PAST_LEN = 2048

CHUNK = 64
Q_BLOCK = 128
SB_HEADS = 8
HEAD_DIM = 64
DSA_HEADS = 8
DSA_KV_HEADS = 2
DSA_GROUP = DSA_HEADS // DSA_KV_HEADS
IDX_HEADS = 8
IDX_DIM = 64
TOPK_MAX = 256
N_BUCKETS = 32
MAX_DISTANCE = 128
N_EXPERTS = 32
TOP_K_EXPERTS = 4
D_FF = 1024
SWIGLU_LIMIT = 7.0
SWIGLU_ALPHA = 1.702
MOE_BLOCK = 128
LN_EPS = 1e-5
DEEPNORM_ALPHA = (2.0 * DEPTH) ** 0.25
DEEPNORM_BETA = (8.0 * DEPTH) ** -0.25

SB_W = SB_HEADS * HEAD_DIM
DSA_W = DSA_HEADS * HEAD_DIM
DSA_KV_W = DSA_KV_HEADS * HEAD_DIM
IN_SPLITS = (SB_W, SB_W, SB_W, DSA_W, DSA_KV_W, DSA_KV_W, IDX_HEADS * IDX_DIM, IDX_HEADS, IDX_DIM, D_MODEL, D_MODEL)
IN_COLS = sum(IN_SPLITS)

kernel_name = 'hybrid_stickbreak_dsa_moe_stream_step'


def layer_norm(x, g, b):
    xf = x.astype(jnp.float32)
    mu = jnp.mean(xf, axis=-1, keepdims=True)
    var = jnp.mean(jnp.square(xf - mu), axis=-1, keepdims=True)
    return ((xf - mu) * lax.rsqrt(var + LN_EPS) * g + b).astype(x.dtype)


def t5_bucket(rel):
    half = N_BUCKETS // 2
    max_exact = half // 2
    base = jnp.where(rel > 0, half, 0)
    n = jnp.abs(rel)
    n_f = jnp.maximum(n, 1).astype(jnp.float32)
    large = max_exact + (jnp.log(n_f / max_exact) / math.log(MAX_DISTANCE / max_exact) * (half - max_exact)).astype(jnp.int32)
    large = jnp.minimum(large, half - 1)
    return base + jnp.where(n < max_exact, n, large)


def over_query_blocks(fn, q_arrays, q_pos):
    B, T = q_arrays[0].shape[:2]
    if T <= Q_BLOCK:
        return fn(*q_arrays, q_pos)
    nb = T // Q_BLOCK
    blocked = tuple(a.reshape(B, nb, Q_BLOCK, *a.shape[2:]).swapaxes(0, 1) for a in q_arrays)
    out = lax.map(lambda args: fn(*args), blocked + (q_pos.reshape(nb, Q_BLOCK),))
    return out.swapaxes(0, 1).reshape(B, T, *out.shape[3:])


def stick_breaking_block(q, q_pos, k, v, k_pos):
    z = jnp.einsum('bqhd,bkhd->bhqk', q, k, preferred_element_type=jnp.float32) * HEAD_DIM ** -0.5
    visible = k_pos[None, :] < q_pos[:, None]
    log_keep = jnp.where(visible, jax.nn.log_sigmoid(-z), 0.0)
    log_after = lax.cumsum(log_keep, axis=3, reverse=True) - log_keep
    w = jnp.where(visible, jnp.exp(jax.nn.log_sigmoid(z) + log_after), 0.0)
    return jnp.einsum('bhqk,bkhd->bqhd', w.astype(v.dtype), v)


def dsa_block(q, qi, wi, q_pos, k, v, ki, k_pos, rel_bias, topk):
    B, Tq = q.shape[:2]
    chunk_end = (q_pos // CHUNK + 1) * CHUNK
    s = jnp.einsum('bqhd,bkd->bqhk', qi, ki, preferred_element_type=jnp.float32) * IDX_DIM ** -0.5
    score = jnp.einsum('bqhk,bqh->bqk', jax.nn.relu(s), wi.astype(jnp.float32))
    admissible = k_pos[None, :] < chunk_end[:, None]
    score = jnp.where(admissible[None], score, -jnp.inf)
    _, idx = lax.top_k(score, topk)
    sel_pos = k_pos[idx]
    sel_ok = sel_pos < chunk_end[None, :, None]
    k_sel = jax.vmap(lambda kb, ib: kb[ib])(k, idx)
    v_sel = jax.vmap(lambda vb, ib: vb[ib])(v, idx)
    qg = q.reshape(B, Tq, DSA_KV_HEADS, DSA_GROUP, HEAD_DIM)
    logits = jnp.einsum('bqngd,bqknd->bqngk', qg, k_sel, preferred_element_type=jnp.float32) * HEAD_DIM ** -0.5
    bias = rel_bias[t5_bucket(sel_pos - q_pos[None, :, None])]
    bias = bias.reshape(B, Tq, topk, DSA_KV_HEADS, DSA_GROUP).transpose(0, 1, 3, 4, 2)
    logits = jnp.where(sel_ok[:, :, None, None, :], logits + bias.astype(jnp.float32), -jnp.inf)
    p = jax.nn.softmax(logits, axis=-1)
    o = jnp.einsum('bqngk,bqknd->bqngd', p.astype(v.dtype), v_sel)
    return o.reshape(B, Tq, DSA_HEADS, HEAD_DIM)


def token_mixer(h, caches, w_in, b_in, w_o_sb, w_o_dsa, w_out, rel_bias):
    B, T, _ = h.shape
    past_len = 0 if caches is None else caches[0].shape[1]
    proj = h @ w_in + b_in
    offsets = [int(o) for o in np.cumsum(IN_SPLITS)[:-1]]
    q_a, k_a, v_a, q_b, k_b, v_b, q_i, w_i, k_i, g_a, g_b = jnp.split(proj, offsets, axis=-1)
    q_a = q_a.reshape(B, T, SB_HEADS, HEAD_DIM)
    k_a = k_a.reshape(B, T, SB_HEADS, HEAD_DIM)
    v_a = v_a.reshape(B, T, SB_HEADS, HEAD_DIM)
    q_b = q_b.reshape(B, T, DSA_HEADS, HEAD_DIM)
    k_b = k_b.reshape(B, T, DSA_KV_HEADS, HEAD_DIM)
    v_b = v_b.reshape(B, T, DSA_KV_HEADS, HEAD_DIM)
    q_i = q_i.reshape(B, T, IDX_HEADS, IDX_DIM)
    w_i = w_i * IDX_HEADS ** -0.5
    new_rows = (k_a, v_a, k_b, v_b, k_i)
    if caches is None:
        ka_all, va_all, kb_all, vb_all, ki_all = new_rows
    else:
        ka_all, va_all, kb_all, vb_all, ki_all = [jnp.concatenate([c_, r], axis=1) for c_, r in zip(caches, new_rows)]
    q_pos = past_len + jnp.arange(T, dtype=jnp.int32)
    k_pos = jnp.arange(past_len + T, dtype=jnp.int32)
    topk = max(1, min(TOPK_MAX, (past_len + T) // 4))
    o_a = over_query_blocks(lambda q, p: stick_breaking_block(q, p, ka_all, va_all, k_pos), (q_a,), q_pos)
    o_b = over_query_blocks(lambda q, qi, wi, p: dsa_block(q, qi, wi, p, kb_all, vb_all, ki_all, k_pos, rel_bias, topk), (q_b, q_i, w_i), q_pos)
    y_a = o_a.reshape(B, T, SB_W) @ w_o_sb
    y_b = o_b.reshape(B, T, DSA_W) @ w_o_dsa
    merged = jax.nn.sigmoid(g_a) * y_a + jax.nn.sigmoid(g_b) * y_b
    return merged @ w_out, new_rows


def clamped_swiglu(u):
    x_glu = jnp.minimum(u[..., ::2], SWIGLU_LIMIT)
    x_lin = jnp.clip(u[..., 1::2], -SWIGLU_LIMIT, SWIGLU_LIMIT)
    return x_glu * jax.nn.sigmoid(SWIGLU_ALPHA * x_glu) * (x_lin + 1.0)


def routed_ffn(h, w_router, b_router, w_up, b_up, w_down, b_down):
    B, T, D = h.shape
    x = h.reshape(B * T, D)
    n_tok = B * T
    logits = (x @ w_router + b_router).astype(jnp.float32)
    top_val, top_idx = lax.top_k(logits, TOP_K_EXPERTS)
    gates = jax.nn.softmax(top_val, axis=-1)
    n_assign = n_tok * TOP_K_EXPERTS
    e_flat = top_idx.reshape(-1)
    tok_flat = jnp.repeat(jnp.arange(n_tok, dtype=jnp.int32), TOP_K_EXPERTS)
    g_flat = gates.reshape(-1)
    order = jnp.argsort(e_flat)
    e_s, tok_s, g_s = e_flat[order], tok_flat[order], g_flat[order]
    counts = jnp.bincount(e_flat, length=N_EXPERTS)
    padded = (counts + MOE_BLOCK - 1) // MOE_BLOCK * MOE_BLOCK
    start = jnp.cumsum(counts) - counts
    pend = jnp.cumsum(padded)
    pstart = pend - padded
    dest = pstart[e_s] + jnp.arange(n_assign, dtype=jnp.int32) - start[e_s]
    n_rows = (n_assign + N_EXPERTS * (MOE_BLOCK - 1) + MOE_BLOCK - 1) // MOE_BLOCK * MOE_BLOCK
    n_blocks = n_rows // MOE_BLOCK
    row_tok = jnp.zeros((n_rows,), jnp.int32).at[dest].set(tok_s)
    row_gate = jnp.zeros((n_rows,), jnp.float32).at[dest].set(g_s)
    block_expert = jnp.minimum(jnp.searchsorted(pend, jnp.arange(n_blocks, dtype=jnp.int32) * MOE_BLOCK, side='right'), N_EXPERTS - 1)

    def expert_block(args):
        toks, e = args
        u = x[toks] @ w_up[e] + b_up[e]
        return clamped_swiglu(u) @ w_down[e] + b_down[e]

    out = lax.map(expert_block, (row_tok.reshape(n_blocks, MOE_BLOCK), block_expert))
    y = jnp.zeros_like(x).at[row_tok].add(out.reshape(n_rows, D) * row_gate[:, None].astype(x.dtype))
    return y.reshape(B, T, D)


def trunk_layer(x, c, caches, w_ada, b_ada, w_in, b_in, w_o_sb, w_o_dsa, w_out, ln1_g, ln1_b,
                w_router, b_router, w_up, b_up, w_down, b_down, ln2_g, ln2_b, rel_bias):
    mod = jax.nn.silu(c) @ w_ada + b_ada
    sh1, sc1, g1, sh2, sc2, g2 = jnp.split(mod[:, None, :], 6, axis=-1)
    h = x * (1.0 + sc1) + sh1
    mix, new_rows = token_mixer(h, caches, w_in, b_in, w_o_sb, w_o_dsa, w_out, rel_bias)
    x = layer_norm(DEEPNORM_ALPHA * x + g1 * mix, ln1_g, ln1_b)
    h = x * (1.0 + sc2) + sh2
    ffn = routed_ffn(h, w_router, b_router, w_up, b_up, w_down, b_down)
    x = layer_norm(DEEPNORM_ALPHA * x + g2 * ffn, ln2_g, ln2_b)
    return x, new_rows


def setup_inputs(seed: int = 0) -> dict:
    key = jax.random.key(seed)
    keys = list(jax.random.split(key, 32))
    counter = [0]

    def nrm(shape, scale):
        k = keys[counter[0]]
        counter[0] += 1
        return jax.random.normal(k, shape, jnp.float32) * scale

    L, D = DEPTH, D_MODEL
    inp = {}
    inp['x_prompt'] = nrm((BATCH, SEQ, D), 1.0)
    inp['x_sample'] = nrm((DEC_BATCH, DEC_SEQ, D), 1.0)
    inp['cache_sb_k'] = nrm((L, DEC_BATCH, PAST_LEN, SB_HEADS, HEAD_DIM), 1.0)
    inp['cache_sb_v'] = nrm((L, DEC_BATCH, PAST_LEN, SB_HEADS, HEAD_DIM), 1.0)
    inp['cache_dsa_k'] = nrm((L, DEC_BATCH, PAST_LEN, DSA_KV_HEADS, HEAD_DIM), 1.0)
    inp['cache_dsa_v'] = nrm((L, DEC_BATCH, PAST_LEN, DSA_KV_HEADS, HEAD_DIM), 1.0)
    inp['cache_idx_k'] = nrm((L, DEC_BATCH, PAST_LEN, IDX_DIM), 1.0)
    inp['c_prompt'] = nrm((BATCH, D), 1.0)
    inp['c_sample'] = nrm((DEC_BATCH, D), 1.0)
    inp['rel_bias'] = nrm((N_BUCKETS, DSA_HEADS), 0.5)
    inp['w_ada'] = nrm((L, D, 6 * D), 0.5 * D ** -0.5)
    inp['b_ada'] = nrm((L, 6 * D), 0.02)
    inp['w_in'] = nrm((L, D, IN_COLS), D ** -0.5)
    inp['b_in'] = nrm((L, IN_COLS), 0.02)
    inp['w_o_sb'] = nrm((L, SB_W, D), DEEPNORM_BETA * SB_W ** -0.5)
    inp['w_o_dsa'] = nrm((L, DSA_W, D), DEEPNORM_BETA * DSA_W ** -0.5)
    inp['w_out'] = nrm((L, D, D), DEEPNORM_BETA * D ** -0.5)
    inp['ln1_g'] = 1.0 + nrm((L, D), 0.02)
    inp['ln1_b'] = nrm((L, D), 0.02)
    inp['w_router'] = nrm((L, D, N_EXPERTS), D ** -0.5)
    inp['b_router'] = nrm((L, N_EXPERTS), 0.01)
    inp['w_up'] = nrm((L, N_EXPERTS, D, 2 * D_FF), D ** -0.5)
    inp['b_up'] = nrm((L, N_EXPERTS, 2 * D_FF), 0.02)
    inp['w_down'] = nrm((L, N_EXPERTS, D_FF, D), DEEPNORM_BETA * D_FF ** -0.5)
    inp['b_down'] = nrm((L, N_EXPERTS, D), 0.02)
    inp['ln2_g'] = 1.0 + nrm((L, D), 0.02)
    inp['ln2_b'] = nrm((L, D), 0.02)
    return inp


def reference(x_prompt, x_sample, cache_sb_k, cache_sb_v, cache_dsa_k, cache_dsa_v, cache_idx_k,
              c_prompt, c_sample, rel_bias, w_ada, b_ada, w_in, b_in, w_o_sb, w_o_dsa, w_out,
              ln1_g, ln1_b, w_router, b_router, w_up, b_up, w_down, b_down, ln2_g, ln2_b):
    y_p, y_s = x_prompt, x_sample
    rows_p, rows_s = [], []
    for l in range(DEPTH):
        weights = (w_ada[l], b_ada[l], w_in[l], b_in[l], w_o_sb[l], w_o_dsa[l], w_out[l], ln1_g[l], ln1_b[l],
                   w_router[l], b_router[l], w_up[l], b_up[l], w_down[l], b_down[l], ln2_g[l], ln2_b[l], rel_bias)
        y_p, new_p = trunk_layer(y_p, c_prompt, None, *weights)
        past = (cache_sb_k[l], cache_sb_v[l], cache_dsa_k[l], cache_dsa_v[l], cache_idx_k[l])
        y_s, new_s = trunk_layer(y_s, c_sample, past, *weights)
        rows_p.append(new_p)
        rows_s.append(new_s)
    sb_k_p, sb_v_p, dsa_k_p, dsa_v_p, idx_k_p = [jnp.stack(r) for r in zip(*rows_p)]
    sb_k_s, sb_v_s, dsa_k_s, dsa_v_s, idx_k_s = [jnp.stack(r) for r in zip(*rows_s)]
    return (y_p, y_s, sb_k_p, sb_v_p, dsa_k_p, dsa_v_p, idx_k_p, sb_k_s, sb_v_s, dsa_k_s, dsa_v_s, idx_k_s)
```

```python
import functools

import jax
import jax.numpy as jnp
import numpy as np
from jax import lax
from jax.experimental import pallas as pl
from jax.experimental.pallas import tpu as pltpu

F32 = jnp.float32
BF16 = jnp.bfloat16
I32 = jnp.int32

D_MODEL = 1024
CHUNK = 64
SB_HEADS = 8
HEAD_DIM = 64
DSA_HEADS = 8
DSA_KV_HEADS = 2
IDX_HEADS = 8
IDX_DIM = 64
TOPK_MAX = 256
N_BUCKETS = 32
N_EXPERTS = 32
TOP_K_EXPERTS = 4
D_FF = 1024
SWIGLU_LIMIT = 7.0
SWIGLU_ALPHA = 1.702
MOE_BLOCK = 128
LN_EPS = 1e-5
DEPTH = 1
DEEPNORM_ALPHA = (2.0 * DEPTH) ** 0.25

LANES = 128
ATT_BLOCK = 128
VMEM_LIMIT = 56 * 1024 * 1024

EXP_ZERO_BELOW = -104.0
NEG_INF_KEY = -2139095041
INT32_MAX = 2147483647
MASKED_LOGIT = -1e30

_SEG = {}
_off = 0
for _name, _w in (("qa", 512), ("ka", 512), ("va", 512), ("qb", 512), ("kb", 128), ("vb", 128),
                  ("kbd", 256), ("vbd", 256), ("qi", 512), ("kid", 128), ("wi", 128),
                  ("ga", 1024), ("gb", 1024)):
    _SEG[_name] = (_off, _off + _w)
    _off += _w
PACKED_COLS = _off


def _cparams(sem):
    return pltpu.CompilerParams(dimension_semantics=sem, vmem_limit_bytes=VMEM_LIMIT)


def _dot(a, b):
    return jnp.dot(a, b, preferred_element_type=F32)


def _dot_nt(a, b):
    return lax.dot_general(a, b, (((1,), (1,)), ((), ())), preferred_element_type=F32)


def _split_bf16(x):
    hi = x.astype(BF16)
    lo = (x - hi.astype(F32)).astype(BF16)
    return hi, lo


def _dot3(a, b_hi, b_lo):
    a_hi, a_lo = _split_bf16(a)
    return _dot(a_hi, b_hi) + (_dot(a_hi, b_lo) + _dot(a_lo, b_hi))


def _mod_kernel(c_ref, w_ref, b_ref, o_ref):
    c = c_ref[...]
    s = c * jax.nn.sigmoid(c)
    w_hi, w_lo = _split_bf16(w_ref[...])
    o_ref[...] = _dot3(s, w_hi, w_lo) + b_ref[...]


def _mod_call(c_all, w_ada, b_ada):
    n, d = c_all.shape
    cols = w_ada.shape[1]
    tn = 1024
    return pl.pallas_call(
        _mod_kernel,
        grid=(cols // tn,),
        in_specs=[pl.BlockSpec((n, d), lambda j: (0, 0)),
                  pl.BlockSpec((d, tn), lambda j: (0, j)),
                  pl.BlockSpec((1, tn), lambda j: (0, j))],
        out_specs=pl.BlockSpec((n, tn), lambda j: (0, j)),
        out_shape=jax.ShapeDtypeStruct((n, cols), F32),
        compiler_params=_cparams(("arbitrary",)),
        name="adaln_mod",
    )(c_all, w_ada, b_ada.reshape(1, cols))


def _proj_kernel(x_ref, mod_ref, w_ref, b_ref,
                 qa_ref, ka32_ref, va32_ref, ka16_ref, va16_ref,
                 qb_ref, kb32_ref, vb32_ref, kbd_ref, vbd_ref,
                 qi_ref, ki32_ref, kid_ref, wi_ref, sg_ref):
    sh1 = mod_ref[0:1, :]
    sc1 = mod_ref[1:2, :]
    h = (x_ref[...] * (1.0 + sc1) + sh1).astype(BF16)

    def seg(name):
        a, b = _SEG[name]
        return _dot(h, w_ref[:, a:b]) + b_ref[:, a:b]

    qa_ref[...] = (seg("qa") * HEAD_DIM ** -0.5).astype(BF16)
    ka = seg("ka")
    ka32_ref[...] = ka
    ka16_ref[...] = ka.astype(BF16)
    va = seg("va")
    va32_ref[...] = va
    va16_ref[...] = va.astype(BF16)
    qb_ref[...] = (seg("qb") * HEAD_DIM ** -0.5).astype(BF16)
    kb32_ref[...] = seg("kb")
    vb32_ref[...] = seg("vb")
    kbd_ref[...] = seg("kbd").astype(BF16)
    vbd_ref[...] = seg("vbd").astype(BF16)
    qi_ref[...] = (seg("qi") * IDX_DIM ** -0.5).astype(BF16)
    kid = seg("kid")
    ki32_ref[...] = kid[:, :IDX_DIM]
    kid_ref[...] = kid.astype(BF16)
    wi_ref[...] = seg("wi") * IDX_HEADS ** -0.5
    a, _ = _SEG["ga"]
    _, b = _SEG["gb"]
    sg_ref[...] = jax.nn.sigmoid(_dot(h, w_ref[:, a:b]) + b_ref[:, a:b])


def _pack_w_in(w_in, b_in):
    offs = np.cumsum((0, 512, 512, 512, 512, 128, 128, 512, 8, 64, 1024, 1024))
    qa, ka, va, qb, kb, vb, qi, wi, ki, ga, gb = [slice(int(offs[i]), int(offs[i + 1])) for i in range(11)]

    def pack(m):
        kb_m, vb_m = m[..., kb], m[..., vb]
        dup = lambda t: jnp.concatenate([t[..., :64], t[..., :64], t[..., 64:], t[..., 64:]], axis=-1)
        wi_m = jnp.concatenate([m[..., wi], jnp.zeros(m.shape[:-1] + (LANES - IDX_HEADS,), m.dtype)], axis=-1)
        return jnp.concatenate([m[..., qa], m[..., ka], m[..., va], m[..., qb], kb_m, vb_m, dup(kb_m), dup(vb_m),
                                m[..., qi], m[..., ki], m[..., ki], wi_m, m[..., ga], m[..., gb]], axis=-1)

    return pack(w_in).astype(BF16), pack(b_in.reshape(1, -1))


def _proj_call(x, mod, w_packed, b_packed, tm):
    s, t, d = x.shape
    nt = t // tm
    row = lambda w: pl.BlockSpec((None, tm, w), lambda b, i: (b, i, 0))
    shp = lambda w, dt: jax.ShapeDtypeStruct((s, t, w), dt)
    outs = [(512, BF16), (512, F32), (512, F32), (512, BF16), (512, BF16),
            (512, BF16), (128, F32), (128, F32), (256, BF16), (256, BF16),
            (512, BF16), (IDX_DIM, F32), (128, BF16), (128, F32), (2048, F32)]
    return pl.pallas_call(
        _proj_kernel,
        grid=(s, nt),
        in_specs=[row(d),
                  pl.BlockSpec((None, 6, d), lambda b, i: (b, 0, 0)),
                  pl.BlockSpec((d, PACKED_COLS), lambda b, i: (0, 0)),
                  pl.BlockSpec((1, PACKED_COLS), lambda b, i: (0, 0))],
        out_specs=[row(w) for w, _ in outs],
        out_shape=[shp(w, dt) for w, dt in outs],
        compiler_params=_cparams(("arbitrary", "arbitrary")),
        name="in_proj",
    )(x, mod, w_packed, b_packed)


def _bias_kernel(tab_ref, o_ref):
    r = lax.broadcasted_iota(I32, (ATT_BLOCK, ATT_BLOCK), 0)
    c = lax.broadcasted_iota(I32, (ATT_BLOCK, ATT_BLOCK), 1)
    half = N_BUCKETS // 2
    max_exact = half // 2
    for j, off in enumerate((0, -ATT_BLOCK, -(1 << 20))):
        rel = c - r + off
        n = jnp.abs(rel)
        large = jnp.full_like(n, max_exact)
        for thr in (12, 16, 23, 32, 46, 64, 91):
            large = large + (n >= thr).astype(I32)
        bucket = jnp.where(rel > 0, half, 0) + jnp.where(n < max_exact, n, large)
        for h in range(DSA_HEADS):
            acc = jnp.zeros((ATT_BLOCK, ATT_BLOCK), F32)
            for b in range(N_BUCKETS):
                acc = jnp.where(bucket == b, tab_ref[b, h], acc)
            o_ref[h, j] = acc


def _bias_call(rel_bias):
    return pl.pallas_call(
        _bias_kernel,
        in_specs=[pl.BlockSpec(memory_space=pltpu.SMEM)],
        out_specs=pl.BlockSpec(memory_space=pltpu.VMEM),
        out_shape=jax.ShapeDtypeStruct((DSA_HEADS, 3, ATT_BLOCK, ATT_BLOCK), F32),
        name="t5_bias_tiles",
    )(rel_bias)


def _softplus(z):
    return jnp.maximum(z, 0.0) + jnp.log1p(jnp.exp(-jnp.abs(z)))


def _sb_kernel(q_ref, k_ref, v_ref, o_ref, *, tq, last_fn):
    tk = ATT_BLOCK
    last = last_fn(pl.program_id(2))
    lane = lax.broadcasted_iota(I32, (1, LANES), 1)
    low = lane < HEAD_DIM
    q = q_ref[...]
    zero = jnp.zeros_like(q)
    qh = (jnp.where(low, q, zero), jnp.where(low, zero, q))
    uj = lax.broadcasted_iota(I32, (tk, tk), 0)
    us = lax.broadcasted_iota(I32, (tk, tk), 1)
    u_mat = jnp.where(uj > us, 1.0, 0.0).astype(BF16)

    def tile(kb, vis, carries):
        start = pl.multiple_of(kb * tk, tk)
        kblk = k_ref[pl.ds(start, tk), :]
        vblk = v_ref[pl.ds(start, tk), :]
        pv = []
        new_carries = []
        for h in range(2):
            z = _dot_nt(qh[h], kblk)
            sp = _softplus(z)
            lk = -sp if vis is None else jnp.where(vis, -sp, 0.0)
            hi, lo = _split_bf16(lk)
            after = _dot(hi, u_mat) + _dot(lo, u_mat)
            w = jnp.exp((z - sp) + (after + carries[h]))
            if vis is not None:
                w = jnp.where(vis, w, 0.0)
            pv.append(_dot(w.astype(BF16), vblk))
            new_carries.append(carries[h] + jnp.sum(lk, axis=1, keepdims=True))
        return jnp.where(low, pv[0], pv[1]), new_carries

    r = lax.broadcasted_iota(I32, (tq, tk), 0)
    c = lax.broadcasted_iota(I32, (tq, tk), 1)
    zc = jnp.zeros((tq, 1), F32)
    acc, carries = tile(last, c < r, [zc, zc])

    def live(carries):
        return jnp.max(jnp.maximum(carries[0], carries[1])) > EXP_ZERO_BELOW

    def cond(st):
        kb, go, _, _, _ = st
        return jnp.logical_and(kb >= 0, go)

    def body(st):
        kb, _, acc, c0, c1 = st
        pv, nc = tile(kb, None, [c0, c1])
        return kb - 1, live(nc), acc + pv, nc[0], nc[1]

    st = lax.while_loop(cond, body, (last - 1, live(carries), acc, carries[0], carries[1]))
    o_ref[...] = st[2].astype(o_ref.dtype)


def _sb_call(q, k, v, tq, last_fn):
    s, t, _ = q.shape
    tkk = k.shape[1]
    return pl.pallas_call(
        functools.partial(_sb_kernel, tq=tq, last_fn=last_fn),
        grid=(s, SB_HEADS // 2, t // tq),
        in_specs=[pl.BlockSpec((None, tq, LANES), lambda b, p, i: (b, i, p)),
                  pl.BlockSpec((None, tkk, LANES), lambda b, p, i: (b, 0, p)),
                  pl.BlockSpec((None, tkk, LANES), lambda b, p, i: (b, 0, p))],
        out_specs=pl.BlockSpec((None, tq, LANES), lambda b, p, i: (b, i, p)),
        out_shape=jax.ShapeDtypeStruct(q.shape, BF16),
        compiler_params=_cparams(("arbitrary", "arbitrary", "arbitrary")),
        name="stick_breaking",
    )(q, k, v)


def _dsa_kernel(qi_ref, wi_ref, qb_ref, ki_ref, kb_ref, vb_ref, bt_ref, o_ref,
                qim_sc, wb_sc, qbm_sc, key_sc, m_sc, l_sc, acc_sc, *, tq, topk, last_fn, adm_fn):
    tk = ATT_BLOCK
    last = last_fn(pl.program_id(1))
    lane = lax.broadcasted_iota(I32, (1, LANES), 1)
    low = lane < HEAD_DIM

    for h in range(IDX_HEADS):
        p = h // 2
        t = qi_ref[:, p * LANES:(p + 1) * LANES]
        z = jnp.zeros_like(t)
        qim_sc[h] = jnp.where(low, t, z) if h % 2 == 0 else jnp.where(low, z, t)
        t = qb_ref[:, p * LANES:(p + 1) * LANES]
        qbm_sc[h] = jnp.where(low, t, z) if h % 2 == 0 else jnp.where(low, z, t)
        wb_sc[h] = jnp.broadcast_to(wi_ref[:, h:h + 1], (tq, LANES))

    def score_block(kb, adm):
        start = pl.multiple_of(kb * tk, tk)
        kblk = ki_ref[pl.ds(start, tk), :]
        s = jnp.zeros((tq, tk), F32)
        for h in range(IDX_HEADS):
            s = s + wb_sc[h] * jnp.maximum(_dot_nt(qim_sc[h], kblk), 0.0)
        bits = pltpu.bitcast(s, I32)
        key = jnp.where(bits < 0, bits ^ INT32_MAX, bits)
        if adm is not None:
            key = jnp.where(adm, key, NEG_INF_KEY)
        key_sc[kb] = key

    def score_body(kb, carry):
        score_block(kb, None)
        return carry

    lax.fori_loop(0, last, score_body, 0)
    r = lax.broadcasted_iota(I32, (tq, tk), 0)
    c = lax.broadcasted_iota(I32, (tq, tk), 1)
    score_block(last, adm_fn(r, c))

    def count_ge(thr, upto):
        def body(kb, acc):
            return acc + jnp.where(key_sc[kb] >= thr, 1.0, 0.0)
        acc = lax.fori_loop(0, upto + 1, body, jnp.zeros((tq, tk), F32))
        return jnp.sum(acc, axis=1, keepdims=True)

    def midpoint(lo, hi):
        return (lo >> 1) + (hi >> 1) + (lo & hi & 1)

    kf = float(topk)

    def bis_cond(st):
        return st[0]

    def is_open(lo, hi, cnt_lo):
        return jnp.logical_and(midpoint(lo, hi) != lo, cnt_lo != kf)

    def bis_body(st):
        _, lo, hi, cnt_lo, cnt_hi = st
        mid = midpoint(lo, hi)
        open_ = is_open(lo, hi, cnt_lo)
        cnt = count_ge(mid, last)
        ge = cnt >= kf
        up = jnp.logical_and(open_, ge)
        dn = jnp.logical_and(open_, jnp.logical_not(ge))
        lo = jnp.where(up, mid, lo)
        cnt_lo = jnp.where(up, cnt, cnt_lo)
        hi = jnp.where(dn, mid, hi)
        cnt_hi = jnp.where(dn, cnt, cnt_hi)
        go = jnp.max(jnp.where(is_open(lo, hi, cnt_lo), 1.0, 0.0)) > 0.0
        return go, lo, hi, cnt_lo, cnt_hi

    col1 = lambda v, dt: jnp.full((tq, 1), v, dt)
    _, thr, _, cnt_lo, cnt_hi = lax.while_loop(
        bis_cond, bis_body,
        (jnp.bool_(True), col1(NEG_INF_KEY, I32), col1(INT32_MAX, I32), col1(-1.0, F32), col1(0.0, F32)))

    tied = cnt_lo > kf

    @pl.when(jnp.max(jnp.where(tied, 1.0, 0.0)) > 0.0)
    def _():
        need = kf - cnt_hi

        def count_tied_before(m):
            def body(kb, acc):
                col = c + kb * tk
                hit = jnp.logical_and(key_sc[kb] == thr, col < m)
                return acc + jnp.where(hit, 1.0, 0.0)
            acc = lax.fori_loop(0, last + 1, body, jnp.zeros((tq, tk), F32))
            return jnp.sum(acc, axis=1, keepdims=True)

        def idx_body(_, st):
            lo2, hi2 = st
            mid2 = (lo2 + hi2) >> 1
            ok = count_tied_before(mid2) >= need
            return jnp.where(ok, lo2, mid2), jnp.where(ok, mid2, hi2)

        n_keys = (last + 1) * tk
        _, mstar = lax.fori_loop(0, 15, idx_body, (col1(0, I32), jnp.full((tq, 1), 0, I32) + n_keys))

        def demote(kb, carry):
            col = c + kb * tk
            key = key_sc[kb]
            drop = jnp.logical_and(tied, jnp.logical_and(key == thr, col >= mstar))
            key_sc[kb] = jnp.where(drop, NEG_INF_KEY, key)
            return carry

        lax.fori_loop(0, last + 1, demote, 0)

    thr_eff = jnp.maximum(thr, NEG_INF_KEY + 1)

    for h in range(DSA_HEADS):
        m_sc[h] = jnp.full((tq, 1), MASKED_LOGIT, F32)
        l_sc[h] = jnp.zeros((tq, 1), F32)
        acc_sc[h] = jnp.zeros((tq, LANES), F32)

    def attn_body(kb, carry):
        start = pl.multiple_of(kb * tk, tk)
        sel = key_sc[kb] >= thr_eff
        j = jnp.minimum(last - kb, 2)
        for h in range(DSA_HEADS):
            n = h // (DSA_HEADS // DSA_KV_HEADS)
            kblk = kb_ref[pl.ds(start, tk), n * LANES:(n + 1) * LANES]
            vblk = vb_ref[pl.ds(start, tk), n * LANES:(n + 1) * LANES]
            logit = _dot_nt(qbm_sc[h], kblk) + bt_ref[h, j, 0:tq, :]
            logit = jnp.where(sel, logit, MASKED_LOGIT)
            m_old = m_sc[h]
            m_new = jnp.maximum(m_old, jnp.max(logit, axis=1, keepdims=True))
            p = jnp.exp(logit - m_new)
            corr = jnp.exp(m_old - m_new)
            l_sc[h] = corr * l_sc[h] + jnp.sum(p, axis=1, keepdims=True)
            acc_sc[h] = corr * acc_sc[h] + _dot(p.astype(BF16), vblk)
            m_sc[h] = m_new
        return carry

    lax.fori_loop(0, last + 1, attn_body, 0)

    for p in range(DSA_HEADS // 2):
        o0 = acc_sc[2 * p] / l_sc[2 * p]
        o1 = acc_sc[2 * p + 1] / l_sc[2 * p + 1]
        o_ref[:, p * LANES:(p + 1) * LANES] = jnp.where(low, o0, o1).astype(o_ref.dtype)


def _dsa_call(qi, wi, qb, kid, kbd, vbd, btiles, tq, topk, last_fn, adm_fn):
    s, t, _ = qi.shape
    tkk = kid.shape[1]
    nkb = tkk // ATT_BLOCK
    rowq = lambda w: pl.BlockSpec((None, tq, w), lambda b, i: (b, i, 0))
    full = lambda w: pl.BlockSpec((None, tkk, w), lambda b, i: (b, 0, 0))
    return pl.pallas_call(
        functools.partial(_dsa_kernel, tq=tq, topk=topk, last_fn=last_fn, adm_fn=adm_fn),
        grid=(s, t // tq),
        in_specs=[rowq(512), rowq(LANES), rowq(512), full(LANES), full(2 * LANES), full(2 * LANES),
                  pl.BlockSpec(btiles.shape, lambda b, i: (0, 0, 0, 0))],
        out_specs=rowq(512),
        out_shape=jax.ShapeDtypeStruct((s, t, 512), BF16),
        scratch_shapes=[pltpu.VMEM((IDX_HEADS, tq, LANES), BF16),
                        pltpu.VMEM((IDX_HEADS, tq, LANES), F32),
                        pltpu.VMEM((DSA_HEADS, tq, LANES), BF16),
                        pltpu.VMEM((nkb, tq, ATT_BLOCK), I32),
                        pltpu.VMEM((DSA_HEADS, tq, 1), F32),
                        pltpu.VMEM((DSA_HEADS, tq, 1), F32),
                        pltpu.VMEM((DSA_HEADS, tq, LANES), F32)],
        compiler_params=_cparams(("arbitrary", "arbitrary")),
        name="dsa_attention",
    )(qi, wi, qb, kid, kbd, vbd, btiles)


def _layer_norm(x, g, b):
    mu = jnp.mean(x, axis=-1, keepdims=True)
    xc = x - mu
    var = jnp.mean(xc * xc, axis=-1, keepdims=True)
    return xc * lax.rsqrt(var + LN_EPS) * g + b


def _post_kernel(oa_ref, ob_ref, sg_ref, x_ref, mod_ref, wsb_ref, wdsa_ref, wout_ref, g_ref, b_ref,
                 wrh_ref, wrl_ref, br_ref, x1_ref, h2_ref, ti_ref, tg_ref):
    d = D_MODEL
    ya = _dot(oa_ref[...], wsb_ref[...])
    yb = _dot(ob_ref[...], wdsa_ref[...])
    merged = sg_ref[:, :d] * ya + sg_ref[:, d:] * yb
    mix = _dot(merged.astype(BF16), wout_ref[...])
    g1 = mod_ref[2:3, :]
    x1 = _layer_norm(DEEPNORM_ALPHA * x_ref[...] + g1 * mix, g_ref[...], b_ref[...])
    x1_ref[...] = x1
    h2 = x1 * (1.0 + mod_ref[4:5, :]) + mod_ref[3:4, :]
    h2_ref[...] = h2
    logits = _dot3(h2, wrh_ref[...], wrl_ref[...]) + br_ref[...]
    lane = lax.broadcasted_iota(I32, logits.shape, 1).astype(F32)
    neg = -jnp.inf
    cur = jnp.where(lane < N_EXPERTS, logits, neg)
    vals, idxs = [], []
    for _ in range(TOP_K_EXPERTS):
        m = jnp.max(cur, axis=1, keepdims=True)
        idx = jnp.min(jnp.where(cur == m, lane, float(LANES)), axis=1, keepdims=True)
        vals.append(m)
        idxs.append(idx)
        cur = jnp.where(lane == idx, neg, cur)
    es = [jnp.exp(v - vals[0]) for v in vals]
    tot = es[0] + es[1] + es[2] + es[3]
    ti = jnp.zeros(logits.shape, F32)
    tg = jnp.zeros(logits.shape, F32)
    for k in range(TOP_K_EXPERTS):
        ti = jnp.where(lane == k, idxs[k], ti)
        tg = jnp.where(lane == k, es[k] / tot, tg)
    ti_ref[...] = ti.astype(I32)
    tg_ref[...] = tg


def _post_call(oa, ob, sg, x, mod, wsb, wdsa, wout, ln_g, ln_b, wr_hi, wr_lo, br, tm):
    s, t, d = x.shape
    row = lambda w: pl.BlockSpec((None, tm, w), lambda b, i: (b, i, 0))
    const = lambda a: pl.BlockSpec(a.shape, lambda b, i: (0,) * a.ndim)
    shp = lambda w, dt: jax.ShapeDtypeStruct((s, t, w), dt)
    return pl.pallas_call(
        _post_kernel,
        grid=(s, t // tm),
        in_specs=[row(512), row(512), row(2 * d), row(d),
                  pl.BlockSpec((None, 6, d), lambda b, i: (b, 0, 0)),
                  const(wsb), const(wdsa), const(wout), const(ln_g), const(ln_b),
                  const(wr_hi), const(wr_lo), const(br)],
        out_specs=[row(d), row(d), row(LANES), row(LANES)],
        out_shape=[shp(d, F32), shp(d, F32), shp(LANES, I32), shp(LANES, F32)],
        compiler_params=_cparams(("arbitrary", "arbitrary")),
        name="post_attention_router",
    )(oa, ob, sg, x, mod, wsb, wdsa, wout, ln_g, ln_b, wr_hi, wr_lo, br)


def _ffn_kernel(be_ref, nused_ref, tok0_ref, tokn_ref, h_hbm, wup_ref, bup_ref, wdn_ref, bdn_ref, gate_ref,
                o_ref, xbuf, sem):
    i = pl.program_id(0)
    nb = pl.num_programs(0)
    m = MOE_BLOCK

    def row_copy(tok, slot, r):
        return pltpu.make_async_copy(h_hbm.at[pl.ds(tok, 1), :], xbuf.at[slot, pl.ds(r, 1), :], sem.at[slot])

    def issue(tok_ref, slot):
        def body(r, carry):
            row_copy(tok_ref[0, r], slot, r).start()
            return carry
        lax.fori_loop(0, m, body, 0, unroll=8)

    slot = i % 2

    @pl.when(i == 0)
    def _():
        issue(tok0_ref, 0)

    @pl.when(i + 1 < nb)
    def _():
        issue(tokn_ref, 1 - slot)

    def wait_body(r, carry):
        row_copy(0, slot, r).wait()
        return carry
    lax.fori_loop(0, m, wait_body, 0, unroll=8)

    @pl.when(i < nused_ref[0])
    def _():
        x = xbuf[slot].astype(BF16)
        u = _dot(x, wup_ref[...]) + bup_ref[...]
        glu = jnp.minimum(u[:, :D_FF], SWIGLU_LIMIT)
        lin = jnp.clip(u[:, D_FF:], -SWIGLU_LIMIT, SWIGLU_LIMIT)
        act = glu * jax.nn.sigmoid(SWIGLU_ALPHA * glu) * (lin + 1.0)
        o = _dot(act.astype(BF16), wdn_ref[...]) + bdn_ref[...]
        o_ref[...] = o * gate_ref[...]

    @pl.when(i >= nused_ref[0])
    def _():
        o_ref[...] = jnp.zeros_like(o_ref)


def _ffn_call(block_expert, row_tok, nused, h2, wup, bup, wdn, bdn, row_gate):
    n_rows = row_tok.shape[0]
    n_blocks = n_rows // MOE_BLOCK
    d = h2.shape[1]
    tok3 = row_tok.reshape(n_blocks, 1, MOE_BLOCK)
    grid_spec = pltpu.PrefetchScalarGridSpec(
        num_scalar_prefetch=2,
        grid=(n_blocks,),
        in_specs=[pl.BlockSpec((None, 1, MOE_BLOCK), lambda i, be, nu: (0, 0, 0), memory_space=pltpu.SMEM),
                  pl.BlockSpec((None, 1, MOE_BLOCK), lambda i, be, nu: (jnp.minimum(i + 1, n_blocks - 1), 0, 0),
                               memory_space=pltpu.SMEM),
                  pl.BlockSpec(memory_space=pl.ANY),
                  pl.BlockSpec((None, d, 2 * D_FF), lambda i, be, nu: (be[i], 0, 0)),
                  pl.BlockSpec((None, 1, 2 * D_FF), lambda i, be, nu: (be[i], 0, 0)),
                  pl.BlockSpec((None, D_FF, d), lambda i, be, nu: (be[i], 0, 0)),
                  pl.BlockSpec((None, 1, d), lambda i, be, nu: (be[i], 0, 0)),
                  pl.BlockSpec((MOE_BLOCK, 1), lambda i, be, nu: (i, 0))],
        out_specs=pl.BlockSpec((MOE_BLOCK, d), lambda i, be, nu: (i, 0)),
        scratch_shapes=[pltpu.VMEM((2, MOE_BLOCK, d), F32), pltpu.SemaphoreType.DMA((2,))],
    )
    return pl.pallas_call(
        _ffn_kernel,
        grid_spec=grid_spec,
        out_shape=jax.ShapeDtypeStruct((n_rows, d), F32),
        compiler_params=_cparams(("arbitrary",)),
        name="expert_ffn",
    )(block_expert, nused, tok3, tok3, h2, wup, bup, wdn, bdn, row_gate)


def _combine_kernel(pos0_ref, posn_ref, rows_hbm, x1_ref, mod_ref, g_ref, b_ref, o_ref, buf, sem, *, tm):
    i = pl.program_id(0)
    nb = pl.num_programs(0)
    k = TOP_K_EXPERTS

    def row_copy(src, slot, r, j):
        return pltpu.make_async_copy(rows_hbm.at[pl.ds(src, 1), :], buf.at[slot, j, pl.ds(r, 1), :], sem.at[slot])

    def issue(pos_ref, slot):
        def body(r, carry):
            for j in range(k):
                row_copy(pos_ref[0, r * k + j], slot, r, j).start()
            return carry
        lax.fori_loop(0, tm, body, 0, unroll=4)

    slot = i % 2

    @pl.when(i == 0)
    def _():
        issue(pos0_ref, 0)

    @pl.when(i + 1 < nb)
    def _():
        issue(posn_ref, 1 - slot)

    def wait_body(r, carry):
        for j in range(k):
            row_copy(0, slot, r, j).wait()
        return carry
    lax.fori_loop(0, tm, wait_body, 0, unroll=4)

    y = (buf[slot, 0] + buf[slot, 1]) + (buf[slot, 2] + buf[slot, 3])
    o_ref[...] = _layer_norm(DEEPNORM_ALPHA * x1_ref[...] + mod_ref[5:6, :] * y, g_ref[...], b_ref[...])


def _combine_call(pos, rows, x1, mod, seq_len, ln_g, ln_b, tm):
    n, d = x1.shape
    nb = n // tm
    per_seq = seq_len // tm
    pos3 = pos.reshape(nb, 1, tm * TOP_K_EXPERTS)
    return pl.pallas_call(
        functools.partial(_combine_kernel, tm=tm),
        grid=(nb,),
        in_specs=[pl.BlockSpec((None, 1, tm * TOP_K_EXPERTS), lambda i: (0, 0, 0), memory_space=pltpu.SMEM),
                  pl.BlockSpec((None, 1, tm * TOP_K_EXPERTS), lambda i: (jnp.minimum(i + 1, nb - 1), 0, 0),
                               memory_space=pltpu.SMEM),
                  pl.BlockSpec(memory_space=pl.ANY),
                  pl.BlockSpec((tm, d), lambda i: (i, 0)),
                  pl.BlockSpec((None, 6, d), lambda i: (i // per_seq, 0, 0)),
                  pl.BlockSpec((1, d), lambda i: (0, 0)),
                  pl.BlockSpec((1, d), lambda i: (0, 0))],
        out_specs=pl.BlockSpec((tm, d), lambda i: (i, 0)),
        out_shape=jax.ShapeDtypeStruct((n, d), F32),
        scratch_shapes=[pltpu.VMEM((2, TOP_K_EXPERTS, tm, d), F32), pltpu.SemaphoreType.DMA((2,))],
        compiler_params=_cparams(("arbitrary",)),
        name="moe_combine_ln2",
    )(pos3, pos3, rows, x1, mod, ln_g, ln_b)


def _routing(top_idx, top_gate):
    n_tok = top_idx.shape[0]
    n_assign = n_tok * TOP_K_EXPERTS
    e_flat = top_idx.reshape(-1)
    onehot = (e_flat[:, None] == jnp.arange(N_EXPERTS, dtype=I32)[None, :]).astype(I32)
    csum = jnp.cumsum(onehot, axis=0)
    rank = jnp.sum(onehot * csum, axis=1) - 1
    counts = csum[-1]
    padded = (counts + MOE_BLOCK - 1) // MOE_BLOCK * MOE_BLOCK
    pend = jnp.cumsum(padded)
    pstart = pend - padded
    dest = (pstart[e_flat] + rank).astype(I32)
    n_rows = (n_assign + N_EXPERTS * (MOE_BLOCK - 1) + MOE_BLOCK - 1) // MOE_BLOCK * MOE_BLOCK
    n_blocks = n_rows // MOE_BLOCK
    tok_flat = jnp.repeat(jnp.arange(n_tok, dtype=I32), TOP_K_EXPERTS)
    row_tok = jnp.zeros((n_rows,), I32).at[dest].set(tok_flat)
    row_gate = jnp.zeros((n_rows,), F32).at[dest].set(top_gate.reshape(-1))
    block_expert = jnp.minimum(
        jnp.searchsorted(pend, jnp.arange(n_blocks, dtype=I32) * MOE_BLOCK, side="right"), N_EXPERTS - 1).astype(I32)
    nused = (pend[-1] // MOE_BLOCK).astype(I32).reshape(1)
    return dest, row_tok, row_gate.reshape(n_rows, 1), block_expert, nused


def _stream(x, mod, kv_cache, weights, btiles, tm, tq):
    (w_packed, b_packed, wsb, wdsa, wout, ln1_g, ln1_b, wr_hi, wr_lo, br,
     wup, bup, wdn, bdn, ln2_g, ln2_b) = weights
    s, t, d = x.shape
    (qa, ka32, va32, ka16, va16, qb, kb32, vb32, kbd, vbd, qi, ki32, kid, wi, sg) = _proj_call(x, mod, w_packed, b_packed, tm)

    if kv_cache is None:
        k_sb, v_sb, k_id, k_bd, v_bd = ka16, va16, kid, kbd, vbd
        total = t
        last_fn = lambda i: i
        adm_fn = lambda r, c: c < (r // CHUNK + 1) * CHUNK
    else:
        past = kv_cache[0].shape[1]
        total = past + t
        pad = (-total) % ATT_BLOCK
        cat = lambda cache, new: jnp.concatenate(
            [cache, new, jnp.zeros((s, pad, new.shape[2]), new.dtype)], axis=1)
        k_sb, v_sb, k_id, k_bd, v_bd = [cat(c_, n_) for c_, n_ in zip(kv_cache, (ka16, va16, kid, kbd, vbd))]
        last_blk = past // ATT_BLOCK
        last_fn = lambda i: i * 0 + last_blk
        adm_fn = lambda r, c: c < t
    topk = max(1, min(TOPK_MAX, total // 4))

    oa = _sb_call(qa, k_sb, v_sb, tq, last_fn)
    ob = _dsa_call(qi, wi, qb, k_id, k_bd, v_bd, btiles, tq, topk, last_fn, adm_fn)
    x1, h2, ti, tg = _post_call(oa, ob, sg, x, mod, wsb, wdsa, wout, ln1_g, ln1_b, wr_hi, wr_lo, br, tm)

    n = s * t
    dest, row_tok, row_gate, block_expert, nused = _routing(
        ti.reshape(n, LANES)[:, :TOP_K_EXPERTS], tg.reshape(n, LANES)[:, :TOP_K_EXPERTS])
    rows = _ffn_call(block_expert, row_tok, nused, h2.reshape(n, d), wup, bup, wdn, bdn, row_gate)
    y = _combine_call(dest, rows, x1.reshape(n, d), mod, t, ln2_g, ln2_b, min(128, t))
    new_rows = (ka32.reshape(1, s, t, SB_HEADS, HEAD_DIM), va32.reshape(1, s, t, SB_HEADS, HEAD_DIM),
                kb32.reshape(1, s, t, DSA_KV_HEADS, HEAD_DIM), vb32.reshape(1, s, t, DSA_KV_HEADS, HEAD_DIM),
                ki32.reshape(1, s, t, IDX_DIM))
    return y.reshape(s, t, d), new_rows


def kernel(x_prompt, x_sample, cache_sb_k, cache_sb_v, cache_dsa_k, cache_dsa_v, cache_idx_k, c_prompt, c_sample, rel_bias, w_ada, b_ada, w_in, b_in, w_o_sb, w_o_dsa, w_out, ln1_g, ln1_b, w_router, b_router, w_up, b_up, w_down, b_down, ln2_g, ln2_b):
    d = D_MODEL
    nb, ns = x_prompt.shape[0], x_sample.shape[0]
    past = cache_sb_k.shape[2]

    mod = _mod_call(jnp.concatenate([c_prompt, c_sample], axis=0), w_ada[0], b_ada[0]).reshape(nb + ns, 6, d)
    btiles = _bias_call(rel_bias)

    w_packed, b_packed = _pack_w_in(w_in[0], b_in[0])
    wr = jnp.concatenate([w_router[0], jnp.zeros((d, LANES - N_EXPERTS), F32)], axis=1)
    wr_hi = wr.astype(BF16)
    wr_lo = (wr - wr_hi.astype(F32)).astype(BF16)
    br = jnp.concatenate([b_router[0], jnp.zeros((LANES - N_EXPERTS,), F32)]).reshape(1, LANES)
    wup = jnp.concatenate([w_up[0][:, :, 0::2], w_up[0][:, :, 1::2]], axis=-1).astype(BF16)
    bup = jnp.concatenate([b_up[0][:, 0::2], b_up[0][:, 1::2]], axis=-1).reshape(N_EXPERTS, 1, 2 * D_FF)
    weights = (w_packed, b_packed, w_o_sb[0].astype(BF16), w_o_dsa[0].astype(BF16), w_out[0].astype(BF16),
               ln1_g[0].reshape(1, d), ln1_b[0].reshape(1, d), wr_hi, wr_lo, br,
               wup, bup, w_down[0].astype(BF16), b_down[0].reshape(N_EXPERTS, 1, d),
               ln2_g[0].reshape(1, d), ln2_b[0].reshape(1, d))

    dup = lambda a: jnp.concatenate([a[..., :64], a[..., :64], a[..., 64:], a[..., 64:]], axis=-1)
    idx_c = cache_idx_k[0]
    caches = (cache_sb_k[0].reshape(ns, past, SB_HEADS * HEAD_DIM).astype(BF16),
              cache_sb_v[0].reshape(ns, past, SB_HEADS * HEAD_DIM).astype(BF16),
              jnp.concatenate([idx_c, idx_c], axis=-1).astype(BF16),
              dup(cache_dsa_k[0].reshape(ns, past, DSA_KV_HEADS * HEAD_DIM)).astype(BF16),
              dup(cache_dsa_v[0].reshape(ns, past, DSA_KV_HEADS * HEAD_DIM)).astype(BF16))

    t_p, t_s = x_prompt.shape[1], x_sample.shape[1]
    y_p, new_p = _stream(x_prompt, mod[:nb], None, weights, btiles, min(256, t_p), ATT_BLOCK)
    y_s, new_s = _stream(x_sample, mod[nb:], caches, weights, btiles, t_s, t_s)
    return (y_p, y_s) + new_p + new_s
```

```python
import functools

import jax
import jax.numpy as jnp
import numpy as np
from jax import lax
from jax.experimental import pallas as pl
from jax.experimental.pallas import tpu as pltpu

F32 = jnp.float32
BF16 = jnp.bfloat16
I32 = jnp.int32

D_MODEL = 1024
CHUNK = 64
SB_HEADS = 8
HEAD_DIM = 64
DSA_HEADS = 8
DSA_KV_HEADS = 2
IDX_HEADS = 8
IDX_DIM = 64
TOPK_MAX = 256
N_BUCKETS = 32
N_EXPERTS = 32
TOP_K_EXPERTS = 4
D_FF = 1024
SWIGLU_LIMIT = 7.0
SWIGLU_ALPHA = 1.702
MOE_BLOCK = 128
LN_EPS = 1e-5
DEPTH = 1
DEEPNORM_ALPHA = (2.0 * DEPTH) ** 0.25

LANES = 128
ATT_BLOCK = 128
VMEM_LIMIT = 56 * 1024 * 1024

EXP_ZERO_BELOW = -104.0
NEG_INF_KEY = -2139095041
INT32_MAX = 2147483647
MASKED_LOGIT = -1e30
BISECT_STEPS_PER_CHECK = 2
SWIGLU_TILE = 2 * LANES

_SEG = {}
_off = 0
for _name, _w in (("qa", 512), ("ka", 512), ("va", 512), ("qb", 512), ("kb", 128), ("vb", 128),
                  ("kbd", 256), ("vx", 256), ("qi", 512), ("kid", 128), ("wi", 128),
                  ("ga", 1024), ("gb", 1024)):
    _SEG[_name] = (_off, _off + _w)
    _off += _w
PACKED_COLS = _off


def _cparams(sem):
    return pltpu.CompilerParams(dimension_semantics=sem, vmem_limit_bytes=VMEM_LIMIT)


def _dot(a, b):
    return jnp.dot(a, b, preferred_element_type=F32)


def _dot_nt(a, b):
    return lax.dot_general(a, b, (((1,), (1,)), ((), ())), preferred_element_type=F32)


def _split_bf16(x):
    hi = x.astype(BF16)
    lo = (x - hi.astype(F32)).astype(BF16)
    return hi, lo


def _dot3(a, b_hi, b_lo):
    a_hi, a_lo = _split_bf16(a)
    return _dot(a_hi, b_hi) + (_dot(a_hi, b_lo) + _dot(a_lo, b_hi))


def _mod_kernel(c_ref, w_ref, b_ref, o_ref):
    c = c_ref[...]
    s = c * jax.nn.sigmoid(c)
    w_hi, w_lo = _split_bf16(w_ref[...])
    o_ref[...] = _dot3(s, w_hi, w_lo) + b_ref[...]


def _mod_call(c_all, w_ada, b_ada):
    n, d = c_all.shape
    cols = w_ada.shape[1]
    tn = 1024
    return pl.pallas_call(
        _mod_kernel,
        grid=(cols // tn,),
        in_specs=[pl.BlockSpec((n, d), lambda j: (0, 0)),
                  pl.BlockSpec((d, tn), lambda j: (0, j)),
                  pl.BlockSpec((1, tn), lambda j: (0, j))],
        out_specs=pl.BlockSpec((n, tn), lambda j: (0, j)),
        out_shape=jax.ShapeDtypeStruct((n, cols), F32),
        compiler_params=_cparams(("arbitrary",)),
        name="adaln_mod",
    )(c_all, w_ada, b_ada.reshape(1, cols))


def _proj_kernel(x_ref, mod_ref, w_ref, b_ref,
                 qa_ref, ka32_ref, va32_ref, ka16_ref, va16_ref,
                 qb_ref, kb32_ref, vb32_ref, kbd_ref, vx_ref,
                 qi_ref, ki32_ref, kid_ref, wi_ref, sg_ref):
    sh1 = mod_ref[0:1, :]
    sc1 = mod_ref[1:2, :]
    h = (x_ref[...] * (1.0 + sc1) + sh1).astype(BF16)

    def seg(name):
        a, b = _SEG[name]
        return _dot(h, w_ref[:, a:b]) + b_ref[:, a:b]

    qa_ref[...] = (seg("qa") * HEAD_DIM ** -0.5).astype(BF16)
    ka = seg("ka")
    ka32_ref[...] = ka
    ka16_ref[...] = ka.astype(BF16)
    va = seg("va")
    va32_ref[...] = va
    va16_ref[...] = va.astype(BF16)
    qb_ref[...] = (seg("qb") * HEAD_DIM ** -0.5).astype(BF16)
    kb32_ref[...] = seg("kb")
    vb32_ref[...] = seg("vb")
    kbd_ref[...] = seg("kbd").astype(BF16)
    vx_ref[...] = seg("vx").astype(BF16)
    qi_ref[...] = (seg("qi") * IDX_DIM ** -0.5).astype(BF16)
    kid = seg("kid")
    ki32_ref[...] = kid[:, :IDX_DIM]
    kid_ref[...] = kid.astype(BF16)
    wi_ref[...] = seg("wi") * IDX_HEADS ** -0.5
    a, _ = _SEG["ga"]
    _, b = _SEG["gb"]
    sg_ref[...] = jax.nn.sigmoid(_dot(h, w_ref[:, a:b]) + b_ref[:, a:b])


def _pack_w_in(w_in, b_in):
    offs = np.cumsum((0, 512, 512, 512, 512, 128, 128, 512, 8, 64, 1024, 1024))
    qa, ka, va, qb, kb, vb, qi, wi, ki, ga, gb = [slice(int(offs[i]), int(offs[i + 1])) for i in range(11)]

    def pack(m, fill):
        kb_m, vb_m = m[..., kb], m[..., vb]
        dup = lambda t: jnp.concatenate([t[..., :64], t[..., :64], t[..., 64:], t[..., 64:]], axis=-1)
        wi_m = jnp.concatenate([m[..., wi], jnp.zeros(m.shape[:-1] + (LANES - IDX_HEADS,), m.dtype)], axis=-1)
        return jnp.concatenate([m[..., qa], m[..., ka], m[..., va], m[..., qb], kb_m, vb_m, dup(kb_m),
                                _with_ones(vb_m, fill),
                                m[..., qi], m[..., ki], m[..., ki], wi_m, m[..., ga], m[..., gb]], axis=-1)

    return pack(w_in, 0.0).astype(BF16), pack(b_in.reshape(1, -1), 1.0)


def _with_ones(v, fill=1.0):
    f = jnp.full(v.shape[:-1] + (HEAD_DIM,), fill, v.dtype)
    return jnp.concatenate([v[..., :HEAD_DIM], f, v[..., HEAD_DIM:], f], axis=-1)


def _proj_call(x, mod, w_packed, b_packed, tm):
    s, t, d = x.shape
    nt = t // tm
    row = lambda w: pl.BlockSpec((None, tm, w), lambda b, i: (b, i, 0))
    shp = lambda w, dt: jax.ShapeDtypeStruct((s, t, w), dt)
    outs = [(512, BF16), (512, F32), (512, F32), (512, BF16), (512, BF16),
            (512, BF16), (128, F32), (128, F32), (256, BF16), (256, BF16),
            (512, BF16), (IDX_DIM, F32), (128, BF16), (128, F32), (2048, F32)]
    return pl.pallas_call(
        _proj_kernel,
        grid=(s, nt),
        in_specs=[row(d),
                  pl.BlockSpec((None, 6, d), lambda b, i: (b, 0, 0)),
                  pl.BlockSpec((d, PACKED_COLS), lambda b, i: (0, 0)),
                  pl.BlockSpec((1, PACKED_COLS), lambda b, i: (0, 0))],
        out_specs=[row(w) for w, _ in outs],
        out_shape=[shp(w, dt) for w, dt in outs],
        compiler_params=_cparams(("arbitrary", "arbitrary")),
        name="in_proj",
    )(x, mod, w_packed, b_packed)


def _bias_kernel(tab_ref, o_ref):
    r = lax.broadcasted_iota(I32, (ATT_BLOCK, ATT_BLOCK), 0)
    c = lax.broadcasted_iota(I32, (ATT_BLOCK, ATT_BLOCK), 1)
    half = N_BUCKETS // 2
    max_exact = half // 2
    for j, off in enumerate((0, -ATT_BLOCK, -(1 << 20))):
        rel = c - r + off
        n = jnp.abs(rel)
        large = jnp.full_like(n, max_exact)
        for thr in (12, 16, 23, 32, 46, 64, 91):
            large = large + (n >= thr).astype(I32)
        bucket = jnp.where(rel > 0, half, 0) + jnp.where(n < max_exact, n, large)
        for h in range(DSA_HEADS):
            acc = jnp.zeros((ATT_BLOCK, ATT_BLOCK), F32)
            for b in range(N_BUCKETS):
                acc = jnp.where(bucket == b, tab_ref[b, h], acc)
            o_ref[h, j] = acc


def _bias_call(rel_bias):
    return pl.pallas_call(
        _bias_kernel,
        in_specs=[pl.BlockSpec(memory_space=pltpu.SMEM)],
        out_specs=pl.BlockSpec(memory_space=pltpu.VMEM),
        out_shape=jax.ShapeDtypeStruct((DSA_HEADS, 3, ATT_BLOCK, ATT_BLOCK), F32),
        name="t5_bias_tiles",
    )(rel_bias)


def _softplus(z):
    return jnp.maximum(z, 0.0) + jnp.log1p(jnp.exp(-jnp.abs(z)))


def _sb_kernel(q_ref, k_ref, v_ref, o_ref, *, tq, last_fn):
    tk = ATT_BLOCK
    last = last_fn(pl.program_id(2))
    lane = lax.broadcasted_iota(I32, (1, LANES), 1)
    low = lane < HEAD_DIM
    q = q_ref[...]
    zero = jnp.zeros_like(q)
    qh = (jnp.where(low, q, zero), jnp.where(low, zero, q))
    uj = lax.broadcasted_iota(I32, (tk, tk), 0)
    us = lax.broadcasted_iota(I32, (tk, tk), 1)
    u_mat = jnp.where(uj > us, 1.0, 0.0).astype(BF16)

    def tile(kb, vis, carries):
        start = pl.multiple_of(kb * tk, tk)
        kblk = k_ref[pl.ds(start, tk), :]
        vblk = v_ref[pl.ds(start, tk), :]
        pv = []
        new_carries = []
        for h in range(2):
            z = _dot_nt(qh[h], kblk)
            sp = _softplus(z)
            lk = -sp if vis is None else jnp.where(vis, -sp, 0.0)
            hi, lo = _split_bf16(lk)
            after = _dot(hi, u_mat) + _dot(lo, u_mat)
            w = jnp.exp((z - sp) + (after + carries[h]))
            if vis is not None:
                w = jnp.where(vis, w, 0.0)
            pv.append(_dot(w.astype(BF16), vblk))
            new_carries.append(carries[h] + jnp.sum(lk, axis=1, keepdims=True))
        return jnp.where(low, pv[0], pv[1]), new_carries

    r = lax.broadcasted_iota(I32, (tq, tk), 0)
    c = lax.broadcasted_iota(I32, (tq, tk), 1)
    zc = jnp.zeros((tq, 1), F32)
    acc, carries = tile(last, c < r, [zc, zc])

    def live(carries):
        return jnp.max(jnp.maximum(carries[0], carries[1])) > EXP_ZERO_BELOW

    def cond(st):
        kb, go, _, _, _ = st
        return jnp.logical_and(kb >= 0, go)

    def body(st):
        kb, _, acc, c0, c1 = st
        pv, nc = tile(kb, None, [c0, c1])
        return kb - 1, live(nc), acc + pv, nc[0], nc[1]

    st = lax.while_loop(cond, body, (last - 1, live(carries), acc, carries[0], carries[1]))
    o_ref[...] = st[2].astype(o_ref.dtype)


def _sb_call(q, k, v, tq, last_fn):
    s, t, _ = q.shape
    tkk = k.shape[1]
    return pl.pallas_call(
        functools.partial(_sb_kernel, tq=tq, last_fn=last_fn),
        grid=(s, SB_HEADS // 2, t // tq),
        in_specs=[pl.BlockSpec((None, tq, LANES), lambda b, p, i: (b, i, p)),
                  pl.BlockSpec((None, tkk, LANES), lambda b, p, i: (b, 0, p)),
                  pl.BlockSpec((None, tkk, LANES), lambda b, p, i: (b, 0, p))],
        out_specs=pl.BlockSpec((None, tq, LANES), lambda b, p, i: (b, i, p)),
        out_shape=jax.ShapeDtypeStruct(q.shape, BF16),
        compiler_params=_cparams(("arbitrary", "arbitrary", "arbitrary")),
        name="stick_breaking",
    )(q, k, v)


def _dsa_kernel(qi_ref, wi_ref, qb_ref, ki_ref, kb_ref, vx_ref, bt_ref, o_ref,
                qis_sc, wb_sc, qbs_sc, key_sc, mx_sc, acc_sc, *, tq, topk, last_fn, adm_fn):
    tk = ATT_BLOCK
    group = DSA_HEADS // DSA_KV_HEADS
    last = last_fn(pl.program_id(1))
    lane = lax.broadcasted_iota(I32, (1, LANES), 1)
    low = lane < HEAD_DIM

    for h in range(IDX_HEADS):
        p = h // 2
        t = qi_ref[:, p * LANES:(p + 1) * LANES]
        z = jnp.zeros_like(t)
        qis_sc[h * tq:(h + 1) * tq, :] = jnp.where(low, t, z) if h % 2 == 0 else jnp.where(low, z, t)
        t = qb_ref[:, p * LANES:(p + 1) * LANES]
        g = h % group
        qbs_sc[h // group, g * tq:(g + 1) * tq, :] = jnp.where(low, t, z) if h % 2 == 0 else jnp.where(low, z, t)
        wb_sc[h] = jnp.broadcast_to(wi_ref[:, h:h + 1], (tq, LANES))

    r = lax.broadcasted_iota(I32, (tq, tk), 0)
    c = lax.broadcasted_iota(I32, (tq, tk), 1)
    last_pair = last // 2

    def score_pair(pi, masked):
        start = pl.multiple_of(pi * (2 * tk), 2 * tk)
        d = _dot_nt(qis_sc[...], ki_ref[pl.ds(start, 2 * tk), :])
        for half in range(2):
            s = jnp.zeros((tq, tk), F32)
            for h in range(IDX_HEADS):
                s = s + wb_sc[h] * jnp.maximum(d[h * tq:(h + 1) * tq, half * tk:(half + 1) * tk], 0.0)
            bits = pltpu.bitcast(s, I32)
            key = jnp.where(bits < 0, bits ^ INT32_MAX, bits)
            if masked:
                key = jnp.where(adm_fn(pl.program_id(1), r, c + (start + half * tk)), key, NEG_INF_KEY)
            key_sc[2 * pi + half] = key

    def score_body(pi, carry):
        score_pair(pi, False)
        return carry

    lax.fori_loop(0, last_pair, score_body, 0)
    score_pair(last_pair, True)

    def count_ge(thr):
        def body(pi, acc):
            k0 = key_sc[2 * pi]
            k1 = key_sc[2 * pi + 1]
            return acc + (jnp.where(k0 >= thr, 1.0, 0.0) + jnp.where(k1 >= thr, 1.0, 0.0))
        acc = lax.fori_loop(0, last_pair + 1, body, jnp.zeros((tq, tk), F32))
        return jnp.sum(acc, axis=1, keepdims=True)

    def midpoint(lo, hi):
        return (lo >> 1) + (hi >> 1) + (lo & hi & 1)

    kf = float(topk)

    def bis_cond(st):
        return st[0]

    def is_open(lo, hi, cnt_lo):
        return jnp.logical_and(midpoint(lo, hi) != lo, cnt_lo != kf)

    def bis_body(st):
        _, lo, hi, cnt_lo, cnt_hi = st
        for _ in range(BISECT_STEPS_PER_CHECK):
            mid = midpoint(lo, hi)
            open_ = is_open(lo, hi, cnt_lo)
            cnt = count_ge(mid)
            ge = cnt >= kf
            up = jnp.logical_and(open_, ge)
            dn = jnp.logical_and(open_, jnp.logical_not(ge))
            lo = jnp.where(up, mid, lo)
            cnt_lo = jnp.where(up, cnt, cnt_lo)
            hi = jnp.where(dn, mid, hi)
            cnt_hi = jnp.where(dn, cnt, cnt_hi)
        go = jnp.max(jnp.where(is_open(lo, hi, cnt_lo), 1.0, 0.0)) > 0.0
        return go, lo, hi, cnt_lo, cnt_hi

    col1 = lambda v, dt: jnp.full((tq, 1), v, dt)
    _, thr, _, cnt_lo, cnt_hi = lax.while_loop(
        bis_cond, bis_body,
        (jnp.bool_(True), col1(NEG_INF_KEY, I32), col1(INT32_MAX, I32), col1(-1.0, F32), col1(0.0, F32)))

    tied = cnt_lo > kf

    @pl.when(jnp.max(jnp.where(tied, 1.0, 0.0)) > 0.0)
    def _():
        need = kf - cnt_hi

        def count_tied_before(m):
            def body(kb, acc):
                col = c + kb * tk
                hit = jnp.logical_and(key_sc[kb] == thr, col < m)
                return acc + jnp.where(hit, 1.0, 0.0)
            acc = lax.fori_loop(0, last + 1, body, jnp.zeros((tq, tk), F32))
            return jnp.sum(acc, axis=1, keepdims=True)

        def idx_body(_, st):
            lo2, hi2 = st
            mid2 = (lo2 + hi2) >> 1
            ok = count_tied_before(mid2) >= need
            return jnp.where(ok, lo2, mid2), jnp.where(ok, mid2, hi2)

        n_keys = (last + 1) * tk
        _, mstar = lax.fori_loop(0, 15, idx_body, (col1(0, I32), jnp.full((tq, 1), 0, I32) + n_keys))

        def demote(kb, carry):
            col = c + kb * tk
            key = key_sc[kb]
            drop = jnp.logical_and(tied, jnp.logical_and(key == thr, col >= mstar))
            key_sc[kb] = jnp.where(drop, NEG_INF_KEY, key)
            return carry

        lax.fori_loop(0, last + 1, demote, 0)

    thr_eff = jnp.maximum(thr, NEG_INF_KEY + 1)

    for h in range(DSA_HEADS):
        mx_sc[h] = jnp.full((tq, LANES), MASKED_LOGIT, F32)
        acc_sc[h] = jnp.zeros((tq, LANES), F32)

    def masked_logits(pi, n):
        start = pl.multiple_of(pi * (2 * tk), 2 * tk)
        lg = _dot_nt(qbs_sc[n], kb_ref[pl.ds(start, 2 * tk), n * LANES:(n + 1) * LANES])
        sel = [key_sc[2 * pi + half] >= thr_eff for half in range(2)]
        j = [jnp.clip(last - (2 * pi + half), 0, 2) for half in range(2)]
        out = []
        for g in range(group):
            h = n * group + g
            out.append([jnp.where(sel[half],
                                  lg[g * tq:(g + 1) * tq, half * tk:(half + 1) * tk] + bt_ref[h, j[half], 0:tq, :],
                                  MASKED_LOGIT) for half in range(2)])
        return out

    def max_body(pi, carry):
        for n in range(DSA_KV_HEADS):
            for g, (la, lb) in enumerate(masked_logits(pi, n)):
                h = n * group + g
                mx_sc[h] = jnp.maximum(mx_sc[h], jnp.maximum(la, lb))
        return carry

    lax.fori_loop(0, last_pair + 1, max_body, 0)
    for h in range(DSA_HEADS):
        mx_sc[h] = jnp.broadcast_to(jnp.max(mx_sc[h], axis=1, keepdims=True), (tq, LANES))

    def pv_body(pi, carry):
        start = pl.multiple_of(pi * (2 * tk), 2 * tk)
        for n in range(DSA_KV_HEADS):
            ps = []
            for g, (la, lb) in enumerate(masked_logits(pi, n)):
                m = mx_sc[n * group + g]
                ps.append(jnp.concatenate([jnp.exp(la - m).astype(BF16), jnp.exp(lb - m).astype(BF16)], axis=1))
            pv = _dot(jnp.concatenate(ps, axis=0), vx_ref[pl.ds(start, 2 * tk), n * LANES:(n + 1) * LANES])
            for g in range(group):
                h = n * group + g
                acc_sc[h] = acc_sc[h] + pv[g * tq:(g + 1) * tq]
        return carry

    lax.fori_loop(0, last_pair + 1, pv_body, 0)

    def normalised(h):
        a = acc_sc[h]
        return a / pltpu.roll(a, HEAD_DIM, axis=1)

    for p in range(DSA_HEADS // 2):
        o1 = pltpu.roll(normalised(2 * p + 1), HEAD_DIM, axis=1)
        o_ref[:, p * LANES:(p + 1) * LANES] = jnp.where(low, normalised(2 * p), o1).astype(o_ref.dtype)


def _dsa_call(qi, wi, qb, kid, kbd, vx, btiles, tq, topk, last_fn, adm_fn):
    s, t, _ = qi.shape
    tkk = kid.shape[1]
    assert tkk % (2 * ATT_BLOCK) == 0, "keys must come in whole pairs of blocks"
    nkb = tkk // ATT_BLOCK
    rowq = lambda w: pl.BlockSpec((None, tq, w), lambda b, i: (b, i, 0))
    full = lambda w: pl.BlockSpec((None, tkk, w), lambda b, i: (b, 0, 0))
    return pl.pallas_call(
        functools.partial(_dsa_kernel, tq=tq, topk=topk, last_fn=last_fn, adm_fn=adm_fn),
        grid=(s, t // tq),
        in_specs=[rowq(512), rowq(LANES), rowq(512), full(LANES), full(2 * LANES), full(2 * LANES),
                  pl.BlockSpec(btiles.shape, lambda b, i: (0, 0, 0, 0))],
        out_specs=rowq(512),
        out_shape=jax.ShapeDtypeStruct((s, t, 512), BF16),
        scratch_shapes=[pltpu.VMEM((IDX_HEADS * tq, LANES), BF16),
                        pltpu.VMEM((IDX_HEADS, tq, LANES), F32),
                        pltpu.VMEM((DSA_KV_HEADS, DSA_HEADS // DSA_KV_HEADS * tq, LANES), BF16),
                        pltpu.VMEM((nkb, tq, ATT_BLOCK), I32),
                        pltpu.VMEM((DSA_HEADS, tq, LANES), F32),
                        pltpu.VMEM((DSA_HEADS, tq, LANES), F32)],
        compiler_params=_cparams(("arbitrary", "arbitrary")),
        name="dsa_attention",
    )(qi, wi, qb, kid, kbd, vx, btiles)


def _layer_norm(x, g, b):
    mu = jnp.mean(x, axis=-1, keepdims=True)
    xc = x - mu
    var = jnp.mean(xc * xc, axis=-1, keepdims=True)
    return xc * lax.rsqrt(var + LN_EPS) * g + b


def _post_kernel(oa_ref, ob_ref, sg_ref, x_ref, mod_ref, wsb_ref, wdsa_ref, wout_ref, g_ref, b_ref,
                 wrh_ref, wrl_ref, br_ref, x1_ref, h2_ref, ti_ref, tg_ref):
    d = D_MODEL
    ya = _dot(oa_ref[...], wsb_ref[...])
    yb = _dot(ob_ref[...], wdsa_ref[...])
    merged = sg_ref[:, :d] * ya + sg_ref[:, d:] * yb
    mix = _dot(merged.astype(BF16), wout_ref[...])
    g1 = mod_ref[2:3, :]
    x1 = _layer_norm(DEEPNORM_ALPHA * x_ref[...] + g1 * mix, g_ref[...], b_ref[...])
    x1_ref[...] = x1
    h2 = x1 * (1.0 + mod_ref[4:5, :]) + mod_ref[3:4, :]
    h2_ref[...] = h2
    logits = _dot3(h2, wrh_ref[...], wrl_ref[...]) + br_ref[...]
    lane = lax.broadcasted_iota(I32, logits.shape, 1).astype(F32)
    neg = -jnp.inf
    cur = jnp.where(lane < N_EXPERTS, logits, neg)
    vals, idxs = [], []
    for _ in range(TOP_K_EXPERTS):
        m = jnp.max(cur, axis=1, keepdims=True)
        idx = jnp.min(jnp.where(cur == m, lane, float(LANES)), axis=1, keepdims=True)
        vals.append(m)
        idxs.append(idx)
        cur = jnp.where(lane == idx, neg, cur)
    es = [jnp.exp(v - vals[0]) for v in vals]
    tot = es[0] + es[1] + es[2] + es[3]
    ti = jnp.zeros(logits.shape, F32)
    tg = jnp.zeros(logits.shape, F32)
    for k in range(TOP_K_EXPERTS):
        ti = jnp.where(lane == k, idxs[k], ti)
        tg = jnp.where(lane == k, es[k] / tot, tg)
    ti_ref[...] = ti.astype(I32)
    tg_ref[...] = tg


def _post_call(oa, ob, sg, x, mod, wsb, wdsa, wout, ln_g, ln_b, wr_hi, wr_lo, br, tm):
    s, t, d = x.shape
    row = lambda w: pl.BlockSpec((None, tm, w), lambda b, i: (b, i, 0))
    const = lambda a: pl.BlockSpec(a.shape, lambda b, i: (0,) * a.ndim)
    shp = lambda w, dt: jax.ShapeDtypeStruct((s, t, w), dt)
    return pl.pallas_call(
        _post_kernel,
        grid=(s, t // tm),
        in_specs=[row(512), row(512), row(2 * d), row(d),
                  pl.BlockSpec((None, 6, d), lambda b, i: (b, 0, 0)),
                  const(wsb), const(wdsa), const(wout), const(ln_g), const(ln_b),
                  const(wr_hi), const(wr_lo), const(br)],
        out_specs=[row(d), row(d), row(LANES), row(LANES)],
        out_shape=[shp(d, F32), shp(d, F32), shp(LANES, I32), shp(LANES, F32)],
        compiler_params=_cparams(("arbitrary", "arbitrary")),
        name="post_attention_router",
    )(oa, ob, sg, x, mod, wsb, wdsa, wout, ln_g, ln_b, wr_hi, wr_lo, br)


def _ffn_kernel(be_ref, nused_ref, tok0_ref, tokn_ref, h_hbm, wup_ref, bup_ref, wdn_ref, bdn_ref,
                o_ref, xbuf, sem, *, m):
    i = pl.program_id(0)
    nb = pl.num_programs(0)

    def row_copy(tok, slot, r):
        return pltpu.make_async_copy(h_hbm.at[pl.ds(tok, 1), :], xbuf.at[slot, pl.ds(r, 1), :], sem.at[slot])

    def issue(tok_ref, slot):
        def body(r, carry):
            row_copy(tok_ref[0, r], slot, r).start()
            return carry
        lax.fori_loop(0, m, body, 0, unroll=8)

    slot = i % 2

    @pl.when(i == 0)
    def _():
        issue(tok0_ref, 0)

    @pl.when(i + 1 < nb)
    def _():
        issue(tokn_ref, 1 - slot)

    def wait_body(r, carry):
        row_copy(0, slot, r).wait()
        return carry
    lax.fori_loop(0, m, wait_body, 0, unroll=8)

    @pl.when(i < nused_ref[0])
    def _():
        x = xbuf[slot].astype(BF16)
        u = _dot(x, wup_ref[...]) + bup_ref[...]
        acts = []
        for t in range(2 * D_FF // SWIGLU_TILE):
            a = t * SWIGLU_TILE
            glu = jnp.minimum(u[:, a:a + LANES], SWIGLU_LIMIT)
            lin = jnp.clip(u[:, a + LANES:a + SWIGLU_TILE], -SWIGLU_LIMIT, SWIGLU_LIMIT)
            acts.append((glu * jax.nn.sigmoid(SWIGLU_ALPHA * glu) * (lin + 1.0)).astype(BF16))
        o_ref[...] = _dot(jnp.concatenate(acts, axis=1), wdn_ref[...]) + bdn_ref[...]

    @pl.when(i >= nused_ref[0])
    def _():
        o_ref[...] = jnp.zeros_like(o_ref)


def _deinterleave_kernel(w_ref, o_ref):
    j = lax.broadcasted_iota(I32, (SWIGLU_TILE, SWIGLU_TILE), 0)
    s = lax.broadcasted_iota(I32, (SWIGLU_TILE, SWIGLU_TILE), 1)
    src = jnp.where(s < LANES, 2 * s, 2 * (s - LANES) + 1)
    perm = jnp.where(j == src, 1.0, 0.0).astype(BF16)
    for t in range(w_ref.shape[1] // SWIGLU_TILE):
        a = t * SWIGLU_TILE
        o_ref[:, a:a + SWIGLU_TILE] = _dot(w_ref[:, a:a + SWIGLU_TILE].astype(BF16), perm).astype(BF16)


def _deinterleave_call(w_up):
    e, d, f = w_up.shape
    tr = 512
    return pl.pallas_call(
        _deinterleave_kernel,
        grid=(e, d // tr),
        in_specs=[pl.BlockSpec((None, tr, f), lambda a, b: (a, b, 0))],
        out_specs=pl.BlockSpec((None, tr, f), lambda a, b: (a, b, 0)),
        out_shape=jax.ShapeDtypeStruct((e, d, f), BF16),
        compiler_params=_cparams(("arbitrary", "arbitrary")),
        name="w_up_tiles",
    )(w_up)


def _ffn_call(block_expert, row_tok, nused, h2, wup, bup, wdn, bdn, tile):
    n_rows = row_tok.shape[0]
    n_blocks = n_rows // tile
    d = h2.shape[1]
    tok3 = row_tok.reshape(n_blocks, 1, tile)
    grid_spec = pltpu.PrefetchScalarGridSpec(
        num_scalar_prefetch=2,
        grid=(n_blocks,),
        in_specs=[pl.BlockSpec((None, 1, tile), lambda i, be, nu: (0, 0, 0), memory_space=pltpu.SMEM),
                  pl.BlockSpec((None, 1, tile), lambda i, be, nu: (jnp.minimum(i + 1, n_blocks - 1), 0, 0),
                               memory_space=pltpu.SMEM),
                  pl.BlockSpec(memory_space=pl.ANY),
                  pl.BlockSpec((None, d, 2 * D_FF), lambda i, be, nu: (be[i], 0, 0)),
                  pl.BlockSpec((None, 1, 2 * D_FF), lambda i, be, nu: (be[i], 0, 0)),
                  pl.BlockSpec((None, D_FF, d), lambda i, be, nu: (be[i], 0, 0)),
                  pl.BlockSpec((None, 1, d), lambda i, be, nu: (be[i], 0, 0))],
        out_specs=pl.BlockSpec((tile, d), lambda i, be, nu: (i, 0)),
        scratch_shapes=[pltpu.VMEM((2, tile, d), F32), pltpu.SemaphoreType.DMA((2,))],
    )
    return pl.pallas_call(
        functools.partial(_ffn_kernel, m=tile),
        grid_spec=grid_spec,
        out_shape=jax.ShapeDtypeStruct((n_rows, d), F32),
        compiler_params=_cparams(("arbitrary",)),
        name="expert_ffn",
    )(block_expert, nused, tok3, tok3, h2, wup, bup, wdn, bdn)


def _combine_kernel(pos0_ref, posn_ref, rows_hbm, x1_ref, tg_ref, mod_ref, g_ref, b_ref, o_ref, buf, sem, *, tm):
    i = pl.program_id(0)
    nb = pl.num_programs(0)
    k = TOP_K_EXPERTS

    def row_copy(src, slot, r, j):
        return pltpu.make_async_copy(rows_hbm.at[pl.ds(src, 1), :], buf.at[slot, j, pl.ds(r, 1), :], sem.at[slot])

    def issue(pos_ref, slot):
        def body(r, carry):
            for j in range(k):
                row_copy(pos_ref[0, r * k + j], slot, r, j).start()
            return carry
        lax.fori_loop(0, tm, body, 0, unroll=4)

    slot = i % 2

    @pl.when(i == 0)
    def _():
        issue(pos0_ref, 0)

    @pl.when(i + 1 < nb)
    def _():
        issue(posn_ref, 1 - slot)

    def wait_body(r, carry):
        for j in range(k):
            row_copy(0, slot, r, j).wait()
        return carry
    lax.fori_loop(0, tm, wait_body, 0, unroll=4)

    gated = [buf[slot, j] * tg_ref[:, j:j + 1] for j in range(k)]
    y = (gated[0] + gated[1]) + (gated[2] + gated[3])
    o_ref[...] = _layer_norm(DEEPNORM_ALPHA * x1_ref[...] + mod_ref[5:6, :] * y, g_ref[...], b_ref[...])


def _combine_call(pos, rows, x1, tg, mod, seq_len, ln_g, ln_b, tm):
    n, d = x1.shape
    nb = n // tm
    per_seq = seq_len // tm
    pos3 = pos.reshape(nb, 1, tm * TOP_K_EXPERTS)
    return pl.pallas_call(
        functools.partial(_combine_kernel, tm=tm),
        grid=(nb,),
        in_specs=[pl.BlockSpec((None, 1, tm * TOP_K_EXPERTS), lambda i: (0, 0, 0), memory_space=pltpu.SMEM),
                  pl.BlockSpec((None, 1, tm * TOP_K_EXPERTS), lambda i: (jnp.minimum(i + 1, nb - 1), 0, 0),
                               memory_space=pltpu.SMEM),
                  pl.BlockSpec(memory_space=pl.ANY),
                  pl.BlockSpec((tm, d), lambda i: (i, 0)),
                  pl.BlockSpec((tm, LANES), lambda i: (i, 0)),
                  pl.BlockSpec((None, 6, d), lambda i: (i // per_seq, 0, 0)),
                  pl.BlockSpec((1, d), lambda i: (0, 0)),
                  pl.BlockSpec((1, d), lambda i: (0, 0))],
        out_specs=pl.BlockSpec((tm, d), lambda i: (i, 0)),
        out_shape=jax.ShapeDtypeStruct((n, d), F32),
        scratch_shapes=[pltpu.VMEM((2, TOP_K_EXPERTS, tm, d), F32), pltpu.SemaphoreType.DMA((2,))],
        compiler_params=_cparams(("arbitrary",)),
        name="moe_combine_ln2",
    )(pos3, pos3, rows, x1, tg, mod, ln_g, ln_b)


def _routing(top_idx, tile):
    n_tok = top_idx.shape[0]
    n_assign = n_tok * TOP_K_EXPERTS
    e_flat = top_idx.reshape(-1)
    onehot = (e_flat[:, None] == jnp.arange(N_EXPERTS, dtype=I32)[None, :]).astype(I32)
    csum = jnp.cumsum(onehot, axis=0)
    rank = jnp.sum(onehot * csum, axis=1) - 1
    counts = csum[-1]
    padded = (counts + tile - 1) // tile * tile
    pend = jnp.cumsum(padded)
    pstart = pend - padded
    dest = (pstart[e_flat] + rank).astype(I32)
    n_rows = (n_assign + N_EXPERTS * (tile - 1) + tile - 1) // tile * tile
    n_blocks = n_rows // tile
    tok_flat = jnp.repeat(jnp.arange(n_tok, dtype=I32), TOP_K_EXPERTS)
    row_tok = jnp.zeros((n_rows,), I32).at[dest].set(tok_flat)
    block_expert = jnp.minimum(
        jnp.searchsorted(pend, jnp.arange(n_blocks, dtype=I32) * tile, side="right"), N_EXPERTS - 1).astype(I32)
    nused = (pend[-1] // tile).astype(I32).reshape(1)
    return dest, row_tok, block_expert, nused


def _stream(x, mod, kv_cache, weights, btiles, tm, tq):
    (w_packed, b_packed, wsb, wdsa, wout, ln1_g, ln1_b, wr_hi, wr_lo, br,
     wup, bup, wdn, bdn, ln2_g, ln2_b) = weights
    s, t, d = x.shape
    (qa, ka32, va32, ka16, va16, qb, kb32, vb32, kbd, vx, qi, ki32, kid, wi, sg) = _proj_call(x, mod, w_packed, b_packed, tm)

    if kv_cache is None:
        k_sb, v_sb, k_id, k_bd, v_x = ka16, va16, kid, kbd, vx
        total = t
        last_fn = lambda i: i
        adm_fn = lambda i, r, key_pos: key_pos < i * tq + (r // CHUNK + 1) * CHUNK
    else:
        past = kv_cache[0].shape[1]
        total = past + t
        pad = (-total) % (2 * ATT_BLOCK)
        cat = lambda cache, new: jnp.concatenate(
            [cache, new, jnp.zeros((s, pad, new.shape[2]), new.dtype)], axis=1)
        k_sb, v_sb, k_id, k_bd, v_x = [cat(c_, n_) for c_, n_ in zip(kv_cache, (ka16, va16, kid, kbd, vx))]
        last_blk = past // ATT_BLOCK
        last_fn = lambda i: i * 0 + last_blk
        adm_fn = lambda i, r, key_pos: key_pos < total
    topk = max(1, min(TOPK_MAX, total // 4))

    oa = _sb_call(qa, k_sb, v_sb, tq, last_fn)
    ob = _dsa_call(qi, wi, qb, k_id, k_bd, v_x, btiles, tq, topk, last_fn, adm_fn)
    x1, h2, ti, tg = _post_call(oa, ob, sg, x, mod, wsb, wdsa, wout, ln1_g, ln1_b, wr_hi, wr_lo, br, tm)

    n = s * t
    tile = 4 * MOE_BLOCK if n * TOP_K_EXPERTS >= N_EXPERTS * 8 * MOE_BLOCK else MOE_BLOCK
    dest, row_tok, block_expert, nused = _routing(ti.reshape(n, LANES)[:, :TOP_K_EXPERTS], tile)
    rows = _ffn_call(block_expert, row_tok, nused, h2.reshape(n, d), wup, bup, wdn, bdn, tile)
    y = _combine_call(dest, rows, x1.reshape(n, d), tg.reshape(n, LANES), mod, t, ln2_g, ln2_b, min(128, t))
    new_rows = (ka32.reshape(1, s, t, SB_HEADS, HEAD_DIM), va32.reshape(1, s, t, SB_HEADS, HEAD_DIM),
                kb32.reshape(1, s, t, DSA_KV_HEADS, HEAD_DIM), vb32.reshape(1, s, t, DSA_KV_HEADS, HEAD_DIM),
                ki32.reshape(1, s, t, IDX_DIM))
    return y.reshape(s, t, d), new_rows


def kernel(x_prompt, x_sample, cache_sb_k, cache_sb_v, cache_dsa_k, cache_dsa_v, cache_idx_k, c_prompt, c_sample, rel_bias, w_ada, b_ada, w_in, b_in, w_o_sb, w_o_dsa, w_out, ln1_g, ln1_b, w_router, b_router, w_up, b_up, w_down, b_down, ln2_g, ln2_b):
    d = D_MODEL
    nb, ns = x_prompt.shape[0], x_sample.shape[0]
    past = cache_sb_k.shape[2]

    mod = _mod_call(jnp.concatenate([c_prompt, c_sample], axis=0), w_ada[0], b_ada[0]).reshape(nb + ns, 6, d)
    btiles = _bias_call(rel_bias)

    w_packed, b_packed = _pack_w_in(w_in[0], b_in[0])
    wr = jnp.concatenate([w_router[0], jnp.zeros((d, LANES - N_EXPERTS), F32)], axis=1)
    wr_hi = wr.astype(BF16)
    wr_lo = (wr - wr_hi.astype(F32)).astype(BF16)
    br = jnp.concatenate([b_router[0], jnp.zeros((LANES - N_EXPERTS,), F32)]).reshape(1, LANES)
    wup = _deinterleave_call(w_up[0])
    bup = b_up[0].reshape(N_EXPERTS, 2 * D_FF // SWIGLU_TILE, LANES, 2).swapaxes(2, 3).reshape(N_EXPERTS, 1, 2 * D_FF)
    weights = (w_packed, b_packed, w_o_sb[0].astype(BF16), w_o_dsa[0].astype(BF16), w_out[0].astype(BF16),
               ln1_g[0].reshape(1, d), ln1_b[0].reshape(1, d), wr_hi, wr_lo, br,
               wup, bup, w_down[0].astype(BF16), b_down[0].reshape(N_EXPERTS, 1, d),
               ln2_g[0].reshape(1, d), ln2_b[0].reshape(1, d))

    dup = lambda a: jnp.concatenate([a[..., :64], a[..., :64], a[..., 64:], a[..., 64:]], axis=-1)
    idx_c = cache_idx_k[0]
    caches = (cache_sb_k[0].reshape(ns, past, SB_HEADS * HEAD_DIM).astype(BF16),
              cache_sb_v[0].reshape(ns, past, SB_HEADS * HEAD_DIM).astype(BF16),
              jnp.concatenate([idx_c, idx_c], axis=-1).astype(BF16),
              dup(cache_dsa_k[0].reshape(ns, past, DSA_KV_HEADS * HEAD_DIM)).astype(BF16),
              _with_ones(cache_dsa_v[0].reshape(ns, past, DSA_KV_HEADS * HEAD_DIM)).astype(BF16))

    t_p, t_s = x_prompt.shape[1], x_sample.shape[1]
    y_p, new_p = _stream(x_prompt, mod[:nb], None, weights, btiles, min(256, t_p), ATT_BLOCK)
    y_s, new_s = _stream(x_sample, mod[nb:], caches, weights, btiles, t_s, t_s)
    return (y_p, y_s) + new_p + new_s
```

```python
import functools

import jax
import jax.numpy as jnp
import numpy as np
from jax import lax
from jax.experimental import pallas as pl
from jax.experimental.pallas import tpu as pltpu

F32 = jnp.float32
BF16 = jnp.bfloat16
I32 = jnp.int32

D_MODEL = 1024
CHUNK = 64
SB_HEADS = 8
HEAD_DIM = 64
DSA_HEADS = 8
DSA_KV_HEADS = 2
IDX_HEADS = 8
IDX_DIM = 64
TOPK_MAX = 256
N_BUCKETS = 32
N_EXPERTS = 32
TOP_K_EXPERTS = 4
D_FF = 1024
SWIGLU_LIMIT = 7.0
SWIGLU_ALPHA = 1.702
MOE_BLOCK = 128
LN_EPS = 1e-5
DEPTH = 1
DEEPNORM_ALPHA = (2.0 * DEPTH) ** 0.25

LANES = 128
ATT_BLOCK = 128
SB_BLOCK = 256
VMEM_LIMIT = 56 * 1024 * 1024

EXP_ZERO_BELOW = -104.0
NEG_INF_KEY = -2139095041
INT32_MAX = 2147483647
MASKED_LOGIT = -1e30
BISECT_STEPS_PER_CHECK = 2
SWIGLU_TILE = 2 * LANES

_SEG = {}
_off = 0
for _name, _w in (("qa", 512), ("ka", 512), ("va", 512), ("qb", 512), ("kb", 128), ("vb", 128),
                  ("kbd", 256), ("vx", 256), ("qi", 512), ("kid", 128), ("wi", 128),
                  ("ga", 1024), ("gb", 1024)):
    _SEG[_name] = (_off, _off + _w)
    _off += _w
PACKED_COLS = _off


def _cparams(sem):
    return pltpu.CompilerParams(dimension_semantics=sem, vmem_limit_bytes=VMEM_LIMIT)


def _dot(a, b):
    return jnp.dot(a, b, preferred_element_type=F32)


def _dot_nt(a, b):
    return lax.dot_general(a, b, (((1,), (1,)), ((), ())), preferred_element_type=F32)


def _split_bf16(x):
    hi = x.astype(BF16)
    lo = (x - hi.astype(F32)).astype(BF16)
    return hi, lo


def _dot3(a, b_hi, b_lo):
    a_hi, a_lo = _split_bf16(a)
    return _dot(a_hi, b_hi) + (_dot(a_hi, b_lo) + _dot(a_lo, b_hi))


def _mod_kernel(c_ref, w_ref, b_ref, o_ref):
    c = c_ref[...]
    s = c * jax.nn.sigmoid(c)
    w_hi, w_lo = _split_bf16(w_ref[...])
    o_ref[...] = _dot3(s, w_hi, w_lo) + b_ref[...]


def _mod_call(c_all, w_ada, b_ada):
    n, d = c_all.shape
    cols = w_ada.shape[1]
    tn = 1024
    return pl.pallas_call(
        _mod_kernel,
        grid=(cols // tn,),
        in_specs=[pl.BlockSpec((n, d), lambda j: (0, 0)),
                  pl.BlockSpec((d, tn), lambda j: (0, j)),
                  pl.BlockSpec((1, tn), lambda j: (0, j))],
        out_specs=pl.BlockSpec((n, tn), lambda j: (0, j)),
        out_shape=jax.ShapeDtypeStruct((n, cols), F32),
        compiler_params=_cparams(("arbitrary",)),
        name="adaln_mod",
    )(c_all, w_ada, b_ada.reshape(1, cols))


def _proj_kernel(x_ref, mod_ref, w_ref, b_ref,
                 qa_ref, ka32_ref, va32_ref, ka16_ref, va16_ref,
                 qb_ref, kb32_ref, vb32_ref, kbd_ref, vx_ref,
                 qi_ref, ki32_ref, kid_ref, wi_ref, sg_ref):
    sh1 = mod_ref[0:1, :]
    sc1 = mod_ref[1:2, :]
    h = (x_ref[...] * (1.0 + sc1) + sh1).astype(BF16)

    def seg(name):
        a, b = _SEG[name]
        return _dot(h, w_ref[:, a:b]) + b_ref[:, a:b]

    qa_ref[...] = (seg("qa") * HEAD_DIM ** -0.5).astype(BF16)
    ka = seg("ka")
    ka32_ref[...] = ka
    ka16_ref[...] = ka.astype(BF16)
    va = seg("va")
    va32_ref[...] = va
    va16_ref[...] = va.astype(BF16)
    qb_ref[...] = (seg("qb") * HEAD_DIM ** -0.5).astype(BF16)
    kb32_ref[...] = seg("kb")
    vb32_ref[...] = seg("vb")
    kbd_ref[...] = seg("kbd").astype(BF16)
    vx_ref[...] = seg("vx").astype(BF16)
    qi_ref[...] = (seg("qi") * IDX_DIM ** -0.5).astype(BF16)
    kid = seg("kid")
    ki32_ref[...] = kid[:, :IDX_DIM]
    kid_ref[...] = kid.astype(BF16)
    wi_ref[...] = seg("wi") * IDX_HEADS ** -0.5
    a, _ = _SEG["ga"]
    _, b = _SEG["gb"]
    sg_ref[...] = jax.nn.sigmoid(_dot(h, w_ref[:, a:b]) + b_ref[:, a:b])


def _pack_w_in(w_in, b_in):
    offs = np.cumsum((0, 512, 512, 512, 512, 128, 128, 512, 8, 64, 1024, 1024))
    qa, ka, va, qb, kb, vb, qi, wi, ki, ga, gb = [slice(int(offs[i]), int(offs[i + 1])) for i in range(11)]

    def pack(m, fill):
        kb_m, vb_m = m[..., kb], m[..., vb]
        dup = lambda t: jnp.concatenate([t[..., :64], t[..., :64], t[..., 64:], t[..., 64:]], axis=-1)
        wi_m = jnp.concatenate([m[..., wi], jnp.zeros(m.shape[:-1] + (LANES - IDX_HEADS,), m.dtype)], axis=-1)
        return jnp.concatenate([m[..., qa], m[..., ka], m[..., va], m[..., qb], kb_m, vb_m, dup(kb_m),
                                _with_ones(vb_m, fill),
                                m[..., qi], m[..., ki], m[..., ki], wi_m, m[..., ga], m[..., gb]], axis=-1)

    return pack(w_in, 0.0).astype(BF16), pack(b_in.reshape(1, -1), 1.0)


def _with_ones(v, fill=1.0):
    f = jnp.full(v.shape[:-1] + (HEAD_DIM,), fill, v.dtype)
    return jnp.concatenate([v[..., :HEAD_DIM], f, v[..., HEAD_DIM:], f], axis=-1)


def _proj_call(x, mod, w_packed, b_packed, tm):
    s, t, d = x.shape
    nt = t // tm
    row = lambda w: pl.BlockSpec((None, tm, w), lambda b, i: (b, i, 0))
    shp = lambda w, dt: jax.ShapeDtypeStruct((s, t, w), dt)
    outs = [(512, BF16), (512, F32), (512, F32), (512, BF16), (512, BF16),
            (512, BF16), (128, F32), (128, F32), (256, BF16), (256, BF16),
            (512, BF16), (IDX_DIM, F32), (128, BF16), (128, F32), (2048, F32)]
    return pl.pallas_call(
        _proj_kernel,
        grid=(s, nt),
        in_specs=[row(d),
                  pl.BlockSpec((None, 6, d), lambda b, i: (b, 0, 0)),
                  pl.BlockSpec((d, PACKED_COLS), lambda b, i: (0, 0)),
                  pl.BlockSpec((1, PACKED_COLS), lambda b, i: (0, 0))],
        out_specs=[row(w) for w, _ in outs],
        out_shape=[shp(w, dt) for w, dt in outs],
        compiler_params=_cparams(("arbitrary", "arbitrary")),
        name="in_proj",
    )(x, mod, w_packed, b_packed)


def _bias_kernel(tab_ref, o_ref):
    r = lax.broadcasted_iota(I32, (ATT_BLOCK, ATT_BLOCK), 0)
    c = lax.broadcasted_iota(I32, (ATT_BLOCK, ATT_BLOCK), 1)
    half = N_BUCKETS // 2
    max_exact = half // 2
    for j, off in enumerate((0, -ATT_BLOCK, -(1 << 20))):
        rel = c - r + off
        n = jnp.abs(rel)
        large = jnp.full_like(n, max_exact)
        for thr in (12, 16, 23, 32, 46, 64, 91):
            large = large + (n >= thr).astype(I32)
        bucket = jnp.where(rel > 0, half, 0) + jnp.where(n < max_exact, n, large)
        for h in range(DSA_HEADS):
            acc = jnp.zeros((ATT_BLOCK, ATT_BLOCK), F32)
            for b in range(N_BUCKETS):
                acc = jnp.where(bucket == b, tab_ref[b, h], acc)
            o_ref[h, j] = acc


def _bias_call(rel_bias):
    return pl.pallas_call(
        _bias_kernel,
        in_specs=[pl.BlockSpec(memory_space=pltpu.SMEM)],
        out_specs=pl.BlockSpec(memory_space=pltpu.VMEM),
        out_shape=jax.ShapeDtypeStruct((DSA_HEADS, 3, ATT_BLOCK, ATT_BLOCK), F32),
        name="t5_bias_tiles",
    )(rel_bias)


def _softplus(z):
    return jnp.maximum(z, 0.0) + jnp.log1p(jnp.exp(-jnp.abs(z)))


def _sb_kernel(q_ref, k_ref, v_ref, o_ref, *, tq, last_fn):
    tk = SB_BLOCK
    last = last_fn(pl.program_id(2))
    lane = lax.broadcasted_iota(I32, (1, LANES), 1)
    low = lane < HEAD_DIM
    q = q_ref[...]
    zero = jnp.zeros_like(q)
    qh = (jnp.where(low, q, zero), jnp.where(low, zero, q))
    uj = lax.broadcasted_iota(I32, (tk, tk), 0)
    us = lax.broadcasted_iota(I32, (tk, tk), 1)
    u_mat = jnp.where(uj > us, 1.0, 0.0).astype(BF16)

    def tile(kb, vis, carries):
        start = pl.multiple_of(kb * tk, tk)
        kblk = k_ref[pl.ds(start, tk), :]
        vblk = v_ref[pl.ds(start, tk), :]
        pv = []
        new_carries = []
        for h in range(2):
            z = _dot_nt(qh[h], kblk)
            sp = _softplus(z)
            lk = -sp if vis is None else jnp.where(vis, -sp, 0.0)
            hi, lo = _split_bf16(lk)
            after = _dot(hi, u_mat) + _dot(lo, u_mat)
            w = jnp.exp((z - sp) + (after + carries[h]))
            if vis is not None:
                w = jnp.where(vis, w, 0.0)
            pv.append(_dot(w.astype(BF16), vblk))
            new_carries.append(carries[h] + jnp.sum(lk, axis=1, keepdims=True))
        return jnp.where(low, pv[0], pv[1]), new_carries

    r = lax.broadcasted_iota(I32, (tq, tk), 0)
    c = lax.broadcasted_iota(I32, (tq, tk), 1)
    zc = jnp.zeros((tq, 1), F32)
    acc, carries = tile(last, c < r, [zc, zc])

    def live(carries):
        return jnp.max(jnp.maximum(carries[0], carries[1])) > EXP_ZERO_BELOW

    def cond(st):
        kb, go, _, _, _ = st
        return jnp.logical_and(kb >= 0, go)

    def body(st):
        kb, _, acc, c0, c1 = st
        pv, nc = tile(kb, None, [c0, c1])
        return kb - 1, live(nc), acc + pv, nc[0], nc[1]

    st = lax.while_loop(cond, body, (last - 1, live(carries), acc, carries[0], carries[1]))
    o_ref[...] = st[2].astype(o_ref.dtype)


def _sb_call(q, k, v, tq, last_fn):
    s, t, _ = q.shape
    tkk = k.shape[1]
    return pl.pallas_call(
        functools.partial(_sb_kernel, tq=tq, last_fn=last_fn),
        grid=(s, SB_HEADS // 2, t // tq),
        in_specs=[pl.BlockSpec((None, tq, LANES), lambda b, p, i: (b, i, p)),
                  pl.BlockSpec((None, tkk, LANES), lambda b, p, i: (b, 0, p)),
                  pl.BlockSpec((None, tkk, LANES), lambda b, p, i: (b, 0, p))],
        out_specs=pl.BlockSpec((None, tq, LANES), lambda b, p, i: (b, i, p)),
        out_shape=jax.ShapeDtypeStruct(q.shape, BF16),
        compiler_params=_cparams(("arbitrary", "arbitrary", "arbitrary")),
        name="stick_breaking",
    )(q, k, v)


def _dsa_kernel(qi_ref, wi_ref, qb_ref, ki_ref, kb_ref, vx_ref, bt_ref, o_ref,
                qis_sc, wb_sc, qbs_sc, key_sc, mx_sc, acc_sc, *, tq, topk, last_fn, adm_fn):
    tk = ATT_BLOCK
    group = DSA_HEADS // DSA_KV_HEADS
    last = last_fn(pl.program_id(1))
    lane = lax.broadcasted_iota(I32, (1, LANES), 1)
    low = lane < HEAD_DIM

    for h in range(IDX_HEADS):
        p = h // 2
        t = qi_ref[:, p * LANES:(p + 1) * LANES]
        z = jnp.zeros_like(t)
        qis_sc[h * tq:(h + 1) * tq, :] = jnp.where(low, t, z) if h % 2 == 0 else jnp.where(low, z, t)
        t = qb_ref[:, p * LANES:(p + 1) * LANES]
        g = h % group
        qbs_sc[h // group, g * tq:(g + 1) * tq, :] = jnp.where(low, t, z) if h % 2 == 0 else jnp.where(low, z, t)
        wb_sc[h] = jnp.broadcast_to(wi_ref[:, h:h + 1], (tq, LANES))

    r = lax.broadcasted_iota(I32, (tq, tk), 0)
    c = lax.broadcasted_iota(I32, (tq, tk), 1)
    last_pair = last // 2

    def score_pair(pi, masked):
        start = pl.multiple_of(pi * (2 * tk), 2 * tk)
        d = _dot_nt(qis_sc[...], ki_ref[pl.ds(start, 2 * tk), :])
        for half in range(2):
            s = jnp.zeros((tq, tk), F32)
            for h in range(IDX_HEADS):
                s = s + wb_sc[h] * jnp.maximum(d[h * tq:(h + 1) * tq, half * tk:(half + 1) * tk], 0.0)
            bits = pltpu.bitcast(s, I32)
            key = jnp.where(bits < 0, bits ^ INT32_MAX, bits)
            if masked:
                key = jnp.where(adm_fn(pl.program_id(1), r, c + (start + half * tk)), key, NEG_INF_KEY)
            key_sc[2 * pi + half] = key

    def score_body(pi, carry):
        score_pair(pi, False)
        return carry

    lax.fori_loop(0, last_pair, score_body, 0)
    score_pair(last_pair, True)

    def count_ge(thr):
        def body(pi, acc):
            k0 = key_sc[2 * pi]
            k1 = key_sc[2 * pi + 1]
            return acc + (jnp.where(k0 >= thr, 1.0, 0.0) + jnp.where(k1 >= thr, 1.0, 0.0))
        acc = lax.fori_loop(0, last_pair + 1, body, jnp.zeros((tq, tk), F32))
        return jnp.sum(acc, axis=1, keepdims=True)

    def midpoint(lo, hi):
        return (lo >> 1) + (hi >> 1) + (lo & hi & 1)

    kf = float(topk)

    def bis_cond(st):
        return st[0]

    def is_open(lo, hi, cnt_lo):
        return jnp.logical_and(midpoint(lo, hi) != lo, cnt_lo != kf)

    def bis_body(st):
        _, lo, hi, cnt_lo, cnt_hi = st
        for _ in range(BISECT_STEPS_PER_CHECK):
            mid = midpoint(lo, hi)
            open_ = is_open(lo, hi, cnt_lo)
            cnt = count_ge(mid)
            ge = cnt >= kf
            up = jnp.logical_and(open_, ge)
            dn = jnp.logical_and(open_, jnp.logical_not(ge))
            lo = jnp.where(up, mid, lo)
            cnt_lo = jnp.where(up, cnt, cnt_lo)
            hi = jnp.where(dn, mid, hi)
            cnt_hi = jnp.where(dn, cnt, cnt_hi)
        go = jnp.max(jnp.where(is_open(lo, hi, cnt_lo), 1.0, 0.0)) > 0.0
        return go, lo, hi, cnt_lo, cnt_hi

    col1 = lambda v, dt: jnp.full((tq, 1), v, dt)
    _, thr, _, cnt_lo, cnt_hi = lax.while_loop(
        bis_cond, bis_body,
        (jnp.bool_(True), col1(NEG_INF_KEY, I32), col1(INT32_MAX, I32), col1(-1.0, F32), col1(0.0, F32)))

    tied = cnt_lo > kf

    @pl.when(jnp.max(jnp.where(tied, 1.0, 0.0)) > 0.0)
    def _():
        need = kf - cnt_hi

        def count_tied_before(m):
            def body(kb, acc):
                col = c + kb * tk
                hit = jnp.logical_and(key_sc[kb] == thr, col < m)
                return acc + jnp.where(hit, 1.0, 0.0)
            acc = lax.fori_loop(0, last + 1, body, jnp.zeros((tq, tk), F32))
            return jnp.sum(acc, axis=1, keepdims=True)

        def idx_body(_, st):
            lo2, hi2 = st
            mid2 = (lo2 + hi2) >> 1
            ok = count_tied_before(mid2) >= need
            return jnp.where(ok, lo2, mid2), jnp.where(ok, mid2, hi2)

        n_keys = (last + 1) * tk
        _, mstar = lax.fori_loop(0, 15, idx_body, (col1(0, I32), jnp.full((tq, 1), 0, I32) + n_keys))

        def demote(kb, carry):
            col = c + kb * tk
            key = key_sc[kb]
            drop = jnp.logical_and(tied, jnp.logical_and(key == thr, col >= mstar))
            key_sc[kb] = jnp.where(drop, NEG_INF_KEY, key)
            return carry

        lax.fori_loop(0, last + 1, demote, 0)

    thr_eff = jnp.maximum(thr, NEG_INF_KEY + 1)

    for h in range(DSA_HEADS):
        mx_sc[h] = jnp.full((tq, LANES), MASKED_LOGIT, F32)
        acc_sc[h] = jnp.zeros((tq, LANES), F32)

    def masked_logits(pi, n):
        start = pl.multiple_of(pi * (2 * tk), 2 * tk)
        lg = _dot_nt(qbs_sc[n], kb_ref[pl.ds(start, 2 * tk), n * LANES:(n + 1) * LANES])
        sel = [key_sc[2 * pi + half] >= thr_eff for half in range(2)]
        j = [jnp.clip(last - (2 * pi + half), 0, 2) for half in range(2)]
        out = []
        for g in range(group):
            h = n * group + g
            out.append([jnp.where(sel[half],
                                  lg[g * tq:(g + 1) * tq, half * tk:(half + 1) * tk] + bt_ref[h, j[half], 0:tq, :],
                                  MASKED_LOGIT) for half in range(2)])
        return out

    def max_body(pi, carry):
        for n in range(DSA_KV_HEADS):
            for g, (la, lb) in enumerate(masked_logits(pi, n)):
                h = n * group + g
                mx_sc[h] = jnp.maximum(mx_sc[h], jnp.maximum(la, lb))
        return carry

    lax.fori_loop(0, last_pair + 1, max_body, 0)
    for h in range(DSA_HEADS):
        mx_sc[h] = jnp.broadcast_to(jnp.max(mx_sc[h], axis=1, keepdims=True), (tq, LANES))

    def pv_body(pi, carry):
        start = pl.multiple_of(pi * (2 * tk), 2 * tk)
        for n in range(DSA_KV_HEADS):
            ps = []
            for g, (la, lb) in enumerate(masked_logits(pi, n)):
                m = mx_sc[n * group + g]
                ps.append(jnp.concatenate([jnp.exp(la - m).astype(BF16), jnp.exp(lb - m).astype(BF16)], axis=1))
            pv = _dot(jnp.concatenate(ps, axis=0), vx_ref[pl.ds(start, 2 * tk), n * LANES:(n + 1) * LANES])
            for g in range(group):
                h = n * group + g
                acc_sc[h] = acc_sc[h] + pv[g * tq:(g + 1) * tq]
        return carry

    lax.fori_loop(0, last_pair + 1, pv_body, 0)

    def normalised(h):
        a = acc_sc[h]
        return a / pltpu.roll(a, HEAD_DIM, axis=1)

    for p in range(DSA_HEADS // 2):
        o1 = pltpu.roll(normalised(2 * p + 1), HEAD_DIM, axis=1)
        o_ref[:, p * LANES:(p + 1) * LANES] = jnp.where(low, normalised(2 * p), o1).astype(o_ref.dtype)


def _dsa_call(qi, wi, qb, kid, kbd, vx, btiles, tq, topk, last_fn, adm_fn):
    s, t, _ = qi.shape
    tkk = kid.shape[1]
    assert tkk % (2 * ATT_BLOCK) == 0, "keys must come in whole pairs of blocks"
    nkb = tkk // ATT_BLOCK
    rowq = lambda w: pl.BlockSpec((None, tq, w), lambda b, i: (b, i, 0))
    full = lambda w: pl.BlockSpec((None, tkk, w), lambda b, i: (b, 0, 0))
    return pl.pallas_call(
        functools.partial(_dsa_kernel, tq=tq, topk=topk, last_fn=last_fn, adm_fn=adm_fn),
        grid=(s, t // tq),
        in_specs=[rowq(512), rowq(LANES), rowq(512), full(LANES), full(2 * LANES), full(2 * LANES),
                  pl.BlockSpec(btiles.shape, lambda b, i: (0, 0, 0, 0))],
        out_specs=rowq(512),
        out_shape=jax.ShapeDtypeStruct((s, t, 512), BF16),
        scratch_shapes=[pltpu.VMEM((IDX_HEADS * tq, LANES), BF16),
                        pltpu.VMEM((IDX_HEADS, tq, LANES), F32),
                        pltpu.VMEM((DSA_KV_HEADS, DSA_HEADS // DSA_KV_HEADS * tq, LANES), BF16),
                        pltpu.VMEM((nkb, tq, ATT_BLOCK), I32),
                        pltpu.VMEM((DSA_HEADS, tq, LANES), F32),
                        pltpu.VMEM((DSA_HEADS, tq, LANES), F32)],
        compiler_params=_cparams(("arbitrary", "arbitrary")),
        name="dsa_attention",
    )(qi, wi, qb, kid, kbd, vx, btiles)


def _layer_norm(x, g, b):
    mu = jnp.mean(x, axis=-1, keepdims=True)
    xc = x - mu
    var = jnp.mean(xc * xc, axis=-1, keepdims=True)
    return xc * lax.rsqrt(var + LN_EPS) * g + b


def _post_kernel(oa_ref, ob_ref, sg_ref, x_ref, mod_ref, wsb_ref, wdsa_ref, wout_ref, g_ref, b_ref,
                 wrh_ref, wrl_ref, br_ref, x1_ref, h2_ref, ti_ref, tg_ref):
    d = D_MODEL
    ya = _dot(oa_ref[...], wsb_ref[...])
    yb = _dot(ob_ref[...], wdsa_ref[...])
    merged = sg_ref[:, :d] * ya + sg_ref[:, d:] * yb
    mix = _dot(merged.astype(BF16), wout_ref[...])
    g1 = mod_ref[2:3, :]
    x1 = _layer_norm(DEEPNORM_ALPHA * x_ref[...] + g1 * mix, g_ref[...], b_ref[...])
    x1_ref[...] = x1
    h2 = x1 * (1.0 + mod_ref[4:5, :]) + mod_ref[3:4, :]
    h2_ref[...] = h2
    logits = _dot3(h2, wrh_ref[...], wrl_ref[...]) + br_ref[...]
    lane = lax.broadcasted_iota(I32, logits.shape, 1).astype(F32)
    neg = -jnp.inf
    cur = jnp.where(lane < N_EXPERTS, logits, neg)
    vals, idxs = [], []
    for _ in range(TOP_K_EXPERTS):
        m = jnp.max(cur, axis=1, keepdims=True)
        idx = jnp.min(jnp.where(cur == m, lane, float(LANES)), axis=1, keepdims=True)
        vals.append(m)
        idxs.append(idx)
        cur = jnp.where(lane == idx, neg, cur)
    es = [jnp.exp(v - vals[0]) for v in vals]
    tot = es[0] + es[1] + es[2] + es[3]
    ti = jnp.zeros(logits.shape, F32)
    tg = jnp.zeros(logits.shape, F32)
    for k in range(TOP_K_EXPERTS):
        ti = jnp.where(lane == k, idxs[k], ti)
        tg = jnp.where(lane == k, es[k] / tot, tg)
    ti_ref[...] = ti.astype(I32)
    tg_ref[...] = tg


def _post_call(oa, ob, sg, x, mod, wsb, wdsa, wout, ln_g, ln_b, wr_hi, wr_lo, br, tm):
    s, t, d = x.shape
    row = lambda w: pl.BlockSpec((None, tm, w), lambda b, i: (b, i, 0))
    const = lambda a: pl.BlockSpec(a.shape, lambda b, i: (0,) * a.ndim)
    shp = lambda w, dt: jax.ShapeDtypeStruct((s, t, w), dt)
    return pl.pallas_call(
        _post_kernel,
        grid=(s, t // tm),
        in_specs=[row(512), row(512), row(2 * d), row(d),
                  pl.BlockSpec((None, 6, d), lambda b, i: (b, 0, 0)),
                  const(wsb), const(wdsa), const(wout), const(ln_g), const(ln_b),
                  const(wr_hi), const(wr_lo), const(br)],
        out_specs=[row(d), row(d), row(LANES), row(LANES)],
        out_shape=[shp(d, F32), shp(d, F32), shp(LANES, I32), shp(LANES, F32)],
        compiler_params=_cparams(("arbitrary", "arbitrary")),
        name="post_attention_router",
    )(oa, ob, sg, x, mod, wsb, wdsa, wout, ln_g, ln_b, wr_hi, wr_lo, br)


def _ffn_kernel(be_ref, tok0_ref, tokn_ref, h_hbm, wup_ref, bup_ref, wdn_ref, bdn_ref,
                o_ref, xbuf, sem, *, m):
    i = pl.program_id(0)
    nb = pl.num_programs(0)

    def row_copy(tok, slot, r):
        return pltpu.make_async_copy(h_hbm.at[pl.ds(tok, 1), :], xbuf.at[slot, pl.ds(r, 1), :], sem.at[slot])

    def wait_all(slot):
        def body(r, carry):
            row_copy(0, slot, r).wait()
            return carry
        lax.fori_loop(0, m, body, 0, unroll=8)

    slot = i % 2

    @pl.when(i == 0)
    def _():
        def body(r, carry):
            row_copy(tok0_ref[0, r], 0, r).start()
            return carry
        lax.fori_loop(0, m, body, 0, unroll=8)

    wait_all(slot)

    for r in range(m):
        row_copy(tokn_ref[0, r], 1 - slot, r).start()

    x = xbuf[slot].astype(BF16)
    u = _dot(x, wup_ref[...]) + bup_ref[...]
    acts = []
    for t in range(2 * D_FF // SWIGLU_TILE):
        a = t * SWIGLU_TILE
        glu = jnp.minimum(u[:, a:a + LANES], SWIGLU_LIMIT)
        lin = jnp.clip(u[:, a + LANES:a + SWIGLU_TILE], -SWIGLU_LIMIT, SWIGLU_LIMIT)
        acts.append((glu * jax.nn.sigmoid(SWIGLU_ALPHA * glu) * (lin + 1.0)).astype(BF16))
    o_ref[...] = _dot(jnp.concatenate(acts, axis=1), wdn_ref[...]) + bdn_ref[...]

    @pl.when(i == nb - 1)
    def _():
        wait_all(1 - slot)


def _deinterleave_kernel(w_ref, o_ref):
    j = lax.broadcasted_iota(I32, (SWIGLU_TILE, SWIGLU_TILE), 0)
    s = lax.broadcasted_iota(I32, (SWIGLU_TILE, SWIGLU_TILE), 1)
    src = jnp.where(s < LANES, 2 * s, 2 * (s - LANES) + 1)
    perm = jnp.where(j == src, 1.0, 0.0).astype(BF16)
    for t in range(w_ref.shape[1] // SWIGLU_TILE):
        a = t * SWIGLU_TILE
        o_ref[:, a:a + SWIGLU_TILE] = _dot(w_ref[:, a:a + SWIGLU_TILE].astype(BF16), perm).astype(BF16)


def _deinterleave_call(w_up):
    e, d, f = w_up.shape
    tr = 512
    return pl.pallas_call(
        _deinterleave_kernel,
        grid=(e, d // tr),
        in_specs=[pl.BlockSpec((None, tr, f), lambda a, b: (a, b, 0))],
        out_specs=pl.BlockSpec((None, tr, f), lambda a, b: (a, b, 0)),
        out_shape=jax.ShapeDtypeStruct((e, d, f), BF16),
        compiler_params=_cparams(("arbitrary", "arbitrary")),
        name="w_up_tiles",
    )(w_up)


def _ffn_call(block_expert, row_tok, h2, wup, bup, wdn, bdn, tile):
    n_rows = row_tok.shape[0]
    n_blocks = n_rows // tile
    d = h2.shape[1]
    tok3 = row_tok.reshape(n_blocks, 1, tile)
    grid_spec = pltpu.PrefetchScalarGridSpec(
        num_scalar_prefetch=1,
        grid=(n_blocks,),
        in_specs=[pl.BlockSpec((None, 1, tile), lambda i, be: (0, 0, 0), memory_space=pltpu.SMEM),
                  pl.BlockSpec((None, 1, tile), lambda i, be: (jnp.minimum(i + 1, n_blocks - 1), 0, 0),
                               memory_space=pltpu.SMEM),
                  pl.BlockSpec(memory_space=pl.ANY),
                  pl.BlockSpec((None, d, 2 * D_FF), lambda i, be: (be[i], 0, 0)),
                  pl.BlockSpec((None, 1, 2 * D_FF), lambda i, be: (be[i], 0, 0)),
                  pl.BlockSpec((None, D_FF, d), lambda i, be: (be[i], 0, 0)),
                  pl.BlockSpec((None, 1, d), lambda i, be: (be[i], 0, 0))],
        out_specs=pl.BlockSpec((tile, d), lambda i, be: (i, 0)),
        scratch_shapes=[pltpu.VMEM((2, tile, d), F32), pltpu.SemaphoreType.DMA((2,))],
    )
    return pl.pallas_call(
        functools.partial(_ffn_kernel, m=tile),
        grid_spec=grid_spec,
        out_shape=jax.ShapeDtypeStruct((n_rows, d), F32),
        compiler_params=_cparams(("arbitrary",)),
        name="expert_ffn",
    )(block_expert, tok3, tok3, h2, wup, bup, wdn, bdn)


def _combine_kernel(pos0_ref, posn_ref, rows_hbm, x1_ref, tg_ref, mod_ref, g_ref, b_ref, o_ref, buf, sem, *, tm):
    i = pl.program_id(0)
    nb = pl.num_programs(0)
    k = TOP_K_EXPERTS

    def row_copy(src, slot, r, j):
        return pltpu.make_async_copy(rows_hbm.at[pl.ds(src, 1), :], buf.at[slot, j, pl.ds(r, 1), :], sem.at[slot])

    def issue(pos_ref, slot):
        def body(r, carry):
            for j in range(k):
                row_copy(pos_ref[0, r * k + j], slot, r, j).start()
            return carry
        lax.fori_loop(0, tm, body, 0, unroll=4)

    slot = i % 2

    @pl.when(i == 0)
    def _():
        issue(pos0_ref, 0)

    @pl.when(i + 1 < nb)
    def _():
        issue(posn_ref, 1 - slot)

    def wait_body(r, carry):
        for j in range(k):
            row_copy(0, slot, r, j).wait()
        return carry
    lax.fori_loop(0, tm, wait_body, 0, unroll=4)

    gated = [buf[slot, j] * tg_ref[:, j:j + 1] for j in range(k)]
    y = (gated[0] + gated[1]) + (gated[2] + gated[3])
    o_ref[...] = _layer_norm(DEEPNORM_ALPHA * x1_ref[...] + mod_ref[5:6, :] * y, g_ref[...], b_ref[...])


def _combine_call(pos, rows, x1, tg, mod, seq_len, ln_g, ln_b, tm):
    n, d = x1.shape
    nb = n // tm
    per_seq = seq_len // tm
    pos3 = pos.reshape(nb, 1, tm * TOP_K_EXPERTS)
    return pl.pallas_call(
        functools.partial(_combine_kernel, tm=tm),
        grid=(nb,),
        in_specs=[pl.BlockSpec((None, 1, tm * TOP_K_EXPERTS), lambda i: (0, 0, 0), memory_space=pltpu.SMEM),
                  pl.BlockSpec((None, 1, tm * TOP_K_EXPERTS), lambda i: (jnp.minimum(i + 1, nb - 1), 0, 0),
                               memory_space=pltpu.SMEM),
                  pl.BlockSpec(memory_space=pl.ANY),
                  pl.BlockSpec((tm, d), lambda i: (i, 0)),
                  pl.BlockSpec((tm, LANES), lambda i: (i, 0)),
                  pl.BlockSpec((None, 6, d), lambda i: (i // per_seq, 0, 0)),
                  pl.BlockSpec((1, d), lambda i: (0, 0)),
                  pl.BlockSpec((1, d), lambda i: (0, 0))],
        out_specs=pl.BlockSpec((tm, d), lambda i: (i, 0)),
        out_shape=jax.ShapeDtypeStruct((n, d), F32),
        scratch_shapes=[pltpu.VMEM((2, TOP_K_EXPERTS, tm, d), F32), pltpu.SemaphoreType.DMA((2,))],
        compiler_params=_cparams(("arbitrary",)),
        name="moe_combine_ln2",
    )(pos3, pos3, rows, x1, tg, mod, ln_g, ln_b)


def _routing(top_idx, tile):
    n_tok = top_idx.shape[0]
    n_assign = n_tok * TOP_K_EXPERTS
    e_flat = top_idx.reshape(-1)
    onehot = (e_flat[:, None] == jnp.arange(N_EXPERTS, dtype=I32)[None, :]).astype(I32)
    csum = jnp.cumsum(onehot, axis=0)
    rank = jnp.sum(onehot * csum, axis=1) - 1
    counts = csum[-1]
    padded = (counts + tile - 1) // tile * tile
    pend = jnp.cumsum(padded)
    pstart = pend - padded
    dest = (pstart[e_flat] + rank).astype(I32)
    n_rows = (n_assign + N_EXPERTS * (tile - 1) + tile - 1) // tile * tile
    n_blocks = n_rows // tile
    tok_flat = jnp.repeat(jnp.arange(n_tok, dtype=I32), TOP_K_EXPERTS)
    row_tok = jnp.zeros((n_rows,), I32).at[dest].set(tok_flat)
    block_expert = jnp.minimum(
        jnp.searchsorted(pend, jnp.arange(n_blocks, dtype=I32) * tile, side="right"), N_EXPERTS - 1).astype(I32)
    return dest, row_tok, block_expert


def _stream(x, mod, kv_cache, weights, btiles, tm, tq):
    (w_packed, b_packed, wsb, wdsa, wout, ln1_g, ln1_b, wr_hi, wr_lo, br,
     wup, bup, wdn, bdn, ln2_g, ln2_b) = weights
    s, t, d = x.shape
    (qa, ka32, va32, ka16, va16, qb, kb32, vb32, kbd, vx, qi, ki32, kid, wi, sg) = _proj_call(x, mod, w_packed, b_packed, tm)

    if kv_cache is None:
        k_sb, v_sb, k_id, k_bd, v_x = ka16, va16, kid, kbd, vx
        total = t
        last_fn = lambda i: i
        adm_fn = lambda i, r, key_pos: key_pos < i * tq + (r // CHUNK + 1) * CHUNK
    else:
        past = kv_cache[0].shape[1]
        total = past + t
        pad = (-total) % (2 * ATT_BLOCK)
        cat = lambda cache, new: jnp.concatenate(
            [cache, new, jnp.zeros((s, pad, new.shape[2]), new.dtype)], axis=1)
        k_sb, v_sb, k_id, k_bd, v_x = [cat(c_, n_) for c_, n_ in zip(kv_cache, (ka16, va16, kid, kbd, vx))]
        last_blk = past // ATT_BLOCK
        last_fn = lambda i: i * 0 + last_blk
        adm_fn = lambda i, r, key_pos: key_pos < total
    topk = max(1, min(TOPK_MAX, total // 4))

    if kv_cache is None:
        oa = _sb_call(qa, k_sb, v_sb, min(SB_BLOCK, t), lambda i: i)
    else:
        assert kv_cache[0].shape[1] % SB_BLOCK == 0 and t <= SB_BLOCK
        oa = _sb_call(qa, k_sb, v_sb, t, lambda i: i * 0 + kv_cache[0].shape[1] // SB_BLOCK)
    ob = _dsa_call(qi, wi, qb, k_id, k_bd, v_x, btiles, tq, topk, last_fn, adm_fn)
    x1, h2, ti, tg = _post_call(oa, ob, sg, x, mod, wsb, wdsa, wout, ln1_g, ln1_b, wr_hi, wr_lo, br, tm)

    n = s * t
    tile = 4 * MOE_BLOCK if n * TOP_K_EXPERTS >= N_EXPERTS * 8 * MOE_BLOCK else MOE_BLOCK
    dest, row_tok, block_expert = _routing(ti.reshape(n, LANES)[:, :TOP_K_EXPERTS], tile)
    rows = _ffn_call(block_expert, row_tok, h2.reshape(n, d), wup, bup, wdn, bdn, tile)
    y = _combine_call(dest, rows, x1.reshape(n, d), tg.reshape(n, LANES), mod, t, ln2_g, ln2_b, min(128, t))
    new_rows = (ka32.reshape(1, s, t, SB_HEADS, HEAD_DIM), va32.reshape(1, s, t, SB_HEADS, HEAD_DIM),
                kb32.reshape(1, s, t, DSA_KV_HEADS, HEAD_DIM), vb32.reshape(1, s, t, DSA_KV_HEADS, HEAD_DIM),
                ki32.reshape(1, s, t, IDX_DIM))
    return y.reshape(s, t, d), new_rows


def kernel(x_prompt, x_sample, cache_sb_k, cache_sb_v, cache_dsa_k, cache_dsa_v, cache_idx_k, c_prompt, c_sample, rel_bias, w_ada, b_ada, w_in, b_in, w_o_sb, w_o_dsa, w_out, ln1_g, ln1_b, w_router, b_router, w_up, b_up, w_down, b_down, ln2_g, ln2_b):
    d = D_MODEL
    nb, ns = x_prompt.shape[0], x_sample.shape[0]
    past = cache_sb_k.shape[2]

    mod = _mod_call(jnp.concatenate([c_prompt, c_sample], axis=0), w_ada[0], b_ada[0]).reshape(nb + ns, 6, d)
    btiles = _bias_call(rel_bias)

    w_packed, b_packed = _pack_w_in(w_in[0], b_in[0])
    wr = jnp.concatenate([w_router[0], jnp.zeros((d, LANES - N_EXPERTS), F32)], axis=1)
    wr_hi = wr.astype(BF16)
    wr_lo = (wr - wr_hi.astype(F32)).astype(BF16)
    br = jnp.concatenate([b_router[0], jnp.zeros((LANES - N_EXPERTS,), F32)]).reshape(1, LANES)
    wup = _deinterleave_call(w_up[0])
    bup = b_up[0].reshape(N_EXPERTS, 2 * D_FF // SWIGLU_TILE, LANES, 2).swapaxes(2, 3).reshape(N_EXPERTS, 1, 2 * D_FF)
    weights = (w_packed, b_packed, w_o_sb[0].astype(BF16), w_o_dsa[0].astype(BF16), w_out[0].astype(BF16),
               ln1_g[0].reshape(1, d), ln1_b[0].reshape(1, d), wr_hi, wr_lo, br,
               wup, bup, w_down[0].astype(BF16), b_down[0].reshape(N_EXPERTS, 1, d),
               ln2_g[0].reshape(1, d), ln2_b[0].reshape(1, d))

    dup = lambda a: jnp.concatenate([a[..., :64], a[..., :64], a[..., 64:], a[..., 64:]], axis=-1)
    idx_c = cache_idx_k[0]
    caches = (cache_sb_k[0].reshape(ns, past, SB_HEADS * HEAD_DIM).astype(BF16),
              cache_sb_v[0].reshape(ns, past, SB_HEADS * HEAD_DIM).astype(BF16),
              jnp.concatenate([idx_c, idx_c], axis=-1).astype(BF16),
              dup(cache_dsa_k[0].reshape(ns, past, DSA_KV_HEADS * HEAD_DIM)).astype(BF16),
              _with_ones(cache_dsa_v[0].reshape(ns, past, DSA_KV_HEADS * HEAD_DIM)).astype(BF16))

    t_p, t_s = x_prompt.shape[1], x_sample.shape[1]
    y_p, new_p = _stream(x_prompt, mod[:nb], None, weights, btiles, min(256, t_p), ATT_BLOCK)
    y_s, new_s = _stream(x_sample, mod[nb:], caches, weights, btiles, t_s, t_s)
    return (y_p, y_s) + new_p + new_s
```

```python
import functools

import jax
import jax.numpy as jnp
import numpy as np
from jax import lax
from jax.experimental import pallas as pl
from jax.experimental.pallas import tpu as pltpu

F32 = jnp.float32
BF16 = jnp.bfloat16
I32 = jnp.int32

D_MODEL = 1024
CHUNK = 64
SB_HEADS = 8
HEAD_DIM = 64
DSA_HEADS = 8
DSA_KV_HEADS = 2
IDX_HEADS = 8
IDX_DIM = 64
TOPK_MAX = 256
N_BUCKETS = 32
N_EXPERTS = 32
TOP_K_EXPERTS = 4
D_FF = 1024
SWIGLU_LIMIT = 7.0
SWIGLU_ALPHA = 1.702
MOE_BLOCK = 128
LN_EPS = 1e-5
DEPTH = 1
DEEPNORM_ALPHA = (2.0 * DEPTH) ** 0.25

LANES = 128
ATT_BLOCK = 128
DSA_Q_BLOCK = 256
SB_BLOCK = 256
VMEM_LIMIT = 56 * 1024 * 1024

EXP_ZERO_BELOW = -104.0
NEG_INF_KEY = -2139095041
INT32_MAX = 2147483647
MASKED_LOGIT = -1e30
BISECT_STEPS_PER_CHECK = 2
SWIGLU_TILE = 2 * LANES

_SEG = {}
_off = 0
for _name, _w in (("qa", 512), ("ka", 512), ("va", 512), ("qb", 512), ("kb", 128), ("vb", 128),
                  ("kbd", 256), ("vx", 256), ("qi", 512), ("kid", 128), ("wi", 128),
                  ("ga", 1024), ("gb", 1024)):
    _SEG[_name] = (_off, _off + _w)
    _off += _w
PACKED_COLS = _off


def _cparams(sem):
    return pltpu.CompilerParams(dimension_semantics=sem, vmem_limit_bytes=VMEM_LIMIT)


def _dot(a, b):
    return jnp.dot(a, b, preferred_element_type=F32)


def _dot_nt(a, b):
    return lax.dot_general(a, b, (((1,), (1,)), ((), ())), preferred_element_type=F32)


def _split_bf16(x):
    hi = x.astype(BF16)
    lo = (x - hi.astype(F32)).astype(BF16)
    return hi, lo


def _dot3(a, b_hi, b_lo):
    a_hi, a_lo = _split_bf16(a)
    return _dot(a_hi, b_hi) + (_dot(a_hi, b_lo) + _dot(a_lo, b_hi))


def _mod_kernel(c_ref, w_ref, b_ref, o_ref):
    c = c_ref[...]
    s = c * jax.nn.sigmoid(c)
    w_hi, w_lo = _split_bf16(w_ref[...])
    o_ref[...] = _dot3(s, w_hi, w_lo) + b_ref[...]


def _mod_call(c_all, w_ada, b_ada):
    n, d = c_all.shape
    cols = w_ada.shape[1]
    tn = 1024
    return pl.pallas_call(
        _mod_kernel,
        grid=(cols // tn,),
        in_specs=[pl.BlockSpec((n, d), lambda j: (0, 0)),
                  pl.BlockSpec((d, tn), lambda j: (0, j)),
                  pl.BlockSpec((1, tn), lambda j: (0, j))],
        out_specs=pl.BlockSpec((n, tn), lambda j: (0, j)),
        out_shape=jax.ShapeDtypeStruct((n, cols), F32),
        compiler_params=_cparams(("arbitrary",)),
        name="adaln_mod",
    )(c_all, w_ada, b_ada.reshape(1, cols))


def _proj_kernel(x_ref, mod_ref, w_ref, b_ref,
                 qa_ref, ka32_ref, va32_ref, ka16_ref, va16_ref,
                 qb_ref, kb32_ref, vb32_ref, kbd_ref, vx_ref,
                 qi_ref, ki32_ref, kid_ref, wi_ref, sg_ref):
    sh1 = mod_ref[0:1, :]
    sc1 = mod_ref[1:2, :]
    h = (x_ref[...] * (1.0 + sc1) + sh1).astype(BF16)

    def seg(name):
        a, b = _SEG[name]
        return _dot(h, w_ref[:, a:b]) + b_ref[:, a:b]

    qa_ref[...] = (seg("qa") * HEAD_DIM ** -0.5).astype(BF16)
    ka = seg("ka")
    ka32_ref[...] = ka
    ka16_ref[...] = ka.astype(BF16)
    va = seg("va")
    va32_ref[...] = va
    va16_ref[...] = va.astype(BF16)
    qb_ref[...] = (seg("qb") * HEAD_DIM ** -0.5).astype(BF16)
    kb32_ref[...] = seg("kb")
    vb32_ref[...] = seg("vb")
    kbd_ref[...] = seg("kbd").astype(BF16)
    vx_ref[...] = seg("vx").astype(BF16)
    qi_ref[...] = (seg("qi") * IDX_DIM ** -0.5).astype(BF16)
    kid = seg("kid")
    ki32_ref[...] = kid[:, :IDX_DIM]
    kid_ref[...] = kid.astype(BF16)
    wi_ref[...] = seg("wi") * IDX_HEADS ** -0.5
    a, _ = _SEG["ga"]
    _, b = _SEG["gb"]
    sg_ref[...] = jax.nn.sigmoid(_dot(h, w_ref[:, a:b]) + b_ref[:, a:b])


def _pack_w_in(w_in, b_in):
    offs = np.cumsum((0, 512, 512, 512, 512, 128, 128, 512, 8, 64, 1024, 1024))
    qa, ka, va, qb, kb, vb, qi, wi, ki, ga, gb = [slice(int(offs[i]), int(offs[i + 1])) for i in range(11)]

    def pack(m, fill):
        kb_m, vb_m = m[..., kb], m[..., vb]
        dup = lambda t: jnp.concatenate([t[..., :64], t[..., :64], t[..., 64:], t[..., 64:]], axis=-1)
        wi_m = jnp.concatenate([m[..., wi], jnp.zeros(m.shape[:-1] + (LANES - IDX_HEADS,), m.dtype)], axis=-1)
        return jnp.concatenate([m[..., qa], m[..., ka], m[..., va], m[..., qb], kb_m, vb_m, dup(kb_m),
                                _with_ones(vb_m, fill),
                                m[..., qi], m[..., ki], m[..., ki], wi_m, m[..., ga], m[..., gb]], axis=-1)

    return pack(w_in, 0.0).astype(BF16), pack(b_in.reshape(1, -1), 1.0)


def _with_ones(v, fill=1.0):
    f = jnp.full(v.shape[:-1] + (HEAD_DIM,), fill, v.dtype)
    return jnp.concatenate([v[..., :HEAD_DIM], f, v[..., HEAD_DIM:], f], axis=-1)


def _proj_call(x, mod, w_packed, b_packed, tm):
    s, t, d = x.shape
    nt = t // tm
    row = lambda w: pl.BlockSpec((None, tm, w), lambda b, i: (b, i, 0))
    shp = lambda w, dt: jax.ShapeDtypeStruct((s, t, w), dt)
    outs = [(512, BF16), (512, F32), (512, F32), (512, BF16), (512, BF16),
            (512, BF16), (128, F32), (128, F32), (256, BF16), (256, BF16),
            (512, BF16), (IDX_DIM, F32), (128, BF16), (128, F32), (2048, F32)]
    return pl.pallas_call(
        _proj_kernel,
        grid=(s, nt),
        in_specs=[row(d),
                  pl.BlockSpec((None, 6, d), lambda b, i: (b, 0, 0)),
                  pl.BlockSpec((d, PACKED_COLS), lambda b, i: (0, 0)),
                  pl.BlockSpec((1, PACKED_COLS), lambda b, i: (0, 0))],
        out_specs=[row(w) for w, _ in outs],
        out_shape=[shp(w, dt) for w, dt in outs],
        compiler_params=_cparams(("arbitrary", "arbitrary")),
        name="in_proj",
    )(x, mod, w_packed, b_packed)


def _bias_kernel(tab_ref, o_ref):
    r = lax.broadcasted_iota(I32, (DSA_Q_BLOCK, ATT_BLOCK), 0)
    c = lax.broadcasted_iota(I32, (DSA_Q_BLOCK, ATT_BLOCK), 1)
    half = N_BUCKETS // 2
    max_exact = half // 2
    for j, off in enumerate((ATT_BLOCK, 0, -ATT_BLOCK, -(1 << 20))):
        rel = c - r + off
        n = jnp.abs(rel)
        large = jnp.full_like(n, max_exact)
        for thr in (12, 16, 23, 32, 46, 64, 91):
            large = large + (n >= thr).astype(I32)
        bucket = jnp.where(rel > 0, half, 0) + jnp.where(n < max_exact, n, large)
        for h in range(DSA_HEADS):
            acc = jnp.zeros((DSA_Q_BLOCK, ATT_BLOCK), F32)
            for b in range(N_BUCKETS):
                acc = jnp.where(bucket == b, tab_ref[b, h], acc)
            o_ref[h, j] = acc


def _bias_call(rel_bias):
    return pl.pallas_call(
        _bias_kernel,
        in_specs=[pl.BlockSpec(memory_space=pltpu.SMEM)],
        out_specs=pl.BlockSpec(memory_space=pltpu.VMEM),
        out_shape=jax.ShapeDtypeStruct((DSA_HEADS, 4, DSA_Q_BLOCK, ATT_BLOCK), F32),
        name="t5_bias_tiles",
    )(rel_bias)


def _softplus(z):
    return jnp.maximum(z, 0.0) + jnp.log1p(jnp.exp(-jnp.abs(z)))


def _sb_kernel(q_ref, k_ref, v_ref, o_ref, *, tq, last_fn):
    tk = SB_BLOCK
    last = last_fn(pl.program_id(2))
    lane = lax.broadcasted_iota(I32, (1, LANES), 1)
    low = lane < HEAD_DIM
    q = q_ref[...]
    zero = jnp.zeros_like(q)
    qh = (jnp.where(low, q, zero), jnp.where(low, zero, q))
    uj = lax.broadcasted_iota(I32, (tk, tk), 0)
    us = lax.broadcasted_iota(I32, (tk, tk), 1)
    u_mat = jnp.where(uj > us, 1.0, 0.0).astype(BF16)

    def tile(kb, vis, carries):
        start = pl.multiple_of(kb * tk, tk)
        kblk = k_ref[pl.ds(start, tk), :]
        vblk = v_ref[pl.ds(start, tk), :]
        pv = []
        new_carries = []
        for h in range(2):
            z = _dot_nt(qh[h], kblk)
            sp = _softplus(z)
            lk = -sp if vis is None else jnp.where(vis, -sp, 0.0)
            hi, lo = _split_bf16(lk)
            after = _dot(hi, u_mat) + _dot(lo, u_mat)
            w = jnp.exp((z - sp) + (after + carries[h]))
            if vis is not None:
                w = jnp.where(vis, w, 0.0)
            pv.append(_dot(w.astype(BF16), vblk))
            new_carries.append(carries[h] + jnp.sum(lk, axis=1, keepdims=True))
        return jnp.where(low, pv[0], pv[1]), new_carries

    r = lax.broadcasted_iota(I32, (tq, tk), 0)
    c = lax.broadcasted_iota(I32, (tq, tk), 1)
    zc = jnp.zeros((tq, 1), F32)
    acc, carries = tile(last, c < r, [zc, zc])

    def live(carries):
        return jnp.max(jnp.maximum(carries[0], carries[1])) > EXP_ZERO_BELOW

    def cond(st):
        kb, go, _, _, _ = st
        return jnp.logical_and(kb >= 0, go)

    def body(st):
        kb, _, acc, c0, c1 = st
        pv, nc = tile(kb, None, [c0, c1])
        return kb - 1, live(nc), acc + pv, nc[0], nc[1]

    st = lax.while_loop(cond, body, (last - 1, live(carries), acc, carries[0], carries[1]))
    o_ref[...] = st[2].astype(o_ref.dtype)


def _sb_call(q, k, v, tq, last_fn):
    s, t, _ = q.shape
    tkk = k.shape[1]
    return pl.pallas_call(
        functools.partial(_sb_kernel, tq=tq, last_fn=last_fn),
        grid=(s, SB_HEADS // 2, t // tq),
        in_specs=[pl.BlockSpec((None, tq, LANES), lambda b, p, i: (b, i, p)),
                  pl.BlockSpec((None, tkk, LANES), lambda b, p, i: (b, 0, p)),
                  pl.BlockSpec((None, tkk, LANES), lambda b, p, i: (b, 0, p))],
        out_specs=pl.BlockSpec((None, tq, LANES), lambda b, p, i: (b, i, p)),
        out_shape=jax.ShapeDtypeStruct(q.shape, BF16),
        compiler_params=_cparams(("arbitrary", "arbitrary", "arbitrary")),
        name="stick_breaking",
    )(q, k, v)


def _dsa_kernel(qi_ref, wi_ref, qb_ref, ki_ref, kb_ref, vx_ref, bt_ref, o_ref,
                qis_sc, wb_sc, qbs_sc, key_sc, mx_sc, acc_sc, *, tq, topk, diag_fn, adm_fn):
    tk = ATT_BLOCK
    group = DSA_HEADS // DSA_KV_HEADS
    diag = diag_fn(pl.program_id(1))
    last = diag + (tq + tk - 1) // tk - 1
    lane = lax.broadcasted_iota(I32, (1, LANES), 1)
    low = lane < HEAD_DIM

    for h in range(IDX_HEADS):
        p = h // 2
        t = qi_ref[:, p * LANES:(p + 1) * LANES]
        z = jnp.zeros_like(t)
        qis_sc[h * tq:(h + 1) * tq, :] = jnp.where(low, t, z) if h % 2 == 0 else jnp.where(low, z, t)
        t = qb_ref[:, p * LANES:(p + 1) * LANES]
        g = h % group
        qbs_sc[h // group, g * tq:(g + 1) * tq, :] = jnp.where(low, t, z) if h % 2 == 0 else jnp.where(low, z, t)
        wb_sc[h] = jnp.broadcast_to(wi_ref[:, h:h + 1], (tq, LANES))

    r = lax.broadcasted_iota(I32, (tq, tk), 0)
    c = lax.broadcasted_iota(I32, (tq, tk), 1)
    last_pair = last // 2

    def score_pair(pi, masked):
        start = pl.multiple_of(pi * (2 * tk), 2 * tk)
        d = _dot_nt(qis_sc[...], ki_ref[pl.ds(start, 2 * tk), :])
        for half in range(2):
            s = jnp.zeros((tq, tk), F32)
            for h in range(IDX_HEADS):
                s = s + wb_sc[h] * jnp.maximum(d[h * tq:(h + 1) * tq, half * tk:(half + 1) * tk], 0.0)
            bits = pltpu.bitcast(s, I32)
            key = jnp.where(bits < 0, bits ^ INT32_MAX, bits)
            if masked:
                key = jnp.where(adm_fn(pl.program_id(1), r, c + (start + half * tk)), key, NEG_INF_KEY)
            key_sc[2 * pi + half] = key

    def score_body(pi, carry):
        score_pair(pi, False)
        return carry

    lax.fori_loop(0, last_pair, score_body, 0)
    score_pair(last_pair, True)

    def count_ge(thr):
        def body(pi, acc):
            k0 = key_sc[2 * pi]
            k1 = key_sc[2 * pi + 1]
            return acc + (jnp.where(k0 >= thr, 1.0, 0.0) + jnp.where(k1 >= thr, 1.0, 0.0))
        acc = lax.fori_loop(0, last_pair + 1, body, jnp.zeros((tq, tk), F32))
        return jnp.sum(acc, axis=1, keepdims=True)

    def midpoint(lo, hi):
        return (lo >> 1) + (hi >> 1) + (lo & hi & 1)

    kf = float(topk)

    def bis_cond(st):
        return st[0]

    def is_open(lo, hi, cnt_lo):
        return jnp.logical_and(midpoint(lo, hi) != lo, cnt_lo != kf)

    def bis_body(st):
        _, lo, hi, cnt_lo, cnt_hi = st
        for _ in range(BISECT_STEPS_PER_CHECK):
            mid = midpoint(lo, hi)
            open_ = is_open(lo, hi, cnt_lo)
            cnt = count_ge(mid)
            ge = cnt >= kf
            up = jnp.logical_and(open_, ge)
            dn = jnp.logical_and(open_, jnp.logical_not(ge))
            lo = jnp.where(up, mid, lo)
            cnt_lo = jnp.where(up, cnt, cnt_lo)
            hi = jnp.where(dn, mid, hi)
            cnt_hi = jnp.where(dn, cnt, cnt_hi)
        go = jnp.max(jnp.where(is_open(lo, hi, cnt_lo), 1.0, 0.0)) > 0.0
        return go, lo, hi, cnt_lo, cnt_hi

    col1 = lambda v, dt: jnp.full((tq, 1), v, dt)
    _, thr, _, cnt_lo, cnt_hi = lax.while_loop(
        bis_cond, bis_body,
        (jnp.bool_(True), col1(NEG_INF_KEY, I32), col1(INT32_MAX, I32), col1(-1.0, F32), col1(0.0, F32)))

    tied = cnt_lo > kf

    @pl.when(jnp.max(jnp.where(tied, 1.0, 0.0)) > 0.0)
    def _():
        need = kf - cnt_hi

        def count_tied_before(m):
            def body(kb, acc):
                col = c + kb * tk
                hit = jnp.logical_and(key_sc[kb] == thr, col < m)
                return acc + jnp.where(hit, 1.0, 0.0)
            acc = lax.fori_loop(0, last + 1, body, jnp.zeros((tq, tk), F32))
            return jnp.sum(acc, axis=1, keepdims=True)

        def idx_body(_, st):
            lo2, hi2 = st
            mid2 = (lo2 + hi2) >> 1
            ok = count_tied_before(mid2) >= need
            return jnp.where(ok, lo2, mid2), jnp.where(ok, mid2, hi2)

        n_keys = (last + 1) * tk
        _, mstar = lax.fori_loop(0, 15, idx_body, (col1(0, I32), jnp.full((tq, 1), 0, I32) + n_keys))

        def demote(kb, carry):
            col = c + kb * tk
            key = key_sc[kb]
            drop = jnp.logical_and(tied, jnp.logical_and(key == thr, col >= mstar))
            key_sc[kb] = jnp.where(drop, NEG_INF_KEY, key)
            return carry

        lax.fori_loop(0, last + 1, demote, 0)

    thr_eff = jnp.maximum(thr, NEG_INF_KEY + 1)

    for h in range(DSA_HEADS):
        mx_sc[h] = jnp.full((tq, LANES), MASKED_LOGIT, F32)
        acc_sc[h] = jnp.zeros((tq, LANES), F32)

    def masked_logits(pi, n):
        start = pl.multiple_of(pi * (2 * tk), 2 * tk)
        lg = _dot_nt(qbs_sc[n], kb_ref[pl.ds(start, 2 * tk), n * LANES:(n + 1) * LANES])
        sel = [key_sc[2 * pi + half] >= thr_eff for half in range(2)]
        j = [jnp.clip(diag - (2 * pi + half) + 1, 0, 3) for half in range(2)]
        out = []
        for g in range(group):
            h = n * group + g
            out.append([jnp.where(sel[half],
                                  lg[g * tq:(g + 1) * tq, half * tk:(half + 1) * tk] + bt_ref[h, j[half], 0:tq, :],
                                  MASKED_LOGIT) for half in range(2)])
        return out

    def max_body(pi, carry):
        for n in range(DSA_KV_HEADS):
            for g, (la, lb) in enumerate(masked_logits(pi, n)):
                h = n * group + g
                mx_sc[h] = jnp.maximum(mx_sc[h], jnp.maximum(la, lb))
        return carry

    lax.fori_loop(0, last_pair + 1, max_body, 0)
    for h in range(DSA_HEADS):
        mx_sc[h] = jnp.broadcast_to(jnp.max(mx_sc[h], axis=1, keepdims=True), (tq, LANES))

    def pv_body(pi, carry):
        start = pl.multiple_of(pi * (2 * tk), 2 * tk)
        for n in range(DSA_KV_HEADS):
            ps = []
            for g, (la, lb) in enumerate(masked_logits(pi, n)):
                m = mx_sc[n * group + g]
                ps.append(jnp.concatenate([jnp.exp(la - m).astype(BF16), jnp.exp(lb - m).astype(BF16)], axis=1))
            pv = _dot(jnp.concatenate(ps, axis=0), vx_ref[pl.ds(start, 2 * tk), n * LANES:(n + 1) * LANES])
            for g in range(group):
                h = n * group + g
                acc_sc[h] = acc_sc[h] + pv[g * tq:(g + 1) * tq]
        return carry

    lax.fori_loop(0, last_pair + 1, pv_body, 0)

    def normalised(h):
        a = acc_sc[h]
        return a / pltpu.roll(a, HEAD_DIM, axis=1)

    for p in range(DSA_HEADS // 2):
        o1 = pltpu.roll(normalised(2 * p + 1), HEAD_DIM, axis=1)
        o_ref[:, p * LANES:(p + 1) * LANES] = jnp.where(low, normalised(2 * p), o1).astype(o_ref.dtype)


def _dsa_call(qi, wi, qb, kid, kbd, vx, btiles, tq, topk, diag_fn, adm_fn):
    s, t, _ = qi.shape
    tkk = kid.shape[1]
    assert tkk % (2 * ATT_BLOCK) == 0, "keys must come in whole pairs of blocks"
    nkb = tkk // ATT_BLOCK
    rowq = lambda w: pl.BlockSpec((None, tq, w), lambda b, i: (b, i, 0))
    full = lambda w: pl.BlockSpec((None, tkk, w), lambda b, i: (b, 0, 0))
    return pl.pallas_call(
        functools.partial(_dsa_kernel, tq=tq, topk=topk, diag_fn=diag_fn, adm_fn=adm_fn),
        grid=(s, t // tq),
        in_specs=[rowq(512), rowq(LANES), rowq(512), full(LANES), full(2 * LANES), full(2 * LANES),
                  pl.BlockSpec(btiles.shape, lambda b, i: (0, 0, 0, 0))],
        out_specs=rowq(512),
        out_shape=jax.ShapeDtypeStruct((s, t, 512), BF16),
        scratch_shapes=[pltpu.VMEM((IDX_HEADS * tq, LANES), BF16),
                        pltpu.VMEM((IDX_HEADS, tq, LANES), F32),
                        pltpu.VMEM((DSA_KV_HEADS, DSA_HEADS // DSA_KV_HEADS * tq, LANES), BF16),
                        pltpu.VMEM((nkb, tq, ATT_BLOCK), I32),
                        pltpu.VMEM((DSA_HEADS, tq, LANES), F32),
                        pltpu.VMEM((DSA_HEADS, tq, LANES), F32)],
        compiler_params=_cparams(("arbitrary", "arbitrary")),
        name="dsa_attention",
    )(qi, wi, qb, kid, kbd, vx, btiles)


def _layer_norm(x, g, b):
    mu = jnp.mean(x, axis=-1, keepdims=True)
    xc = x - mu
    var = jnp.mean(xc * xc, axis=-1, keepdims=True)
    return xc * lax.rsqrt(var + LN_EPS) * g + b


def _post_kernel(oa_ref, ob_ref, sg_ref, x_ref, mod_ref, wsb_ref, wdsa_ref, wout_ref, g_ref, b_ref,
                 wrh_ref, wrl_ref, br_ref, x1_ref, h2_ref, ti_ref, tg_ref):
    d = D_MODEL
    ya = _dot(oa_ref[...], wsb_ref[...])
    yb = _dot(ob_ref[...], wdsa_ref[...])
    merged = sg_ref[:, :d] * ya + sg_ref[:, d:] * yb
    mix = _dot(merged.astype(BF16), wout_ref[...])
    g1 = mod_ref[2:3, :]
    x1 = _layer_norm(DEEPNORM_ALPHA * x_ref[...] + g1 * mix, g_ref[...], b_ref[...])
    x1_ref[...] = x1
    h2 = x1 * (1.0 + mod_ref[4:5, :]) + mod_ref[3:4, :]
    h2_ref[...] = h2
    logits = _dot3(h2, wrh_ref[...], wrl_ref[...]) + br_ref[...]
    lane = lax.broadcasted_iota(I32, logits.shape, 1).astype(F32)
    neg = -jnp.inf
    cur = jnp.where(lane < N_EXPERTS, logits, neg)
    vals, idxs = [], []
    for _ in range(TOP_K_EXPERTS):
        m = jnp.max(cur, axis=1, keepdims=True)
        idx = jnp.min(jnp.where(cur == m, lane, float(LANES)), axis=1, keepdims=True)
        vals.append(m)
        idxs.append(idx)
        cur = jnp.where(lane == idx, neg, cur)
    es = [jnp.exp(v - vals[0]) for v in vals]
    tot = es[0] + es[1] + es[2] + es[3]
    ti = jnp.zeros(logits.shape, F32)
    tg = jnp.zeros(logits.shape, F32)
    for k in range(TOP_K_EXPERTS):
        ti = jnp.where(lane == k, idxs[k], ti)
        tg = jnp.where(lane == k, es[k] / tot, tg)
    ti_ref[...] = ti.astype(I32)
    tg_ref[...] = tg


def _post_call(oa, ob, sg, x, mod, wsb, wdsa, wout, ln_g, ln_b, wr_hi, wr_lo, br, tm):
    s, t, d = x.shape
    row = lambda w: pl.BlockSpec((None, tm, w), lambda b, i: (b, i, 0))
    const = lambda a: pl.BlockSpec(a.shape, lambda b, i: (0,) * a.ndim)
    shp = lambda w, dt: jax.ShapeDtypeStruct((s, t, w), dt)
    return pl.pallas_call(
        _post_kernel,
        grid=(s, t // tm),
        in_specs=[row(512), row(512), row(2 * d), row(d),
                  pl.BlockSpec((None, 6, d), lambda b, i: (b, 0, 0)),
                  const(wsb), const(wdsa), const(wout), const(ln_g), const(ln_b),
                  const(wr_hi), const(wr_lo), const(br)],
        out_specs=[row(d), row(d), row(LANES), row(LANES)],
        out_shape=[shp(d, F32), shp(d, F32), shp(LANES, I32), shp(LANES, F32)],
        compiler_params=_cparams(("arbitrary", "arbitrary")),
        name="post_attention_router",
    )(oa, ob, sg, x, mod, wsb, wdsa, wout, ln_g, ln_b, wr_hi, wr_lo, br)


def _dispatch_kernel(zoff_ref, nused_ref, dest_ref, h_ref, xs_hbm, buf, zbuf, sem, zsem, *, tm, tile, n_tiles):
    i = pl.program_id(0)
    nb = pl.num_programs(0)
    k = TOP_K_EXPERTS
    slot = i % 2

    def row_copy(dst, slot, r):
        return pltpu.make_async_copy(buf.at[slot, pl.ds(r, 1), :], xs_hbm.at[pl.ds(dst, 1), :], sem.at[slot])

    def wait_all(slot):
        def body(r, carry):
            for _ in range(k):
                row_copy(0, slot, r).wait()
            return carry
        lax.fori_loop(0, tm, body, 0, unroll=4)

    def zero_fill(start):
        return pltpu.make_async_copy(zbuf, xs_hbm.at[pl.ds(start, tile), :], zsem)

    @pl.when(i == 0)
    def _():
        zbuf[...] = jnp.zeros_like(zbuf)
        fills = [(zoff_ref[e] >= 0, pl.multiple_of(jnp.maximum(zoff_ref[e], 0), MOE_BLOCK)) for e in range(N_EXPERTS)]
        fills += [(t >= nused_ref[0], t * tile) for t in range(max(n_tiles - N_EXPERTS - 1, 0), n_tiles)]
        for on, start in fills:
            @pl.when(on)
            def _():
                zero_fill(start).start()
        for on, start in fills:
            @pl.when(on)
            def _():
                zero_fill(start).wait()

    @pl.when(i >= 2)
    def _():
        wait_all(slot)

    buf[slot] = h_ref[...]
    for r in range(tm):
        for j in range(k):
            row_copy(dest_ref[0, r * k + j], slot, r).start()

    @pl.when(i == nb - 1)
    def _():
        wait_all(slot)

        @pl.when(nb >= 2)
        def _():
            wait_all(1 - slot)


def _dispatch_call(zoff, nused, dest, h2, n_rows, tile):
    n, d = h2.shape
    tm = min(MOE_BLOCK, n)
    nb = n // tm
    dest3 = dest.reshape(nb, 1, tm * TOP_K_EXPERTS)
    grid_spec = pltpu.PrefetchScalarGridSpec(
        num_scalar_prefetch=2,
        grid=(nb,),
        in_specs=[pl.BlockSpec((None, 1, tm * TOP_K_EXPERTS), lambda i, zo, nu: (i, 0, 0), memory_space=pltpu.SMEM),
                  pl.BlockSpec((tm, d), lambda i, zo, nu: (i, 0))],
        out_specs=pl.BlockSpec(memory_space=pl.ANY),
        scratch_shapes=[pltpu.VMEM((2, tm, d), F32), pltpu.VMEM((tile, d), F32),
                        pltpu.SemaphoreType.DMA((2,)), pltpu.SemaphoreType.DMA(())],
    )
    return pl.pallas_call(
        functools.partial(_dispatch_kernel, tm=tm, tile=tile, n_tiles=n_rows // tile),
        grid_spec=grid_spec,
        out_shape=jax.ShapeDtypeStruct((n_rows, d), F32),
        compiler_params=_cparams(("arbitrary",)),
        name="moe_dispatch",
    )(zoff, nused, dest3, h2)


def _ffn_kernel(be_ref, x_ref, wup_ref, bup_ref, wdn_ref, bdn_ref, o_ref):
    u = _dot(x_ref[...].astype(BF16), wup_ref[...]) + bup_ref[...]
    acts = []
    for t in range(2 * D_FF // SWIGLU_TILE):
        a = t * SWIGLU_TILE
        glu = jnp.minimum(u[:, a:a + LANES], SWIGLU_LIMIT)
        lin = jnp.clip(u[:, a + LANES:a + SWIGLU_TILE], -SWIGLU_LIMIT, SWIGLU_LIMIT)
        acts.append((glu * jax.nn.sigmoid(SWIGLU_ALPHA * glu) * (lin + 1.0)).astype(BF16))
    o_ref[...] = _dot(jnp.concatenate(acts, axis=1), wdn_ref[...]) + bdn_ref[...]


def _deinterleave_kernel(w_ref, o_ref):
    j = lax.broadcasted_iota(I32, (SWIGLU_TILE, SWIGLU_TILE), 0)
    s = lax.broadcasted_iota(I32, (SWIGLU_TILE, SWIGLU_TILE), 1)
    src = jnp.where(s < LANES, 2 * s, 2 * (s - LANES) + 1)
    perm = jnp.where(j == src, 1.0, 0.0).astype(BF16)
    for t in range(w_ref.shape[1] // SWIGLU_TILE):
        a = t * SWIGLU_TILE
        o_ref[:, a:a + SWIGLU_TILE] = _dot(w_ref[:, a:a + SWIGLU_TILE].astype(BF16), perm).astype(BF16)


def _deinterleave_call(w_up):
    e, d, f = w_up.shape
    tr = 512
    return pl.pallas_call(
        _deinterleave_kernel,
        grid=(e, d // tr),
        in_specs=[pl.BlockSpec((None, tr, f), lambda a, b: (a, b, 0))],
        out_specs=pl.BlockSpec((None, tr, f), lambda a, b: (a, b, 0)),
        out_shape=jax.ShapeDtypeStruct((e, d, f), BF16),
        compiler_params=_cparams(("arbitrary", "arbitrary")),
        name="w_up_tiles",
    )(w_up)


def _ffn_call(block_expert, xs, wup, bup, wdn, bdn, tile):
    n_rows, d = xs.shape
    n_blocks = n_rows // tile
    grid_spec = pltpu.PrefetchScalarGridSpec(
        num_scalar_prefetch=1,
        grid=(n_blocks,),
        in_specs=[pl.BlockSpec((tile, d), lambda i, be: (i, 0)),
                  pl.BlockSpec((None, d, 2 * D_FF), lambda i, be: (be[i], 0, 0)),
                  pl.BlockSpec((None, 1, 2 * D_FF), lambda i, be: (be[i], 0, 0)),
                  pl.BlockSpec((None, D_FF, d), lambda i, be: (be[i], 0, 0)),
                  pl.BlockSpec((None, 1, d), lambda i, be: (be[i], 0, 0))],
        out_specs=pl.BlockSpec((tile, d), lambda i, be: (i, 0)),
    )
    return pl.pallas_call(
        _ffn_kernel,
        grid_spec=grid_spec,
        out_shape=jax.ShapeDtypeStruct((n_rows, d), F32),
        compiler_params=_cparams(("arbitrary",)),
        name="expert_ffn",
    )(block_expert, xs, wup, bup, wdn, bdn)


def _combine_kernel(pos0_ref, posn_ref, rows_hbm, x1_ref, tg_ref, mod_ref, g_ref, b_ref, o_ref, buf, sem, *, tm):
    i = pl.program_id(0)
    nb = pl.num_programs(0)
    k = TOP_K_EXPERTS

    def row_copy(src, slot, r, j):
        return pltpu.make_async_copy(rows_hbm.at[pl.ds(src, 1), :], buf.at[slot, j, pl.ds(r, 1), :], sem.at[slot])

    def issue(pos_ref, slot):
        def body(r, carry):
            for j in range(k):
                row_copy(pos_ref[0, r * k + j], slot, r, j).start()
            return carry
        lax.fori_loop(0, tm, body, 0, unroll=4)

    slot = i % 2

    @pl.when(i == 0)
    def _():
        issue(pos0_ref, 0)

    @pl.when(i + 1 < nb)
    def _():
        issue(posn_ref, 1 - slot)

    def wait_body(r, carry):
        for j in range(k):
            row_copy(0, slot, r, j).wait()
        return carry
    lax.fori_loop(0, tm, wait_body, 0, unroll=4)

    gated = [buf[slot, j] * tg_ref[:, j:j + 1] for j in range(k)]
    y = (gated[0] + gated[1]) + (gated[2] + gated[3])
    o_ref[...] = _layer_norm(DEEPNORM_ALPHA * x1_ref[...] + mod_ref[5:6, :] * y, g_ref[...], b_ref[...])


def _combine_call(pos, rows, x1, tg, mod, seq_len, ln_g, ln_b, tm):
    n, d = x1.shape
    nb = n // tm
    per_seq = seq_len // tm
    pos3 = pos.reshape(nb, 1, tm * TOP_K_EXPERTS)
    return pl.pallas_call(
        functools.partial(_combine_kernel, tm=tm),
        grid=(nb,),
        in_specs=[pl.BlockSpec((None, 1, tm * TOP_K_EXPERTS), lambda i: (0, 0, 0), memory_space=pltpu.SMEM),
                  pl.BlockSpec((None, 1, tm * TOP_K_EXPERTS), lambda i: (jnp.minimum(i + 1, nb - 1), 0, 0),
                               memory_space=pltpu.SMEM),
                  pl.BlockSpec(memory_space=pl.ANY),
                  pl.BlockSpec((tm, d), lambda i: (i, 0)),
                  pl.BlockSpec((tm, LANES), lambda i: (i, 0)),
                  pl.BlockSpec((None, 6, d), lambda i: (i // per_seq, 0, 0)),
                  pl.BlockSpec((1, d), lambda i: (0, 0)),
                  pl.BlockSpec((1, d), lambda i: (0, 0))],
        out_specs=pl.BlockSpec((tm, d), lambda i: (i, 0)),
        out_shape=jax.ShapeDtypeStruct((n, d), F32),
        scratch_shapes=[pltpu.VMEM((2, TOP_K_EXPERTS, tm, d), F32), pltpu.SemaphoreType.DMA((2,))],
        compiler_params=_cparams(("arbitrary",)),
        name="moe_combine_ln2",
    )(pos3, pos3, rows, x1, tg, mod, ln_g, ln_b)


def _routing(top_idx, tile):
    n_tok = top_idx.shape[0]
    n_assign = n_tok * TOP_K_EXPERTS
    e_flat = top_idx.reshape(-1)
    onehot = (e_flat[:, None] == jnp.arange(N_EXPERTS, dtype=I32)[None, :]).astype(I32)
    csum = jnp.cumsum(onehot, axis=0)
    rank = jnp.sum(onehot * csum, axis=1) - 1
    counts = csum[-1]
    padded = (counts + tile - 1) // tile * tile
    pend = jnp.cumsum(padded)
    pstart = pend - padded
    dest = (pstart[e_flat] + rank).astype(I32)
    n_rows = (n_assign + N_EXPERTS * (tile - 1) + tile - 1) // tile * tile
    n_blocks = n_rows // tile
    block_expert = jnp.minimum(
        jnp.searchsorted(pend, jnp.arange(n_blocks, dtype=I32) * tile, side="right"), N_EXPERTS - 1).astype(I32)
    last_tile = jnp.where(padded > 0, pend - tile, -1).astype(I32)
    nused = (pend[-1] // tile).astype(I32).reshape(1)
    return dest, block_expert, last_tile, nused, n_rows


def _stream(x, mod, kv_cache, weights, btiles, tm, tq):
    (w_packed, b_packed, wsb, wdsa, wout, ln1_g, ln1_b, wr_hi, wr_lo, br,
     wup, bup, wdn, bdn, ln2_g, ln2_b) = weights
    s, t, d = x.shape
    (qa, ka32, va32, ka16, va16, qb, kb32, vb32, kbd, vx, qi, ki32, kid, wi, sg) = _proj_call(x, mod, w_packed, b_packed, tm)

    if kv_cache is None:
        k_sb, v_sb, k_id, k_bd, v_x = ka16, va16, kid, kbd, vx
        total = t
        assert tq % ATT_BLOCK == 0
        diag_fn = lambda i: i * (tq // ATT_BLOCK)
        adm_fn = lambda i, r, key_pos: key_pos < i * tq + (r // CHUNK + 1) * CHUNK
    else:
        past = kv_cache[0].shape[1]
        total = past + t
        pad = (-total) % (2 * ATT_BLOCK)
        cat = lambda cache, new: jnp.concatenate(
            [cache, new, jnp.zeros((s, pad, new.shape[2]), new.dtype)], axis=1)
        k_sb, v_sb, k_id, k_bd, v_x = [cat(c_, n_) for c_, n_ in zip(kv_cache, (ka16, va16, kid, kbd, vx))]
        assert past % ATT_BLOCK == 0 and t <= ATT_BLOCK
        diag_fn = lambda i: i * 0 + past // ATT_BLOCK
        adm_fn = lambda i, r, key_pos: key_pos < total
    topk = max(1, min(TOPK_MAX, total // 4))

    if kv_cache is None:
        oa = _sb_call(qa, k_sb, v_sb, min(SB_BLOCK, t), lambda i: i)
    else:
        assert kv_cache[0].shape[1] % SB_BLOCK == 0 and t <= SB_BLOCK
        oa = _sb_call(qa, k_sb, v_sb, t, lambda i: i * 0 + kv_cache[0].shape[1] // SB_BLOCK)
    ob = _dsa_call(qi, wi, qb, k_id, k_bd, v_x, btiles, tq, topk, diag_fn, adm_fn)
    x1, h2, ti, tg = _post_call(oa, ob, sg, x, mod, wsb, wdsa, wout, ln1_g, ln1_b, wr_hi, wr_lo, br, tm)

    n = s * t
    tile = 4 * MOE_BLOCK if n * TOP_K_EXPERTS >= N_EXPERTS * 8 * MOE_BLOCK else MOE_BLOCK
    dest, block_expert, last_tile, nused, n_rows = _routing(ti.reshape(n, LANES)[:, :TOP_K_EXPERTS], tile)
    xs = _dispatch_call(last_tile, nused, dest, h2.reshape(n, d), n_rows, tile)
    rows = _ffn_call(block_expert, xs, wup, bup, wdn, bdn, tile)
    y = _combine_call(dest, rows, x1.reshape(n, d), tg.reshape(n, LANES), mod, t, ln2_g, ln2_b, min(128, t))
    new_rows = (ka32.reshape(1, s, t, SB_HEADS, HEAD_DIM), va32.reshape(1, s, t, SB_HEADS, HEAD_DIM),
                kb32.reshape(1, s, t, DSA_KV_HEADS, HEAD_DIM), vb32.reshape(1, s, t, DSA_KV_HEADS, HEAD_DIM),
                ki32.reshape(1, s, t, IDX_DIM))
    return y.reshape(s, t, d), new_rows


def kernel(x_prompt, x_sample, cache_sb_k, cache_sb_v, cache_dsa_k, cache_dsa_v, cache_idx_k, c_prompt, c_sample, rel_bias, w_ada, b_ada, w_in, b_in, w_o_sb, w_o_dsa, w_out, ln1_g, ln1_b, w_router, b_router, w_up, b_up, w_down, b_down, ln2_g, ln2_b):
    d = D_MODEL
    nb, ns = x_prompt.shape[0], x_sample.shape[0]
    past = cache_sb_k.shape[2]

    mod = _mod_call(jnp.concatenate([c_prompt, c_sample], axis=0), w_ada[0], b_ada[0]).reshape(nb + ns, 6, d)
    btiles = _bias_call(rel_bias)

    w_packed, b_packed = _pack_w_in(w_in[0], b_in[0])
    wr = jnp.concatenate([w_router[0], jnp.zeros((d, LANES - N_EXPERTS), F32)], axis=1)
    wr_hi = wr.astype(BF16)
    wr_lo = (wr - wr_hi.astype(F32)).astype(BF16)
    br = jnp.concatenate([b_router[0], jnp.zeros((LANES - N_EXPERTS,), F32)]).reshape(1, LANES)
    wup = _deinterleave_call(w_up[0])
    bup = b_up[0].reshape(N_EXPERTS, 2 * D_FF // SWIGLU_TILE, LANES, 2).swapaxes(2, 3).reshape(N_EXPERTS, 1, 2 * D_FF)
    weights = (w_packed, b_packed, w_o_sb[0].astype(BF16), w_o_dsa[0].astype(BF16), w_out[0].astype(BF16),
               ln1_g[0].reshape(1, d), ln1_b[0].reshape(1, d), wr_hi, wr_lo, br,
               wup, bup, w_down[0].astype(BF16), b_down[0].reshape(N_EXPERTS, 1, d),
               ln2_g[0].reshape(1, d), ln2_b[0].reshape(1, d))

    dup = lambda a: jnp.concatenate([a[..., :64], a[..., :64], a[..., 64:], a[..., 64:]], axis=-1)
    idx_c = cache_idx_k[0]
    caches = (cache_sb_k[0].reshape(ns, past, SB_HEADS * HEAD_DIM).astype(BF16),
              cache_sb_v[0].reshape(ns, past, SB_HEADS * HEAD_DIM).astype(BF16),
              jnp.concatenate([idx_c, idx_c], axis=-1).astype(BF16),
              dup(cache_dsa_k[0].reshape(ns, past, DSA_KV_HEADS * HEAD_DIM)).astype(BF16),
              _with_ones(cache_dsa_v[0].reshape(ns, past, DSA_KV_HEADS * HEAD_DIM)).astype(BF16))

    t_p, t_s = x_prompt.shape[1], x_sample.shape[1]
    y_p, new_p = _stream(x_prompt, mod[:nb], None, weights, btiles, min(256, t_p), min(DSA_Q_BLOCK, t_p))
    y_s, new_s = _stream(x_sample, mod[nb:], caches, weights, btiles, t_s, t_s)
    return (y_p, y_s) + new_p + new_s
```

```python
import functools

import jax
import jax.numpy as jnp
import numpy as np
from jax import lax
from jax.experimental import pallas as pl
from jax.experimental.pallas import tpu as pltpu

F32 = jnp.float32
BF16 = jnp.bfloat16
I32 = jnp.int32

D_MODEL = 1024
CHUNK = 64
SB_HEADS = 8
HEAD_DIM = 64
DSA_HEADS = 8
DSA_KV_HEADS = 2
IDX_HEADS = 8
IDX_DIM = 64
TOPK_MAX = 256
N_BUCKETS = 32
N_EXPERTS = 32
TOP_K_EXPERTS = 4
D_FF = 1024
SWIGLU_LIMIT = 7.0
SWIGLU_ALPHA = 1.702
MOE_BLOCK = 128
LN_EPS = 1e-5
DEPTH = 1
DEEPNORM_ALPHA = (2.0 * DEPTH) ** 0.25

LANES = 128
ATT_BLOCK = 128
DSA_Q_BLOCK = 256
SB_BLOCK = 256
VMEM_LIMIT = 56 * 1024 * 1024

EXP_ZERO_BELOW = -104.0
NEG_INF_KEY = -2139095041
INT32_MAX = 2147483647
MASKED_LOGIT = -1e30
SWIGLU_TILE = 2 * LANES

_SEG = {}
_off = 0
for _name, _w in (("qa", 512), ("ka", 512), ("va", 512), ("qb", 512), ("kb", 128), ("vb", 128),
                  ("kbd", 256), ("vx", 256), ("qi", 512), ("kid", 128), ("wi", 128),
                  ("ga", 1024), ("gb", 1024)):
    _SEG[_name] = (_off, _off + _w)
    _off += _w
PACKED_COLS = _off


def _cparams(sem):
    return pltpu.CompilerParams(dimension_semantics=sem, vmem_limit_bytes=VMEM_LIMIT)


def _dot(a, b):
    return jnp.dot(a, b, preferred_element_type=F32)


def _dot_nt(a, b):
    return lax.dot_general(a, b, (((1,), (1,)), ((), ())), preferred_element_type=F32)


def _split_bf16(x):
    hi = x.astype(BF16)
    lo = (x - hi.astype(F32)).astype(BF16)
    return hi, lo


def _dot3(a, b_hi, b_lo):
    a_hi, a_lo = _split_bf16(a)
    return _dot(a_hi, b_hi) + (_dot(a_hi, b_lo) + _dot(a_lo, b_hi))


def _mod_kernel(c_ref, w_ref, b_ref, o_ref):
    c = c_ref[...]
    s = c * jax.nn.sigmoid(c)
    w_hi, w_lo = _split_bf16(w_ref[...])
    o_ref[...] = _dot3(s, w_hi, w_lo) + b_ref[...]


def _mod_call(c_all, w_ada, b_ada):
    n, d = c_all.shape
    cols = w_ada.shape[1]
    tn = 1024
    return pl.pallas_call(
        _mod_kernel,
        grid=(cols // tn,),
        in_specs=[pl.BlockSpec((n, d), lambda j: (0, 0)),
                  pl.BlockSpec((d, tn), lambda j: (0, j)),
                  pl.BlockSpec((1, tn), lambda j: (0, j))],
        out_specs=pl.BlockSpec((n, tn), lambda j: (0, j)),
        out_shape=jax.ShapeDtypeStruct((n, cols), F32),
        compiler_params=_cparams(("arbitrary",)),
        name="adaln_mod",
    )(c_all, w_ada, b_ada.reshape(1, cols))


def _proj_kernel(x_ref, mod_ref, w_ref, b_ref,
                 qa_ref, ka32_ref, va32_ref, ka16_ref, va16_ref,
                 qb_ref, kb32_ref, vb32_ref, kbd_ref, vx_ref,
                 qi_ref, ki32_ref, kid_ref, wi_ref, sg_ref):
    sh1 = mod_ref[0:1, :]
    sc1 = mod_ref[1:2, :]
    h = (x_ref[...] * (1.0 + sc1) + sh1).astype(BF16)

    def seg(name):
        a, b = _SEG[name]
        return _dot(h, w_ref[:, a:b]) + b_ref[:, a:b]

    qa_ref[...] = (seg("qa") * HEAD_DIM ** -0.5).astype(BF16)
    ka = seg("ka")
    ka32_ref[...] = ka
    ka16_ref[...] = ka.astype(BF16)
    va = seg("va")
    va32_ref[...] = va
    va16_ref[...] = va.astype(BF16)
    qb_ref[...] = (seg("qb") * HEAD_DIM ** -0.5).astype(BF16)
    kb32_ref[...] = seg("kb")
    vb32_ref[...] = seg("vb")
    kbd_ref[...] = seg("kbd").astype(BF16)
    vx_ref[...] = seg("vx").astype(BF16)
    qi_ref[...] = (seg("qi") * IDX_DIM ** -0.5).astype(BF16)
    kid = seg("kid")
    ki32_ref[...] = kid[:, :IDX_DIM]
    kid_ref[...] = kid.astype(BF16)
    wi_ref[...] = seg("wi") * IDX_HEADS ** -0.5
    a, _ = _SEG["ga"]
    _, b = _SEG["gb"]
    sg_ref[...] = jax.nn.sigmoid(_dot(h, w_ref[:, a:b]) + b_ref[:, a:b])


def _pack_w_in(w_in, b_in):
    offs = np.cumsum((0, 512, 512, 512, 512, 128, 128, 512, 8, 64, 1024, 1024))
    qa, ka, va, qb, kb, vb, qi, wi, ki, ga, gb = [slice(int(offs[i]), int(offs[i + 1])) for i in range(11)]

    def pack(m, fill):
        kb_m, vb_m = m[..., kb], m[..., vb]
        dup = lambda t: jnp.concatenate([t[..., :64], t[..., :64], t[..., 64:], t[..., 64:]], axis=-1)
        wi_m = jnp.concatenate([m[..., wi], jnp.zeros(m.shape[:-1] + (LANES - IDX_HEADS,), m.dtype)], axis=-1)
        return jnp.concatenate([m[..., qa], m[..., ka], m[..., va], m[..., qb], kb_m, vb_m, dup(kb_m),
                                _with_ones(vb_m, fill),
                                m[..., qi], m[..., ki], m[..., ki], wi_m, m[..., ga], m[..., gb]], axis=-1)

    return pack(w_in, 0.0).astype(BF16), pack(b_in.reshape(1, -1), 1.0)


def _with_ones(v, fill=1.0):
    f = jnp.full(v.shape[:-1] + (HEAD_DIM,), fill, v.dtype)
    return jnp.concatenate([v[..., :HEAD_DIM], f, v[..., HEAD_DIM:], f], axis=-1)


def _proj_call(x, mod, w_packed, b_packed, tm):
    s, t, d = x.shape
    nt = t // tm
    row = lambda w: pl.BlockSpec((None, tm, w), lambda b, i: (b, i, 0))
    shp = lambda w, dt: jax.ShapeDtypeStruct((s, t, w), dt)
    outs = [(512, BF16), (512, F32), (512, F32), (512, BF16), (512, BF16),
            (512, BF16), (128, F32), (128, F32), (256, BF16), (256, BF16),
            (512, BF16), (IDX_DIM, F32), (128, BF16), (128, F32), (2048, F32)]
    return pl.pallas_call(
        _proj_kernel,
        grid=(s, nt),
        in_specs=[row(d),
                  pl.BlockSpec((None, 6, d), lambda b, i: (b, 0, 0)),
                  pl.BlockSpec((d, PACKED_COLS), lambda b, i: (0, 0)),
                  pl.BlockSpec((1, PACKED_COLS), lambda b, i: (0, 0))],
        out_specs=[row(w) for w, _ in outs],
        out_shape=[shp(w, dt) for w, dt in outs],
        compiler_params=_cparams(("arbitrary", "arbitrary")),
        name="in_proj",
    )(x, mod, w_packed, b_packed)


def _bias_kernel(tab_ref, o_ref):
    r = lax.broadcasted_iota(I32, (DSA_Q_BLOCK, ATT_BLOCK), 0)
    c = lax.broadcasted_iota(I32, (DSA_Q_BLOCK, ATT_BLOCK), 1)
    half = N_BUCKETS // 2
    max_exact = half // 2
    for j, off in enumerate((ATT_BLOCK, 0, -ATT_BLOCK, -(1 << 20))):
        rel = c - r + off
        n = jnp.abs(rel)
        large = jnp.full_like(n, max_exact)
        for thr in (12, 16, 23, 32, 46, 64, 91):
            large = large + (n >= thr).astype(I32)
        bucket = jnp.where(rel > 0, half, 0) + jnp.where(n < max_exact, n, large)
        for h in range(DSA_HEADS):
            acc = jnp.zeros((DSA_Q_BLOCK, ATT_BLOCK), F32)
            for b in range(N_BUCKETS):
                acc = jnp.where(bucket == b, tab_ref[b, h], acc)
            o_ref[h, j] = acc


def _bias_call(rel_bias):
    return pl.pallas_call(
        _bias_kernel,
        in_specs=[pl.BlockSpec(memory_space=pltpu.SMEM)],
        out_specs=pl.BlockSpec(memory_space=pltpu.VMEM),
        out_shape=jax.ShapeDtypeStruct((DSA_HEADS, 4, DSA_Q_BLOCK, ATT_BLOCK), F32),
        name="t5_bias_tiles",
    )(rel_bias)


def _softplus(z):
    return jnp.maximum(z, 0.0) + jnp.log1p(jnp.exp(-jnp.abs(z)))


def _sb_kernel(q_ref, k_ref, v_ref, o_ref, *, tq, last_fn):
    tk = SB_BLOCK
    last = last_fn(pl.program_id(2))
    lane = lax.broadcasted_iota(I32, (1, LANES), 1)
    low = lane < HEAD_DIM
    q = q_ref[...]
    zero = jnp.zeros_like(q)
    qh = (jnp.where(low, q, zero), jnp.where(low, zero, q))
    uj = lax.broadcasted_iota(I32, (tk, tk), 0)
    us = lax.broadcasted_iota(I32, (tk, tk), 1)
    u_mat = jnp.where(uj > us, 1.0, 0.0).astype(BF16)

    def tile(kb, vis, carries):
        start = pl.multiple_of(kb * tk, tk)
        kblk = k_ref[pl.ds(start, tk), :]
        vblk = v_ref[pl.ds(start, tk), :]
        pv = []
        new_carries = []
        for h in range(2):
            z = _dot_nt(qh[h], kblk)
            sp = _softplus(z)
            lk = -sp if vis is None else jnp.where(vis, -sp, 0.0)
            hi, lo = _split_bf16(lk)
            after = _dot(hi, u_mat) + _dot(lo, u_mat)
            w = jnp.exp((z - sp) + (after + carries[h]))
            if vis is not None:
                w = jnp.where(vis, w, 0.0)
            pv.append(_dot(w.astype(BF16), vblk))
            new_carries.append(carries[h] + jnp.sum(lk, axis=1, keepdims=True))
        return jnp.where(low, pv[0], pv[1]), new_carries

    r = lax.broadcasted_iota(I32, (tq, tk), 0)
    c = lax.broadcasted_iota(I32, (tq, tk), 1)
    zc = jnp.zeros((tq, 1), F32)
    acc, carries = tile(last, c < r, [zc, zc])

    def live(carries):
        return jnp.max(jnp.maximum(carries[0], carries[1])) > EXP_ZERO_BELOW

    def cond(st):
        kb, go, _, _, _ = st
        return jnp.logical_and(kb >= 0, go)

    def body(st):
        kb, _, acc, c0, c1 = st
        pv, nc = tile(kb, None, [c0, c1])
        return kb - 1, live(nc), acc + pv, nc[0], nc[1]

    st = lax.while_loop(cond, body, (last - 1, live(carries), acc, carries[0], carries[1]))
    o_ref[...] = st[2].astype(o_ref.dtype)


def _sb_call(q, k, v, tq, last_fn):
    s, t, _ = q.shape
    tkk = k.shape[1]
    return pl.pallas_call(
        functools.partial(_sb_kernel, tq=tq, last_fn=last_fn),
        grid=(s, SB_HEADS // 2, t // tq),
        in_specs=[pl.BlockSpec((None, tq, LANES), lambda b, p, i: (b, i, p)),
                  pl.BlockSpec((None, tkk, LANES), lambda b, p, i: (b, 0, p)),
                  pl.BlockSpec((None, tkk, LANES), lambda b, p, i: (b, 0, p))],
        out_specs=pl.BlockSpec((None, tq, LANES), lambda b, p, i: (b, i, p)),
        out_shape=jax.ShapeDtypeStruct(q.shape, BF16),
        compiler_params=_cparams(("arbitrary", "arbitrary", "arbitrary")),
        name="stick_breaking",
    )(q, k, v)


def _dsa_kernel(qi_ref, wi_ref, qb_ref, ki_ref, kb_ref, vx_ref, bt_ref, o_ref,
                qis_sc, wb_sc, qbs_sc, key_sc, smax_sc, mx_sc, acc_sc, *, tq, topk, diag_fn, adm_fn):
    tk = ATT_BLOCK
    group = DSA_HEADS // DSA_KV_HEADS
    diag = diag_fn(pl.program_id(1))
    last = diag + (tq + tk - 1) // tk - 1
    lane = lax.broadcasted_iota(I32, (1, LANES), 1)
    low = lane < HEAD_DIM

    for h in range(IDX_HEADS):
        p = h // 2
        t = qi_ref[:, p * LANES:(p + 1) * LANES]
        z = jnp.zeros_like(t)
        qis_sc[h * tq:(h + 1) * tq, :] = jnp.where(low, t, z) if h % 2 == 0 else jnp.where(low, z, t)
        t = qb_ref[:, p * LANES:(p + 1) * LANES]
        g = h % group
        qbs_sc[h // group, g * tq:(g + 1) * tq, :] = jnp.where(low, t, z) if h % 2 == 0 else jnp.where(low, z, t)
        wb_sc[h] = jnp.broadcast_to(wi_ref[:, h:h + 1], (tq, LANES))

    r = lax.broadcasted_iota(I32, (tq, tk), 0)
    c = lax.broadcasted_iota(I32, (tq, tk), 1)
    last_pair = last // 2
    smax_sc[...] = jnp.full((tq, tk), -jnp.inf, F32)

    def score_pair(pi, masked):
        start = pl.multiple_of(pi * (2 * tk), 2 * tk)
        d = _dot_nt(qis_sc[...], ki_ref[pl.ds(start, 2 * tk), :])
        for half in range(2):
            s = jnp.zeros((tq, tk), F32)
            for h in range(IDX_HEADS):
                s = s + wb_sc[h] * jnp.maximum(d[h * tq:(h + 1) * tq, half * tk:(half + 1) * tk], 0.0)
            bits = pltpu.bitcast(s, I32)
            key = jnp.where(bits < 0, bits ^ INT32_MAX, bits)
            if masked:
                adm = adm_fn(pl.program_id(1), r, c + (start + half * tk))
                key = jnp.where(adm, key, NEG_INF_KEY)
                s = jnp.where(adm, s, -jnp.inf)
            key_sc[2 * pi + half] = key
            smax_sc[...] = jnp.maximum(smax_sc[...], s)

    def score_body(pi, carry):
        score_pair(pi, False)
        return carry

    lax.fori_loop(0, last_pair, score_body, 0)
    score_pair(last_pair, True)

    rc = min(tq, ATT_BLOCK)

    def count_ge(*thrs):
        parts = [[] for _ in thrs]
        for r0 in range(0, tq, rc):
            ts = [jnp.broadcast_to(t[r0:r0 + rc], (rc, tk)) if hasattr(t, "shape") else t for t in thrs]

            def body(pi, accs):
                k0 = key_sc[2 * pi, r0:r0 + rc, :]
                k1 = key_sc[2 * pi + 1, r0:r0 + rc, :]
                return tuple(a + (jnp.where(k0 >= t, 1.0, 0.0) + jnp.where(k1 >= t, 1.0, 0.0))
                             for a, t in zip(accs, ts))
            accs = lax.fori_loop(0, last_pair + 1, body, tuple(jnp.zeros((rc, tk), F32) for _ in thrs))
            for p, a in zip(parts, accs):
                p.append(jnp.sum(a, axis=1, keepdims=True))
        return [p[0] if len(p) == 1 else jnp.concatenate(p, axis=0) for p in parts]

    def midpoint(lo, hi):
        return (lo >> 1) + (hi >> 1) + (lo & hi & 1)

    def float_to_key(v):
        bits = pltpu.bitcast(v, I32)
        return jnp.where(bits < 0, bits ^ INT32_MAX, bits)

    kf = float(topk)

    def bis_cond(st):
        return st[0]

    def is_open(lo, hi, cnt_lo):
        return jnp.logical_and(midpoint(lo, hi) != lo, cnt_lo != kf)

    def bis_body(st):
        _, lo, hi, cnt_lo, cnt_hi = st
        for _ in range(2):
            open_ = is_open(lo, hi, cnt_lo)
            piv = midpoint(lo, hi)
            cnt, = count_ge(piv)
            ge = cnt >= kf
            up = jnp.logical_and(open_, ge)
            dn = jnp.logical_and(open_, jnp.logical_not(ge))
            lo = jnp.where(up, piv, lo)
            cnt_lo = jnp.where(up, cnt, cnt_lo)
            hi = jnp.where(dn, piv, hi)
            cnt_hi = jnp.where(dn, cnt, cnt_hi)
        go = jnp.max(jnp.where(is_open(lo, hi, cnt_lo), 1.0, 0.0)) > 0.0
        return go, lo, hi, cnt_lo, cnt_hi

    c_zero, c_pos = count_ge(0, 1)
    pos = c_pos >= kf
    zer = jnp.logical_and(c_zero >= kf, jnp.logical_not(pos))
    hi_max = float_to_key(jnp.max(smax_sc[...], axis=1, keepdims=True)) + 1
    lo0 = jnp.where(pos, 1, jnp.where(zer, 0, NEG_INF_KEY))
    hi0 = jnp.where(pos, hi_max, jnp.where(zer, 1, 0))
    cnt_lo0 = jnp.where(pos, c_pos, jnp.where(zer, c_zero, -1.0))
    cnt_hi0 = jnp.where(pos, 0.0, jnp.where(zer, c_pos, c_zero))
    go0 = jnp.max(jnp.where(is_open(lo0, hi0, cnt_lo0), 1.0, 0.0)) > 0.0
    _, thr, _, cnt_lo, cnt_hi = lax.while_loop(bis_cond, bis_body, (go0, lo0, hi0, cnt_lo0, cnt_hi0))

    tied = cnt_lo > kf

    @pl.when(jnp.max(jnp.where(tied, 1.0, 0.0)) > 0.0)
    def _():
        need = kf - cnt_hi
        prefix_mat = jnp.where(lax.broadcasted_iota(I32, (tk, tk), 0) <= lax.broadcasted_iota(I32, (tk, tk), 1),
                               1.0, 0.0).astype(BF16)

        def demote(kb, before):
            key = key_sc[kb]
            eq = jnp.logical_and(tied, key == thr)
            ind = jnp.where(eq, 1.0, 0.0)
            rank = _dot(ind.astype(BF16), prefix_mat) + before
            key_sc[kb] = jnp.where(jnp.logical_and(eq, rank > need), NEG_INF_KEY, key)
            return before + jnp.sum(ind, axis=1, keepdims=True)

        lax.fori_loop(0, last + 1, demote, jnp.zeros((tq, 1), F32))

    thr_eff = jnp.maximum(thr, NEG_INF_KEY + 1)

    for h in range(DSA_HEADS):
        mx_sc[h] = jnp.full((tq, LANES), MASKED_LOGIT, F32)
        acc_sc[h] = jnp.zeros((tq, LANES), F32)

    def masked_logits(pi, n):
        start = pl.multiple_of(pi * (2 * tk), 2 * tk)
        lg = _dot_nt(qbs_sc[n], kb_ref[pl.ds(start, 2 * tk), n * LANES:(n + 1) * LANES])
        sel = [key_sc[2 * pi + half] >= thr_eff for half in range(2)]
        j = [jnp.clip(diag - (2 * pi + half) + 1, 0, 3) for half in range(2)]
        out = []
        for g in range(group):
            h = n * group + g
            out.append([jnp.where(sel[half],
                                  lg[g * tq:(g + 1) * tq, half * tk:(half + 1) * tk] + bt_ref[h, j[half], 0:tq, :],
                                  MASKED_LOGIT) for half in range(2)])
        return out

    def max_body(pi, carry):
        for n in range(DSA_KV_HEADS):
            for g, (la, lb) in enumerate(masked_logits(pi, n)):
                h = n * group + g
                mx_sc[h] = jnp.maximum(mx_sc[h], jnp.maximum(la, lb))
        return carry

    lax.fori_loop(0, last_pair + 1, max_body, 0)
    for h in range(DSA_HEADS):
        mx_sc[h] = jnp.broadcast_to(jnp.max(mx_sc[h], axis=1, keepdims=True), (tq, LANES))

    def pv_body(pi, carry):
        start = pl.multiple_of(pi * (2 * tk), 2 * tk)
        for n in range(DSA_KV_HEADS):
            ps = []
            for g, (la, lb) in enumerate(masked_logits(pi, n)):
                m = mx_sc[n * group + g]
                ps.append(jnp.concatenate([jnp.exp(la - m).astype(BF16), jnp.exp(lb - m).astype(BF16)], axis=1))
            pv = _dot(jnp.concatenate(ps, axis=0), vx_ref[pl.ds(start, 2 * tk), n * LANES:(n + 1) * LANES])
            for g in range(group):
                h = n * group + g
                acc_sc[h] = acc_sc[h] + pv[g * tq:(g + 1) * tq]
        return carry

    lax.fori_loop(0, last_pair + 1, pv_body, 0)

    def normalised(h):
        a = acc_sc[h]
        return a / pltpu.roll(a, HEAD_DIM, axis=1)

    for p in range(DSA_HEADS // 2):
        o1 = pltpu.roll(normalised(2 * p + 1), HEAD_DIM, axis=1)
        o_ref[:, p * LANES:(p + 1) * LANES] = jnp.where(low, normalised(2 * p), o1).astype(o_ref.dtype)


def _dsa_call(qi, wi, qb, kid, kbd, vx, btiles, tq, topk, diag_fn, adm_fn):
    s, t, _ = qi.shape
    tkk = kid.shape[1]
    assert tkk % (2 * ATT_BLOCK) == 0, "keys must come in whole pairs of blocks"
    nkb = tkk // ATT_BLOCK
    rowq = lambda w: pl.BlockSpec((None, tq, w), lambda b, i: (b, i, 0))
    full = lambda w: pl.BlockSpec((None, tkk, w), lambda b, i: (b, 0, 0))
    return pl.pallas_call(
        functools.partial(_dsa_kernel, tq=tq, topk=topk, diag_fn=diag_fn, adm_fn=adm_fn),
        grid=(s, t // tq),
        in_specs=[rowq(512), rowq(LANES), rowq(512), full(LANES), full(2 * LANES), full(2 * LANES),
                  pl.BlockSpec(btiles.shape, lambda b, i: (0, 0, 0, 0))],
        out_specs=rowq(512),
        out_shape=jax.ShapeDtypeStruct((s, t, 512), BF16),
        scratch_shapes=[pltpu.VMEM((IDX_HEADS * tq, LANES), BF16),
                        pltpu.VMEM((IDX_HEADS, tq, LANES), F32),
                        pltpu.VMEM((DSA_KV_HEADS, DSA_HEADS // DSA_KV_HEADS * tq, LANES), BF16),
                        pltpu.VMEM((nkb, tq, ATT_BLOCK), I32),
                        pltpu.VMEM((tq, ATT_BLOCK), F32),
                        pltpu.VMEM((DSA_HEADS, tq, LANES), F32),
                        pltpu.VMEM((DSA_HEADS, tq, LANES), F32)],
        compiler_params=_cparams(("arbitrary", "arbitrary")),
        name="dsa_attention",
    )(qi, wi, qb, kid, kbd, vx, btiles)


def _layer_norm(x, g, b):
    mu = jnp.mean(x, axis=-1, keepdims=True)
    xc = x - mu
    var = jnp.mean(xc * xc, axis=-1, keepdims=True)
    return xc * lax.rsqrt(var + LN_EPS) * g + b


def _post_kernel(oa_ref, ob_ref, sg_ref, x_ref, mod_ref, wsb_ref, wdsa_ref, wout_ref, g_ref, b_ref,
                 wrh_ref, wrl_ref, br_ref, x1_ref, h2_ref, ti_ref, tg_ref):
    d = D_MODEL
    ya = _dot(oa_ref[...], wsb_ref[...])
    yb = _dot(ob_ref[...], wdsa_ref[...])
    merged = sg_ref[:, :d] * ya + sg_ref[:, d:] * yb
    mix = _dot(merged.astype(BF16), wout_ref[...])
    g1 = mod_ref[2:3, :]
    x1 = _layer_norm(DEEPNORM_ALPHA * x_ref[...] + g1 * mix, g_ref[...], b_ref[...])
    x1_ref[...] = x1
    h2 = x1 * (1.0 + mod_ref[4:5, :]) + mod_ref[3:4, :]
    h2_ref[...] = h2
    logits = _dot3(h2, wrh_ref[...], wrl_ref[...]) + br_ref[...]
    lane = lax.broadcasted_iota(I32, logits.shape, 1).astype(F32)
    neg = -jnp.inf
    cur = jnp.where(lane < N_EXPERTS, logits, neg)
    vals, idxs = [], []
    for _ in range(TOP_K_EXPERTS):
        m = jnp.max(cur, axis=1, keepdims=True)
        idx = jnp.min(jnp.where(cur == m, lane, float(LANES)), axis=1, keepdims=True)
        vals.append(m)
        idxs.append(idx)
        cur = jnp.where(lane == idx, neg, cur)
    es = [jnp.exp(v - vals[0]) for v in vals]
    tot = es[0] + es[1] + es[2] + es[3]
    ti = jnp.zeros(logits.shape, F32)
    tg = jnp.zeros(logits.shape, F32)
    for k in range(TOP_K_EXPERTS):
        ti = jnp.where(lane == k, idxs[k], ti)
        tg = jnp.where(lane == k, es[k] / tot, tg)
    ti_ref[...] = ti.astype(I32)
    tg_ref[...] = tg


def _post_call(oa, ob, sg, x, mod, wsb, wdsa, wout, ln_g, ln_b, wr_hi, wr_lo, br, tm):
    s, t, d = x.shape
    row = lambda w: pl.BlockSpec((None, tm, w), lambda b, i: (b, i, 0))
    const = lambda a: pl.BlockSpec(a.shape, lambda b, i: (0,) * a.ndim)
    shp = lambda w, dt: jax.ShapeDtypeStruct((s, t, w), dt)
    return pl.pallas_call(
        _post_kernel,
        grid=(s, t // tm),
        in_specs=[row(512), row(512), row(2 * d), row(d),
                  pl.BlockSpec((None, 6, d), lambda b, i: (b, 0, 0)),
                  const(wsb), const(wdsa), const(wout), const(ln_g), const(ln_b),
                  const(wr_hi), const(wr_lo), const(br)],
        out_specs=[row(d), row(d), row(LANES), row(LANES)],
        out_shape=[shp(d, F32), shp(d, F32), shp(LANES, I32), shp(LANES, F32)],
        compiler_params=_cparams(("arbitrary", "arbitrary")),
        name="post_attention_router",
    )(oa, ob, sg, x, mod, wsb, wdsa, wout, ln_g, ln_b, wr_hi, wr_lo, br)


def _dispatch_kernel(zoff_ref, nused_ref, dest_ref, h_ref, xs_hbm, buf, zbuf, sem, zsem, *, tm, tile, n_tiles):
    i = pl.program_id(0)
    nb = pl.num_programs(0)
    k = TOP_K_EXPERTS
    slot = i % 2

    def row_copy(dst, slot, r):
        return pltpu.make_async_copy(buf.at[slot, pl.ds(r, 1), :], xs_hbm.at[pl.ds(dst, 1), :], sem.at[slot])

    def wait_all(slot):
        def body(r, carry):
            for _ in range(k):
                row_copy(0, slot, r).wait()
            return carry
        lax.fori_loop(0, tm, body, 0, unroll=4)

    def zero_fill(start):
        return pltpu.make_async_copy(zbuf, xs_hbm.at[pl.ds(start, tile), :], zsem)

    @pl.when(i == 0)
    def _():
        zbuf[...] = jnp.zeros_like(zbuf)
        fills = [(zoff_ref[e] >= 0, pl.multiple_of(jnp.maximum(zoff_ref[e], 0), MOE_BLOCK)) for e in range(N_EXPERTS)]
        fills += [(t >= nused_ref[0], t * tile) for t in range(max(n_tiles - N_EXPERTS - 1, 0), n_tiles)]
        for on, start in fills:
            @pl.when(on)
            def _():
                zero_fill(start).start()
        for on, start in fills:
            @pl.when(on)
            def _():
                zero_fill(start).wait()

    @pl.when(i >= 2)
    def _():
        wait_all(slot)

    buf[slot] = h_ref[...]
    for r in range(tm):
        for j in range(k):
            row_copy(dest_ref[0, r * k + j], slot, r).start()

    @pl.when(i == nb - 1)
    def _():
        wait_all(slot)

        @pl.when(nb >= 2)
        def _():
            wait_all(1 - slot)


def _dispatch_call(zoff, nused, dest, h2, n_rows, tile):
    n, d = h2.shape
    tm = min(MOE_BLOCK, n)
    nb = n // tm
    dest3 = dest.reshape(nb, 1, tm * TOP_K_EXPERTS)
    grid_spec = pltpu.PrefetchScalarGridSpec(
        num_scalar_prefetch=2,
        grid=(nb,),
        in_specs=[pl.BlockSpec((None, 1, tm * TOP_K_EXPERTS), lambda i, zo, nu: (i, 0, 0), memory_space=pltpu.SMEM),
                  pl.BlockSpec((tm, d), lambda i, zo, nu: (i, 0))],
        out_specs=pl.BlockSpec(memory_space=pl.ANY),
        scratch_shapes=[pltpu.VMEM((2, tm, d), F32), pltpu.VMEM((tile, d), F32),
                        pltpu.SemaphoreType.DMA((2,)), pltpu.SemaphoreType.DMA(())],
    )
    return pl.pallas_call(
        functools.partial(_dispatch_kernel, tm=tm, tile=tile, n_tiles=n_rows // tile),
        grid_spec=grid_spec,
        out_shape=jax.ShapeDtypeStruct((n_rows, d), F32),
        compiler_params=_cparams(("arbitrary",)),
        name="moe_dispatch",
    )(zoff, nused, dest3, h2)


def _ffn_kernel(be_ref, x_ref, wup_ref, bup_ref, wdn_ref, bdn_ref, o_ref):
    u = _dot(x_ref[...].astype(BF16), wup_ref[...]) + bup_ref[...]
    acts = []
    for t in range(2 * D_FF // SWIGLU_TILE):
        a = t * SWIGLU_TILE
        glu = jnp.minimum(u[:, a:a + LANES], SWIGLU_LIMIT)
        lin = jnp.clip(u[:, a + LANES:a + SWIGLU_TILE], -SWIGLU_LIMIT, SWIGLU_LIMIT)
        acts.append((glu * jax.nn.sigmoid(SWIGLU_ALPHA * glu) * (lin + 1.0)).astype(BF16))
    o_ref[...] = _dot(jnp.concatenate(acts, axis=1), wdn_ref[...]) + bdn_ref[...]


def _deinterleave_kernel(w_ref, o_ref):
    j = lax.broadcasted_iota(I32, (SWIGLU_TILE, SWIGLU_TILE), 0)
    s = lax.broadcasted_iota(I32, (SWIGLU_TILE, SWIGLU_TILE), 1)
    src = jnp.where(s < LANES, 2 * s, 2 * (s - LANES) + 1)
    perm = jnp.where(j == src, 1.0, 0.0).astype(BF16)
    for t in range(w_ref.shape[1] // SWIGLU_TILE):
        a = t * SWIGLU_TILE
        o_ref[:, a:a + SWIGLU_TILE] = _dot(w_ref[:, a:a + SWIGLU_TILE].astype(BF16), perm).astype(BF16)


def _deinterleave_call(w_up):
    e, d, f = w_up.shape
    tr = 512
    return pl.pallas_call(
        _deinterleave_kernel,
        grid=(e, d // tr),
        in_specs=[pl.BlockSpec((None, tr, f), lambda a, b: (a, b, 0))],
        out_specs=pl.BlockSpec((None, tr, f), lambda a, b: (a, b, 0)),
        out_shape=jax.ShapeDtypeStruct((e, d, f), BF16),
        compiler_params=_cparams(("arbitrary", "arbitrary")),
        name="w_up_tiles",
    )(w_up)


def _ffn_call(block_expert, xs, wup, bup, wdn, bdn, tile):
    n_rows, d = xs.shape
    n_blocks = n_rows // tile
    grid_spec = pltpu.PrefetchScalarGridSpec(
        num_scalar_prefetch=1,
        grid=(n_blocks,),
        in_specs=[pl.BlockSpec((tile, d), lambda i, be: (i, 0)),
                  pl.BlockSpec((None, d, 2 * D_FF), lambda i, be: (be[i], 0, 0)),
                  pl.BlockSpec((None, 1, 2 * D_FF), lambda i, be: (be[i], 0, 0)),
                  pl.BlockSpec((None, D_FF, d), lambda i, be: (be[i], 0, 0)),
                  pl.BlockSpec((None, 1, d), lambda i, be: (be[i], 0, 0))],
        out_specs=pl.BlockSpec((tile, d), lambda i, be: (i, 0)),
    )
    return pl.pallas_call(
        _ffn_kernel,
        grid_spec=grid_spec,
        out_shape=jax.ShapeDtypeStruct((n_rows, d), F32),
        compiler_params=_cparams(("arbitrary",)),
        name="expert_ffn",
    )(block_expert, xs, wup, bup, wdn, bdn)


def _combine_kernel(pos0_ref, posn_ref, rows_hbm, x1_ref, tg_ref, mod_ref, g_ref, b_ref, o_ref, buf, sem, *, tm):
    i = pl.program_id(0)
    nb = pl.num_programs(0)
    k = TOP_K_EXPERTS

    def row_copy(src, slot, r, j):
        return pltpu.make_async_copy(rows_hbm.at[pl.ds(src, 1), :], buf.at[slot, j, pl.ds(r, 1), :], sem.at[slot])

    def issue(pos_ref, slot):
        def body(r, carry):
            for j in range(k):
                row_copy(pos_ref[0, r * k + j], slot, r, j).start()
            return carry
        lax.fori_loop(0, tm, body, 0, unroll=4)

    slot = i % 2

    @pl.when(i == 0)
    def _():
        issue(pos0_ref, 0)

    @pl.when(i + 1 < nb)
    def _():
        issue(posn_ref, 1 - slot)

    def wait_body(r, carry):
        for j in range(k):
            row_copy(0, slot, r, j).wait()
        return carry
    lax.fori_loop(0, tm, wait_body, 0, unroll=4)

    gated = [buf[slot, j] * tg_ref[:, j:j + 1] for j in range(k)]
    y = (gated[0] + gated[1]) + (gated[2] + gated[3])
    o_ref[...] = _layer_norm(DEEPNORM_ALPHA * x1_ref[...] + mod_ref[5:6, :] * y, g_ref[...], b_ref[...])


def _combine_call(pos, rows, x1, tg, mod, seq_len, ln_g, ln_b, tm):
    n, d = x1.shape
    nb = n // tm
    per_seq = seq_len // tm
    pos3 = pos.reshape(nb, 1, tm * TOP_K_EXPERTS)
    return pl.pallas_call(
        functools.partial(_combine_kernel, tm=tm),
        grid=(nb,),
        in_specs=[pl.BlockSpec((None, 1, tm * TOP_K_EXPERTS), lambda i: (0, 0, 0), memory_space=pltpu.SMEM),
                  pl.BlockSpec((None, 1, tm * TOP_K_EXPERTS), lambda i: (jnp.minimum(i + 1, nb - 1), 0, 0),
                               memory_space=pltpu.SMEM),
                  pl.BlockSpec(memory_space=pl.ANY),
                  pl.BlockSpec((tm, d), lambda i: (i, 0)),
                  pl.BlockSpec((tm, LANES), lambda i: (i, 0)),
                  pl.BlockSpec((None, 6, d), lambda i: (i // per_seq, 0, 0)),
                  pl.BlockSpec((1, d), lambda i: (0, 0)),
                  pl.BlockSpec((1, d), lambda i: (0, 0))],
        out_specs=pl.BlockSpec((tm, d), lambda i: (i, 0)),
        out_shape=jax.ShapeDtypeStruct((n, d), F32),
        scratch_shapes=[pltpu.VMEM((2, TOP_K_EXPERTS, tm, d), F32), pltpu.SemaphoreType.DMA((2,))],
        compiler_params=_cparams(("arbitrary",)),
        name="moe_combine_ln2",
    )(pos3, pos3, rows, x1, tg, mod, ln_g, ln_b)


def _routing(top_idx, tile):
    n_tok = top_idx.shape[0]
    n_assign = n_tok * TOP_K_EXPERTS
    e_flat = top_idx.reshape(-1)
    onehot = (e_flat[:, None] == jnp.arange(N_EXPERTS, dtype=I32)[None, :]).astype(I32)
    csum = jnp.cumsum(onehot, axis=0)
    rank = jnp.sum(onehot * csum, axis=1) - 1
    counts = csum[-1]
    padded = (counts + tile - 1) // tile * tile
    pend = jnp.cumsum(padded)
    pstart = pend - padded
    dest = (pstart[e_flat] + rank).astype(I32)
    n_rows = (n_assign + N_EXPERTS * (tile - 1) + tile - 1) // tile * tile
    n_blocks = n_rows // tile
    block_expert = jnp.minimum(
        jnp.searchsorted(pend, jnp.arange(n_blocks, dtype=I32) * tile, side="right"), N_EXPERTS - 1).astype(I32)
    last_tile = jnp.where(padded > 0, pend - tile, -1).astype(I32)
    nused = (pend[-1] // tile).astype(I32).reshape(1)
    return dest, block_expert, last_tile, nused, n_rows


def _stream(x, mod, kv_cache, weights, btiles, tm, tq):
    (w_packed, b_packed, wsb, wdsa, wout, ln1_g, ln1_b, wr_hi, wr_lo, br,
     wup, bup, wdn, bdn, ln2_g, ln2_b) = weights
    s, t, d = x.shape
    (qa, ka32, va32, ka16, va16, qb, kb32, vb32, kbd, vx, qi, ki32, kid, wi, sg) = _proj_call(x, mod, w_packed, b_packed, tm)

    if kv_cache is None:
        k_sb, v_sb, k_id, k_bd, v_x = ka16, va16, kid, kbd, vx
        total = t
        assert tq % ATT_BLOCK == 0
        diag_fn = lambda i: i * (tq // ATT_BLOCK)
        adm_fn = lambda i, r, key_pos: key_pos < i * tq + (r // CHUNK + 1) * CHUNK
    else:
        past = kv_cache[0].shape[1]
        total = past + t
        pad = (-total) % (2 * ATT_BLOCK)
        cat = lambda cache, new: jnp.concatenate(
            [cache, new, jnp.zeros((s, pad, new.shape[2]), new.dtype)], axis=1)
        k_sb, v_sb, k_id, k_bd, v_x = [cat(c_, n_) for c_, n_ in zip(kv_cache, (ka16, va16, kid, kbd, vx))]
        assert past % ATT_BLOCK == 0 and t <= ATT_BLOCK
        diag_fn = lambda i: i * 0 + past // ATT_BLOCK
        adm_fn = lambda i, r, key_pos: key_pos < total
    topk = max(1, min(TOPK_MAX, total // 4))

    if kv_cache is None:
        oa = _sb_call(qa, k_sb, v_sb, min(SB_BLOCK, t), lambda i: i)
    else:
        assert kv_cache[0].shape[1] % SB_BLOCK == 0 and t <= SB_BLOCK
        oa = _sb_call(qa, k_sb, v_sb, t, lambda i: i * 0 + kv_cache[0].shape[1] // SB_BLOCK)
    ob = _dsa_call(qi, wi, qb, k_id, k_bd, v_x, btiles, tq, topk, diag_fn, adm_fn)
    x1, h2, ti, tg = _post_call(oa, ob, sg, x, mod, wsb, wdsa, wout, ln1_g, ln1_b, wr_hi, wr_lo, br, tm)

    n = s * t
    tile = 4 * MOE_BLOCK if n * TOP_K_EXPERTS >= N_EXPERTS * 8 * MOE_BLOCK else MOE_BLOCK
    dest, block_expert, last_tile, nused, n_rows = _routing(ti.reshape(n, LANES)[:, :TOP_K_EXPERTS], tile)
    xs = _dispatch_call(last_tile, nused, dest, h2.reshape(n, d), n_rows, tile)
    rows = _ffn_call(block_expert, xs, wup, bup, wdn, bdn, tile)
    y = _combine_call(dest, rows, x1.reshape(n, d), tg.reshape(n, LANES), mod, t, ln2_g, ln2_b, min(128, t))
    new_rows = (ka32.reshape(1, s, t, SB_HEADS, HEAD_DIM), va32.reshape(1, s, t, SB_HEADS, HEAD_DIM),
                kb32.reshape(1, s, t, DSA_KV_HEADS, HEAD_DIM), vb32.reshape(1, s, t, DSA_KV_HEADS, HEAD_DIM),
                ki32.reshape(1, s, t, IDX_DIM))
    return y.reshape(s, t, d), new_rows


def kernel(x_prompt, x_sample, cache_sb_k, cache_sb_v, cache_dsa_k, cache_dsa_v, cache_idx_k, c_prompt, c_sample, rel_bias, w_ada, b_ada, w_in, b_in, w_o_sb, w_o_dsa, w_out, ln1_g, ln1_b, w_router, b_router, w_up, b_up, w_down, b_down, ln2_g, ln2_b):
    d = D_MODEL
    nb, ns = x_prompt.shape[0], x_sample.shape[0]
    past = cache_sb_k.shape[2]

    mod = _mod_call(jnp.concatenate([c_prompt, c_sample], axis=0), w_ada[0], b_ada[0]).reshape(nb + ns, 6, d)
    btiles = _bias_call(rel_bias)

    w_packed, b_packed = _pack_w_in(w_in[0], b_in[0])
    wr = jnp.concatenate([w_router[0], jnp.zeros((d, LANES - N_EXPERTS), F32)], axis=1)
    wr_hi = wr.astype(BF16)
    wr_lo = (wr - wr_hi.astype(F32)).astype(BF16)
    br = jnp.concatenate([b_router[0], jnp.zeros((LANES - N_EXPERTS,), F32)]).reshape(1, LANES)
    wup = _deinterleave_call(w_up[0])
    bup = b_up[0].reshape(N_EXPERTS, 2 * D_FF // SWIGLU_TILE, LANES, 2).swapaxes(2, 3).reshape(N_EXPERTS, 1, 2 * D_FF)
    weights = (w_packed, b_packed, w_o_sb[0].astype(BF16), w_o_dsa[0].astype(BF16), w_out[0].astype(BF16),
               ln1_g[0].reshape(1, d), ln1_b[0].reshape(1, d), wr_hi, wr_lo, br,
               wup, bup, w_down[0].astype(BF16), b_down[0].reshape(N_EXPERTS, 1, d),
               ln2_g[0].reshape(1, d), ln2_b[0].reshape(1, d))

    dup = lambda a: jnp.concatenate([a[..., :64], a[..., :64], a[..., 64:], a[..., 64:]], axis=-1)
    idx_c = cache_idx_k[0]
    caches = (cache_sb_k[0].reshape(ns, past, SB_HEADS * HEAD_DIM).astype(BF16),
              cache_sb_v[0].reshape(ns, past, SB_HEADS * HEAD_DIM).astype(BF16),
              jnp.concatenate([idx_c, idx_c], axis=-1).astype(BF16),
              dup(cache_dsa_k[0].reshape(ns, past, DSA_KV_HEADS * HEAD_DIM)).astype(BF16),
              _with_ones(cache_dsa_v[0].reshape(ns, past, DSA_KV_HEADS * HEAD_DIM)).astype(BF16))

    t_p, t_s = x_prompt.shape[1], x_sample.shape[1]
    y_p, new_p = _stream(x_prompt, mod[:nb], None, weights, btiles, min(256, t_p), min(DSA_Q_BLOCK, t_p))
    y_s, new_s = _stream(x_sample, mod[nb:], caches, weights, btiles, t_s, t_s)
    return (y_p, y_s) + new_p + new_s
```

```python
import functools

import jax
import jax.numpy as jnp
import numpy as np
from jax import lax
from jax.experimental import pallas as pl
from jax.experimental.pallas import tpu as pltpu

F32 = jnp.float32
BF16 = jnp.bfloat16
I32 = jnp.int32

D_MODEL = 1024
CHUNK = 64
SB_HEADS = 8
HEAD_DIM = 64
DSA_HEADS = 8
DSA_KV_HEADS = 2
IDX_HEADS = 8
IDX_DIM = 64
TOPK_MAX = 256
N_BUCKETS = 32
N_EXPERTS = 32
TOP_K_EXPERTS = 4
D_FF = 1024
SWIGLU_LIMIT = 7.0
SWIGLU_ALPHA = 1.702
MOE_BLOCK = 128
LN_EPS = 1e-5
DEPTH = 1
DEEPNORM_ALPHA = (2.0 * DEPTH) ** 0.25

LANES = 128
ATT_BLOCK = 128
DSA_Q_BLOCK = 256
SB_BLOCK = 256
VMEM_LIMIT = 56 * 1024 * 1024

EXP_ZERO_BELOW = -104.0
NEG_INF_KEY = -2139095041
INT32_MAX = 2147483647
MASKED_LOGIT = -1e30
SOFTMAX_DEN_FLOOR = 1e-30
SWIGLU_TILE = 2 * LANES

_SEG = {}
_off = 0
for _name, _w in (("qa", 512), ("ka", 512), ("va", 512), ("qb", 512), ("kb", 128), ("vb", 128),
                  ("kbd", 256), ("vx", 256), ("qi", 512), ("kid", 128), ("wi", 128),
                  ("ga", 1024), ("gb", 1024)):
    _SEG[_name] = (_off, _off + _w)
    _off += _w
PACKED_COLS = _off


def _cparams(sem):
    return pltpu.CompilerParams(dimension_semantics=sem, vmem_limit_bytes=VMEM_LIMIT)


def _dot(a, b):
    return jnp.dot(a, b, preferred_element_type=F32)


def _dot_nt(a, b):
    return lax.dot_general(a, b, (((1,), (1,)), ((), ())), preferred_element_type=F32)


def _split_bf16(x):
    hi = x.astype(BF16)
    lo = (x - hi.astype(F32)).astype(BF16)
    return hi, lo


def _dot3(a, b_hi, b_lo):
    a_hi, a_lo = _split_bf16(a)
    return _dot(a_hi, b_hi) + (_dot(a_hi, b_lo) + _dot(a_lo, b_hi))


def _mod_kernel(c_ref, w_ref, b_ref, o_ref):
    c = c_ref[...]
    s = c * jax.nn.sigmoid(c)
    w_hi, w_lo = _split_bf16(w_ref[...])
    o_ref[...] = _dot3(s, w_hi, w_lo) + b_ref[...]


def _mod_call(c_all, w_ada, b_ada):
    n, d = c_all.shape
    cols = w_ada.shape[1]
    tn = 1024
    return pl.pallas_call(
        _mod_kernel,
        grid=(cols // tn,),
        in_specs=[pl.BlockSpec((n, d), lambda j: (0, 0)),
                  pl.BlockSpec((d, tn), lambda j: (0, j)),
                  pl.BlockSpec((1, tn), lambda j: (0, j))],
        out_specs=pl.BlockSpec((n, tn), lambda j: (0, j)),
        out_shape=jax.ShapeDtypeStruct((n, cols), F32),
        compiler_params=_cparams(("arbitrary",)),
        name="adaln_mod",
    )(c_all, w_ada, b_ada.reshape(1, cols))


def _proj_kernel(x_ref, mod_ref, w_ref, b_ref,
                 qa_ref, ka32_ref, va32_ref, ka16_ref, va16_ref,
                 qb_ref, kb32_ref, vb32_ref, kbd_ref, vx_ref,
                 qi_ref, ki32_ref, kid_ref, wi_ref, sg_ref):
    sh1 = mod_ref[0:1, :]
    sc1 = mod_ref[1:2, :]
    h = (x_ref[...] * (1.0 + sc1) + sh1).astype(BF16)

    def seg(name):
        a, b = _SEG[name]
        return _dot(h, w_ref[:, a:b]) + b_ref[:, a:b]

    qa_ref[...] = (seg("qa") * HEAD_DIM ** -0.5).astype(BF16)
    ka = seg("ka")
    ka32_ref[...] = ka
    ka16_ref[...] = ka.astype(BF16)
    va = seg("va")
    va32_ref[...] = va
    va16_ref[...] = va.astype(BF16)
    qb_ref[...] = (seg("qb") * HEAD_DIM ** -0.5).astype(BF16)
    kb32_ref[...] = seg("kb")
    vb32_ref[...] = seg("vb")
    kbd_ref[...] = seg("kbd").astype(BF16)
    vx_ref[...] = seg("vx").astype(BF16)
    qi_ref[...] = (seg("qi") * IDX_DIM ** -0.5).astype(BF16)
    kid = seg("kid")
    ki32_ref[...] = kid[:, :IDX_DIM]
    kid_ref[...] = kid.astype(BF16)
    wi_ref[...] = seg("wi") * IDX_HEADS ** -0.5
    a, _ = _SEG["ga"]
    _, b = _SEG["gb"]
    sg_ref[...] = jax.nn.sigmoid(_dot(h, w_ref[:, a:b]) + b_ref[:, a:b])


def _pack_w_in(w_in, b_in):
    offs = np.cumsum((0, 512, 512, 512, 512, 128, 128, 512, 8, 64, 1024, 1024))
    qa, ka, va, qb, kb, vb, qi, wi, ki, ga, gb = [slice(int(offs[i]), int(offs[i + 1])) for i in range(11)]

    def pack(m, fill):
        kb_m, vb_m = m[..., kb], m[..., vb]
        dup = lambda t: jnp.concatenate([t[..., :64], t[..., :64], t[..., 64:], t[..., 64:]], axis=-1)
        wi_m = jnp.concatenate([m[..., wi], jnp.zeros(m.shape[:-1] + (LANES - IDX_HEADS,), m.dtype)], axis=-1)
        return jnp.concatenate([m[..., qa], m[..., ka], m[..., va], m[..., qb], kb_m, vb_m, dup(kb_m),
                                _with_ones(vb_m, fill),
                                m[..., qi], m[..., ki], m[..., ki], wi_m, m[..., ga], m[..., gb]], axis=-1)

    return pack(w_in, 0.0).astype(BF16), pack(b_in.reshape(1, -1), 1.0)


def _with_ones(v, fill=1.0):
    f = jnp.full(v.shape[:-1] + (HEAD_DIM,), fill, v.dtype)
    return jnp.concatenate([v[..., :HEAD_DIM], f, v[..., HEAD_DIM:], f], axis=-1)


def _proj_call(x, mod, w_packed, b_packed, tm):
    s, t, d = x.shape
    nt = t // tm
    row = lambda w: pl.BlockSpec((None, tm, w), lambda b, i: (b, i, 0))
    shp = lambda w, dt: jax.ShapeDtypeStruct((s, t, w), dt)
    outs = [(512, BF16), (512, F32), (512, F32), (512, BF16), (512, BF16),
            (512, BF16), (128, F32), (128, F32), (256, BF16), (256, BF16),
            (512, BF16), (IDX_DIM, F32), (128, BF16), (128, F32), (2048, F32)]
    return pl.pallas_call(
        _proj_kernel,
        grid=(s, nt),
        in_specs=[row(d),
                  pl.BlockSpec((None, 6, d), lambda b, i: (b, 0, 0)),
                  pl.BlockSpec((d, PACKED_COLS), lambda b, i: (0, 0)),
                  pl.BlockSpec((1, PACKED_COLS), lambda b, i: (0, 0))],
        out_specs=[row(w) for w, _ in outs],
        out_shape=[shp(w, dt) for w, dt in outs],
        compiler_params=_cparams(("arbitrary", "arbitrary")),
        name="in_proj",
    )(x, mod, w_packed, b_packed)


def _bias_kernel(tab_ref, o_ref, max_ref):
    r = lax.broadcasted_iota(I32, (DSA_Q_BLOCK, ATT_BLOCK), 0)
    c = lax.broadcasted_iota(I32, (DSA_Q_BLOCK, ATT_BLOCK), 1)
    half = N_BUCKETS // 2
    max_exact = half // 2
    for j, off in enumerate((ATT_BLOCK, 0, -ATT_BLOCK, -(1 << 20))):
        rel = c - r + off
        n = jnp.abs(rel)
        large = jnp.full_like(n, max_exact)
        for thr in (12, 16, 23, 32, 46, 64, 91):
            large = large + (n >= thr).astype(I32)
        bucket = jnp.where(rel > 0, half, 0) + jnp.where(n < max_exact, n, large)
        for h in range(DSA_HEADS):
            acc = jnp.zeros((DSA_Q_BLOCK, ATT_BLOCK), F32)
            for b in range(N_BUCKETS):
                acc = jnp.where(bucket == b, tab_ref[b, h], acc)
            o_ref[h, j] = acc
    for h in range(DSA_HEADS):
        top = tab_ref[0, h]
        for b in range(1, N_BUCKETS):
            top = jnp.maximum(top, tab_ref[b, h])
        max_ref[h] = jnp.full((8, LANES), top, F32)


def _bias_call(rel_bias):
    return pl.pallas_call(
        _bias_kernel,
        in_specs=[pl.BlockSpec(memory_space=pltpu.SMEM)],
        out_specs=[pl.BlockSpec(memory_space=pltpu.VMEM), pl.BlockSpec(memory_space=pltpu.VMEM)],
        out_shape=[jax.ShapeDtypeStruct((DSA_HEADS, 4, DSA_Q_BLOCK, ATT_BLOCK), F32),
                   jax.ShapeDtypeStruct((DSA_HEADS, 8, LANES), F32)],
        name="t5_bias_tiles",
    )(rel_bias)


def _softplus(z):
    return jnp.maximum(z, 0.0) + jnp.log1p(jnp.exp(-jnp.abs(z)))


def _sb_kernel(q_ref, k_ref, v_ref, o_ref, *, tq, last_fn):
    tk = SB_BLOCK
    last = last_fn(pl.program_id(2))
    lane = lax.broadcasted_iota(I32, (1, LANES), 1)
    low = lane < HEAD_DIM
    q = q_ref[...]
    zero = jnp.zeros_like(q)
    qh = (jnp.where(low, q, zero), jnp.where(low, zero, q))
    uj = lax.broadcasted_iota(I32, (tk, tk), 0)
    us = lax.broadcasted_iota(I32, (tk, tk), 1)
    u_mat = jnp.where(uj > us, 1.0, 0.0).astype(BF16)

    def tile(kb, vis, carries):
        start = pl.multiple_of(kb * tk, tk)
        kblk = k_ref[pl.ds(start, tk), :]
        vblk = v_ref[pl.ds(start, tk), :]
        pv = []
        new_carries = []
        for h in range(2):
            z = _dot_nt(qh[h], kblk)
            sp = _softplus(z)
            lk = -sp if vis is None else jnp.where(vis, -sp, 0.0)
            hi, lo = _split_bf16(lk)
            after = _dot(hi, u_mat) + _dot(lo, u_mat)
            w = jnp.exp((z - sp) + (after + carries[h]))
            if vis is not None:
                w = jnp.where(vis, w, 0.0)
            pv.append(_dot(w.astype(BF16), vblk))
            new_carries.append(carries[h] + jnp.sum(lk, axis=1, keepdims=True))
        return jnp.where(low, pv[0], pv[1]), new_carries

    r = lax.broadcasted_iota(I32, (tq, tk), 0)
    c = lax.broadcasted_iota(I32, (tq, tk), 1)
    zc = jnp.zeros((tq, 1), F32)
    acc, carries = tile(last, c < r, [zc, zc])

    def live(carries):
        return jnp.max(jnp.maximum(carries[0], carries[1])) > EXP_ZERO_BELOW

    def cond(st):
        kb, go, _, _, _ = st
        return jnp.logical_and(kb >= 0, go)

    def body(st):
        kb, _, acc, c0, c1 = st
        pv, nc = tile(kb, None, [c0, c1])
        return kb - 1, live(nc), acc + pv, nc[0], nc[1]

    st = lax.while_loop(cond, body, (last - 1, live(carries), acc, carries[0], carries[1]))
    o_ref[...] = st[2].astype(o_ref.dtype)


def _sb_call(q, k, v, tq, last_fn):
    s, t, _ = q.shape
    tkk = k.shape[1]
    return pl.pallas_call(
        functools.partial(_sb_kernel, tq=tq, last_fn=last_fn),
        grid=(s, SB_HEADS // 2, t // tq),
        in_specs=[pl.BlockSpec((None, tq, LANES), lambda b, p, i: (b, i, p)),
                  pl.BlockSpec((None, tkk, LANES), lambda b, p, i: (b, 0, p)),
                  pl.BlockSpec((None, tkk, LANES), lambda b, p, i: (b, 0, p))],
        out_specs=pl.BlockSpec((None, tq, LANES), lambda b, p, i: (b, i, p)),
        out_shape=jax.ShapeDtypeStruct(q.shape, BF16),
        compiler_params=_cparams(("arbitrary", "arbitrary", "arbitrary")),
        name="stick_breaking",
    )(q, k, v)


def _dsa_kernel(qi_ref, wi_ref, qb_ref, ki_ref, kb_ref, vx_ref, bt_ref, bmax_ref, o_ref,
                qis_sc, wb_sc, qbs_sc, key_sc, smax_sc, mx_sc, acc_sc, kn_sc, *, tq, topk, diag_fn, adm_fn):
    tk = ATT_BLOCK
    group = DSA_HEADS // DSA_KV_HEADS

    @pl.when(pl.program_id(1) == 0)
    def _():
        for n in range(DSA_KV_HEADS):
            def body(pi, best):
                start = pl.multiple_of(pi * (2 * tk), 2 * tk)
                x = kb_ref[pl.ds(start, 2 * tk), n * LANES:(n + 1) * LANES].astype(F32)
                return jnp.maximum(best, jnp.sum(x * x, axis=1, keepdims=True))
            best = lax.fori_loop(0, kb_ref.shape[0] // (2 * tk), body, jnp.zeros((2 * tk, 1), F32))
            kn_sc[n] = jnp.broadcast_to(jnp.sqrt(0.5 * jnp.max(best, axis=0, keepdims=True)), (8, LANES))

    diag = diag_fn(pl.program_id(1))
    last = diag + (tq + tk - 1) // tk - 1
    lane = lax.broadcasted_iota(I32, (1, LANES), 1)
    low = lane < HEAD_DIM

    for h in range(IDX_HEADS):
        p = h // 2
        t = qi_ref[:, p * LANES:(p + 1) * LANES]
        z = jnp.zeros_like(t)
        qis_sc[h * tq:(h + 1) * tq, :] = jnp.where(low, t, z) if h % 2 == 0 else jnp.where(low, z, t)
        t = qb_ref[:, p * LANES:(p + 1) * LANES]
        g = h % group
        qbs_sc[h // group, g * tq:(g + 1) * tq, :] = jnp.where(low, t, z) if h % 2 == 0 else jnp.where(low, z, t)
        wb_sc[h] = jnp.broadcast_to(wi_ref[:, h:h + 1], (tq, LANES))

    r = lax.broadcasted_iota(I32, (tq, tk), 0)
    c = lax.broadcasted_iota(I32, (tq, tk), 1)
    last_pair = last // 2
    smax_sc[...] = jnp.full((tq, tk), -jnp.inf, F32)

    def score_pair(pi, masked):
        start = pl.multiple_of(pi * (2 * tk), 2 * tk)
        d = _dot_nt(qis_sc[...], ki_ref[pl.ds(start, 2 * tk), :])
        for half in range(2):
            s = jnp.zeros((tq, tk), F32)
            for h in range(IDX_HEADS):
                s = s + wb_sc[h] * jnp.maximum(d[h * tq:(h + 1) * tq, half * tk:(half + 1) * tk], 0.0)
            bits = pltpu.bitcast(s, I32)
            key = jnp.where(bits < 0, bits ^ INT32_MAX, bits)
            if masked:
                adm = adm_fn(pl.program_id(1), r, c + (start + half * tk))
                key = jnp.where(adm, key, NEG_INF_KEY)
                s = jnp.where(adm, s, -jnp.inf)
            key_sc[2 * pi + half] = key
            smax_sc[...] = jnp.maximum(smax_sc[...], s)

    def score_body(pi, carry):
        score_pair(pi, False)
        return carry

    lax.fori_loop(0, last_pair, score_body, 0)
    score_pair(last_pair, True)

    rc = min(tq, ATT_BLOCK)

    def count_ge(*thrs):
        parts = [[] for _ in thrs]
        for r0 in range(0, tq, rc):
            ts = [jnp.broadcast_to(t[r0:r0 + rc], (rc, tk)) if hasattr(t, "shape") else t for t in thrs]

            def body(pi, accs):
                k0 = key_sc[2 * pi, r0:r0 + rc, :]
                k1 = key_sc[2 * pi + 1, r0:r0 + rc, :]
                return tuple(a + (jnp.where(k0 >= t, 1.0, 0.0) + jnp.where(k1 >= t, 1.0, 0.0))
                             for a, t in zip(accs, ts))
            accs = lax.fori_loop(0, last_pair + 1, body, tuple(jnp.zeros((rc, tk), F32) for _ in thrs))
            for p, a in zip(parts, accs):
                p.append(jnp.sum(a, axis=1, keepdims=True))
        return [p[0] if len(p) == 1 else jnp.concatenate(p, axis=0) for p in parts]

    def midpoint(lo, hi):
        return (lo >> 1) + (hi >> 1) + (lo & hi & 1)

    def float_to_key(v):
        bits = pltpu.bitcast(v, I32)
        return jnp.where(bits < 0, bits ^ INT32_MAX, bits)

    kf = float(topk)

    def bis_cond(st):
        return st[0]

    def is_open(piv, lo, cnt_lo):
        return jnp.logical_and(piv != lo, cnt_lo != kf)

    def bis_body(st):
        _, lo, hi, cnt_lo = st
        for _ in range(2):
            piv = midpoint(lo, hi)
            open_ = is_open(piv, lo, cnt_lo)
            cnt, = count_ge(piv)
            ge = cnt >= kf
            up = jnp.logical_and(open_, ge)
            lo = jnp.where(up, piv, lo)
            cnt_lo = jnp.where(up, cnt, cnt_lo)
            hi = jnp.where(jnp.logical_and(open_, jnp.logical_not(ge)), piv, hi)
        go = jnp.max(jnp.where(is_open(midpoint(lo, hi), lo, cnt_lo), 1.0, 0.0)) > 0.0
        return go, lo, hi, cnt_lo

    c_zero, c_pos = count_ge(0, 1)
    pos = c_pos >= kf
    zer = jnp.logical_and(c_zero >= kf, jnp.logical_not(pos))
    hi_max = float_to_key(jnp.max(smax_sc[...], axis=1, keepdims=True)) + 1
    lo0 = jnp.where(pos, 1, jnp.where(zer, 0, NEG_INF_KEY))
    hi0 = jnp.where(pos, hi_max, jnp.where(zer, 1, 0))
    cnt_lo0 = jnp.where(pos, c_pos, jnp.where(zer, c_zero, -1.0))
    go0 = jnp.max(jnp.where(is_open(midpoint(lo0, hi0), lo0, cnt_lo0), 1.0, 0.0)) > 0.0
    _, thr, _, cnt_lo = lax.while_loop(bis_cond, bis_body, (go0, lo0, hi0, cnt_lo0))

    tied = cnt_lo > kf

    @pl.when(jnp.max(jnp.where(tied, 1.0, 0.0)) > 0.0)
    def _():
        above, = count_ge(thr + 1)
        need = kf - above
        prefix_mat = jnp.where(lax.broadcasted_iota(I32, (tk, tk), 0) <= lax.broadcasted_iota(I32, (tk, tk), 1),
                               1.0, 0.0).astype(BF16)

        def demote(kb, before):
            key = key_sc[kb]
            eq = jnp.logical_and(tied, key == thr)
            ind = jnp.where(eq, 1.0, 0.0)
            rank = _dot(ind.astype(BF16), prefix_mat) + before
            key_sc[kb] = jnp.where(jnp.logical_and(eq, rank > need), NEG_INF_KEY, key)
            return before + jnp.sum(ind, axis=1, keepdims=True)

        lax.fori_loop(0, last + 1, demote, jnp.zeros((tq, 1), F32))

    thr_eff = jnp.maximum(thr, NEG_INF_KEY + 1)

    def masked_logits(pi, n):
        start = pl.multiple_of(pi * (2 * tk), 2 * tk)
        lg = _dot_nt(qbs_sc[n], kb_ref[pl.ds(start, 2 * tk), n * LANES:(n + 1) * LANES])
        sel = [key_sc[2 * pi + half] >= thr_eff for half in range(2)]
        j = [jnp.clip(diag - (2 * pi + half) + 1, 0, 3) for half in range(2)]
        out = []
        for g in range(group):
            h = n * group + g
            out.append([jnp.where(sel[half],
                                  lg[g * tq:(g + 1) * tq, half * tk:(half + 1) * tk] + bt_ref[h, j[half], 0:tq, :],
                                  MASKED_LOGIT) for half in range(2)])
        return out

    def max_body(pi, carry):
        for n in range(DSA_KV_HEADS):
            for g, (la, lb) in enumerate(masked_logits(pi, n)):
                h = n * group + g
                mx_sc[h] = jnp.maximum(mx_sc[h], jnp.maximum(la, lb))
        return carry

    def exact_row_maxima():
        for h in range(DSA_HEADS):
            mx_sc[h] = jnp.full((tq, LANES), MASKED_LOGIT, F32)
        lax.fori_loop(0, last_pair + 1, max_body, 0)
        for h in range(DSA_HEADS):
            mx_sc[h] = jnp.broadcast_to(jnp.max(mx_sc[h], axis=1, keepdims=True), (tq, LANES))

    def bounded_row_maxima():
        for h in range(DSA_HEADS):
            n, g = h // group, h % group
            q = qbs_sc[n, g * tq:(g + 1) * tq, :].astype(F32)
            q_norm = jnp.sqrt(jnp.sum(q * q, axis=1, keepdims=True))
            mx_sc[h] = q_norm * kn_sc[n, 0:1, :] + bmax_ref[h, 0:1, :]

    def pv_body(pi, carry):
        start = pl.multiple_of(pi * (2 * tk), 2 * tk)
        for n in range(DSA_KV_HEADS):
            ps = []
            for g, (la, lb) in enumerate(masked_logits(pi, n)):
                m = mx_sc[n * group + g]
                ps.append(jnp.concatenate([jnp.exp(la - m).astype(BF16), jnp.exp(lb - m).astype(BF16)], axis=1))
            pv = _dot(jnp.concatenate(ps, axis=0), vx_ref[pl.ds(start, 2 * tk), n * LANES:(n + 1) * LANES])
            for g in range(group):
                h = n * group + g
                acc_sc[h] = acc_sc[h] + pv[g * tq:(g + 1) * tq]
        return carry

    def weights_sweep():
        for h in range(DSA_HEADS):
            acc_sc[h] = jnp.zeros((tq, LANES), F32)
        lax.fori_loop(0, last_pair + 1, pv_body, 0)

    bounded_row_maxima()
    weights_sweep()
    den = acc_sc[0][:, HEAD_DIM:HEAD_DIM + 1]
    for h in range(1, DSA_HEADS):
        den = jnp.minimum(den, acc_sc[h][:, HEAD_DIM:HEAD_DIM + 1])
    healthy = jnp.min(jnp.where(den >= SOFTMAX_DEN_FLOOR, 1.0, 0.0)) > 0.0

    @pl.when(jnp.logical_not(healthy))
    def _():
        exact_row_maxima()
        weights_sweep()

    def normalised(h):
        a = acc_sc[h]
        return a / pltpu.roll(a, HEAD_DIM, axis=1)

    for p in range(DSA_HEADS // 2):
        o1 = pltpu.roll(normalised(2 * p + 1), HEAD_DIM, axis=1)
        o_ref[:, p * LANES:(p + 1) * LANES] = jnp.where(low, normalised(2 * p), o1).astype(o_ref.dtype)


def _dsa_call(qi, wi, qb, kid, kbd, vx, btiles, bmax, tq, topk, diag_fn, adm_fn):
    s, t, _ = qi.shape
    tkk = kid.shape[1]
    assert tkk % (2 * ATT_BLOCK) == 0, "keys must come in whole pairs of blocks"
    nkb = tkk // ATT_BLOCK
    rowq = lambda w: pl.BlockSpec((None, tq, w), lambda b, i: (b, i, 0))
    full = lambda w: pl.BlockSpec((None, tkk, w), lambda b, i: (b, 0, 0))
    return pl.pallas_call(
        functools.partial(_dsa_kernel, tq=tq, topk=topk, diag_fn=diag_fn, adm_fn=adm_fn),
        grid=(s, t // tq),
        in_specs=[rowq(512), rowq(LANES), rowq(512), full(LANES), full(2 * LANES), full(2 * LANES),
                  pl.BlockSpec(btiles.shape, lambda b, i: (0, 0, 0, 0)),
                  pl.BlockSpec(bmax.shape, lambda b, i: (0, 0, 0))],
        out_specs=rowq(512),
        out_shape=jax.ShapeDtypeStruct((s, t, 512), BF16),
        scratch_shapes=[pltpu.VMEM((IDX_HEADS * tq, LANES), BF16),
                        pltpu.VMEM((IDX_HEADS, tq, LANES), F32),
                        pltpu.VMEM((DSA_KV_HEADS, DSA_HEADS // DSA_KV_HEADS * tq, LANES), BF16),
                        pltpu.VMEM((nkb, tq, ATT_BLOCK), I32),
                        pltpu.VMEM((tq, ATT_BLOCK), F32),
                        pltpu.VMEM((DSA_HEADS, tq, LANES), F32),
                        pltpu.VMEM((DSA_HEADS, tq, LANES), F32),
                        pltpu.VMEM((DSA_KV_HEADS, 8, LANES), F32)],
        compiler_params=_cparams(("arbitrary", "arbitrary")),
        name="dsa_attention",
    )(qi, wi, qb, kid, kbd, vx, btiles, bmax)


def _layer_norm(x, g, b):
    mu = jnp.mean(x, axis=-1, keepdims=True)
    xc = x - mu
    var = jnp.mean(xc * xc, axis=-1, keepdims=True)
    return xc * lax.rsqrt(var + LN_EPS) * g + b


def _post_kernel(oa_ref, ob_ref, sg_ref, x_ref, mod_ref, wsb_ref, wdsa_ref, wout_ref, g_ref, b_ref,
                 wrh_ref, wrl_ref, br_ref, x1_ref, h2_ref, ti_ref, tg_ref):
    d = D_MODEL
    ya = _dot(oa_ref[...], wsb_ref[...])
    yb = _dot(ob_ref[...], wdsa_ref[...])
    merged = sg_ref[:, :d] * ya + sg_ref[:, d:] * yb
    mix = _dot(merged.astype(BF16), wout_ref[...])
    g1 = mod_ref[2:3, :]
    x1 = _layer_norm(DEEPNORM_ALPHA * x_ref[...] + g1 * mix, g_ref[...], b_ref[...])
    x1_ref[...] = x1
    h2 = x1 * (1.0 + mod_ref[4:5, :]) + mod_ref[3:4, :]
    h2_ref[...] = h2
    logits = _dot3(h2, wrh_ref[...], wrl_ref[...]) + br_ref[...]
    lane = lax.broadcasted_iota(I32, logits.shape, 1).astype(F32)
    neg = -jnp.inf
    cur = jnp.where(lane < N_EXPERTS, logits, neg)
    vals, idxs = [], []
    for _ in range(TOP_K_EXPERTS):
        m = jnp.max(cur, axis=1, keepdims=True)
        idx = jnp.min(jnp.where(cur == m, lane, float(LANES)), axis=1, keepdims=True)
        vals.append(m)
        idxs.append(idx)
        cur = jnp.where(lane == idx, neg, cur)
    es = [jnp.exp(v - vals[0]) for v in vals]
    tot = es[0] + es[1] + es[2] + es[3]
    ti = jnp.zeros(logits.shape, F32)
    tg = jnp.zeros(logits.shape, F32)
    for k in range(TOP_K_EXPERTS):
        ti = jnp.where(lane == k, idxs[k], ti)
        tg = jnp.where(lane == k, es[k] / tot, tg)
    ti_ref[...] = ti.astype(I32)
    tg_ref[...] = tg


def _post_call(oa, ob, sg, x, mod, wsb, wdsa, wout, ln_g, ln_b, wr_hi, wr_lo, br, tm):
    s, t, d = x.shape
    row = lambda w: pl.BlockSpec((None, tm, w), lambda b, i: (b, i, 0))
    const = lambda a: pl.BlockSpec(a.shape, lambda b, i: (0,) * a.ndim)
    shp = lambda w, dt: jax.ShapeDtypeStruct((s, t, w), dt)
    return pl.pallas_call(
        _post_kernel,
        grid=(s, t // tm),
        in_specs=[row(512), row(512), row(2 * d), row(d),
                  pl.BlockSpec((None, 6, d), lambda b, i: (b, 0, 0)),
                  const(wsb), const(wdsa), const(wout), const(ln_g), const(ln_b),
                  const(wr_hi), const(wr_lo), const(br)],
        out_specs=[row(d), row(d), row(LANES), row(LANES)],
        out_shape=[shp(d, F32), shp(d, F32), shp(LANES, I32), shp(LANES, F32)],
        compiler_params=_cparams(("arbitrary", "arbitrary")),
        name="post_attention_router",
    )(oa, ob, sg, x, mod, wsb, wdsa, wout, ln_g, ln_b, wr_hi, wr_lo, br)


def _dispatch_kernel(zoff_ref, nused_ref, dest_ref, h_ref, xs_hbm, buf, zbuf, sem, zsem, *, tm, tile, n_tiles):
    i = pl.program_id(0)
    nb = pl.num_programs(0)
    k = TOP_K_EXPERTS
    slot = i % 2

    def row_copy(dst, slot, r):
        return pltpu.make_async_copy(buf.at[slot, pl.ds(r, 1), :], xs_hbm.at[pl.ds(dst, 1), :], sem.at[slot])

    def wait_all(slot):
        def body(r, carry):
            for _ in range(k):
                row_copy(0, slot, r).wait()
            return carry
        lax.fori_loop(0, tm, body, 0, unroll=4)

    def zero_fill(start):
        return pltpu.make_async_copy(zbuf, xs_hbm.at[pl.ds(start, tile), :], zsem)

    @pl.when(i == 0)
    def _():
        zbuf[...] = jnp.zeros_like(zbuf)
        fills = [(zoff_ref[e] >= 0, pl.multiple_of(jnp.maximum(zoff_ref[e], 0), MOE_BLOCK)) for e in range(N_EXPERTS)]
        fills += [(t >= nused_ref[0], t * tile) for t in range(max(n_tiles - N_EXPERTS - 1, 0), n_tiles)]
        for on, start in fills:
            @pl.when(on)
            def _():
                zero_fill(start).start()
        for on, start in fills:
            @pl.when(on)
            def _():
                zero_fill(start).wait()

    @pl.when(i >= 2)
    def _():
        wait_all(slot)

    buf[slot] = h_ref[...]
    for r in range(tm):
        for j in range(k):
            row_copy(dest_ref[0, r * k + j], slot, r).start()

    @pl.when(i == nb - 1)
    def _():
        wait_all(slot)

        @pl.when(nb >= 2)
        def _():
            wait_all(1 - slot)


def _dispatch_call(zoff, nused, dest, h2, n_rows, tile):
    n, d = h2.shape
    tm = min(MOE_BLOCK, n)
    nb = n // tm
    dest3 = dest.reshape(nb, 1, tm * TOP_K_EXPERTS)
    grid_spec = pltpu.PrefetchScalarGridSpec(
        num_scalar_prefetch=2,
        grid=(nb,),
        in_specs=[pl.BlockSpec((None, 1, tm * TOP_K_EXPERTS), lambda i, zo, nu: (i, 0, 0), memory_space=pltpu.SMEM),
                  pl.BlockSpec((tm, d), lambda i, zo, nu: (i, 0))],
        out_specs=pl.BlockSpec(memory_space=pl.ANY),
        scratch_shapes=[pltpu.VMEM((2, tm, d), F32), pltpu.VMEM((tile, d), F32),
                        pltpu.SemaphoreType.DMA((2,)), pltpu.SemaphoreType.DMA(())],
    )
    return pl.pallas_call(
        functools.partial(_dispatch_kernel, tm=tm, tile=tile, n_tiles=n_rows // tile),
        grid_spec=grid_spec,
        out_shape=jax.ShapeDtypeStruct((n_rows, d), F32),
        compiler_params=_cparams(("arbitrary",)),
        name="moe_dispatch",
    )(zoff, nused, dest3, h2)


def _ffn_kernel(be_ref, x_ref, wup_ref, bup_ref, wdn_ref, bdn_ref, o_ref):
    u = _dot(x_ref[...].astype(BF16), wup_ref[...]) + bup_ref[...]
    acts = []
    for t in range(2 * D_FF // SWIGLU_TILE):
        a = t * SWIGLU_TILE
        glu = jnp.minimum(u[:, a:a + LANES], SWIGLU_LIMIT)
        lin = jnp.clip(u[:, a + LANES:a + SWIGLU_TILE], -SWIGLU_LIMIT, SWIGLU_LIMIT)
        acts.append((glu * jax.nn.sigmoid(SWIGLU_ALPHA * glu) * (lin + 1.0)).astype(BF16))
    o_ref[...] = _dot(jnp.concatenate(acts, axis=1), wdn_ref[...]) + bdn_ref[...]


def _deinterleave_kernel(w_ref, o_ref):
    j = lax.broadcasted_iota(I32, (SWIGLU_TILE, SWIGLU_TILE), 0)
    s = lax.broadcasted_iota(I32, (SWIGLU_TILE, SWIGLU_TILE), 1)
    src = jnp.where(s < LANES, 2 * s, 2 * (s - LANES) + 1)
    perm = jnp.where(j == src, 1.0, 0.0).astype(BF16)
    for t in range(w_ref.shape[1] // SWIGLU_TILE):
        a = t * SWIGLU_TILE
        o_ref[:, a:a + SWIGLU_TILE] = _dot(w_ref[:, a:a + SWIGLU_TILE].astype(BF16), perm).astype(BF16)


def _deinterleave_call(w_up):
    e, d, f = w_up.shape
    tr = 512
    return pl.pallas_call(
        _deinterleave_kernel,
        grid=(e, d // tr),
        in_specs=[pl.BlockSpec((None, tr, f), lambda a, b: (a, b, 0))],
        out_specs=pl.BlockSpec((None, tr, f), lambda a, b: (a, b, 0)),
        out_shape=jax.ShapeDtypeStruct((e, d, f), BF16),
        compiler_params=_cparams(("arbitrary", "arbitrary")),
        name="w_up_tiles",
    )(w_up)


def _ffn_call(block_expert, xs, wup, bup, wdn, bdn, tile):
    n_rows, d = xs.shape
    n_blocks = n_rows // tile
    grid_spec = pltpu.PrefetchScalarGridSpec(
        num_scalar_prefetch=1,
        grid=(n_blocks,),
        in_specs=[pl.BlockSpec((tile, d), lambda i, be: (i, 0)),
                  pl.BlockSpec((None, d, 2 * D_FF), lambda i, be: (be[i], 0, 0)),
                  pl.BlockSpec((None, 1, 2 * D_FF), lambda i, be: (be[i], 0, 0)),
                  pl.BlockSpec((None, D_FF, d), lambda i, be: (be[i], 0, 0)),
                  pl.BlockSpec((None, 1, d), lambda i, be: (be[i], 0, 0))],
        out_specs=pl.BlockSpec((tile, d), lambda i, be: (i, 0)),
    )
    return pl.pallas_call(
        _ffn_kernel,
        grid_spec=grid_spec,
        out_shape=jax.ShapeDtypeStruct((n_rows, d), F32),
        compiler_params=_cparams(("arbitrary",)),
        name="expert_ffn",
    )(block_expert, xs, wup, bup, wdn, bdn)


def _combine_kernel(pos0_ref, posn_ref, rows_hbm, x1_ref, tg_ref, mod_ref, g_ref, b_ref, o_ref, buf, sem, *, tm):
    i = pl.program_id(0)
    nb = pl.num_programs(0)
    k = TOP_K_EXPERTS

    def row_copy(src, slot, r, j):
        return pltpu.make_async_copy(rows_hbm.at[pl.ds(src, 1), :], buf.at[slot, j, pl.ds(r, 1), :], sem.at[slot])

    def issue(pos_ref, slot):
        def body(r, carry):
            for j in range(k):
                row_copy(pos_ref[0, r * k + j], slot, r, j).start()
            return carry
        lax.fori_loop(0, tm, body, 0, unroll=4)

    slot = i % 2

    @pl.when(i == 0)
    def _():
        issue(pos0_ref, 0)

    @pl.when(i + 1 < nb)
    def _():
        issue(posn_ref, 1 - slot)

    def wait_body(r, carry):
        for j in range(k):
            row_copy(0, slot, r, j).wait()
        return carry
    lax.fori_loop(0, tm, wait_body, 0, unroll=4)

    gated = [buf[slot, j] * tg_ref[:, j:j + 1] for j in range(k)]
    y = (gated[0] + gated[1]) + (gated[2] + gated[3])
    o_ref[...] = _layer_norm(DEEPNORM_ALPHA * x1_ref[...] + mod_ref[5:6, :] * y, g_ref[...], b_ref[...])


def _combine_call(pos, rows, x1, tg, mod, seq_len, ln_g, ln_b, tm):
    n, d = x1.shape
    nb = n // tm
    per_seq = seq_len // tm
    pos3 = pos.reshape(nb, 1, tm * TOP_K_EXPERTS)
    return pl.pallas_call(
        functools.partial(_combine_kernel, tm=tm),
        grid=(nb,),
        in_specs=[pl.BlockSpec((None, 1, tm * TOP_K_EXPERTS), lambda i: (0, 0, 0), memory_space=pltpu.SMEM),
                  pl.BlockSpec((None, 1, tm * TOP_K_EXPERTS), lambda i: (jnp.minimum(i + 1, nb - 1), 0, 0),
                               memory_space=pltpu.SMEM),
                  pl.BlockSpec(memory_space=pl.ANY),
                  pl.BlockSpec((tm, d), lambda i: (i, 0)),
                  pl.BlockSpec((tm, LANES), lambda i: (i, 0)),
                  pl.BlockSpec((None, 6, d), lambda i: (i // per_seq, 0, 0)),
                  pl.BlockSpec((1, d), lambda i: (0, 0)),
                  pl.BlockSpec((1, d), lambda i: (0, 0))],
        out_specs=pl.BlockSpec((tm, d), lambda i: (i, 0)),
        out_shape=jax.ShapeDtypeStruct((n, d), F32),
        scratch_shapes=[pltpu.VMEM((2, TOP_K_EXPERTS, tm, d), F32), pltpu.SemaphoreType.DMA((2,))],
        compiler_params=_cparams(("arbitrary",)),
        name="moe_combine_ln2",
    )(pos3, pos3, rows, x1, tg, mod, ln_g, ln_b)


def _routing(top_idx, tile):
    n_tok = top_idx.shape[0]
    n_assign = n_tok * TOP_K_EXPERTS
    e_flat = top_idx.reshape(-1)
    onehot = (e_flat[:, None] == jnp.arange(N_EXPERTS, dtype=I32)[None, :]).astype(I32)
    csum = jnp.cumsum(onehot, axis=0)
    rank = jnp.sum(onehot * csum, axis=1) - 1
    counts = csum[-1]
    padded = (counts + tile - 1) // tile * tile
    pend = jnp.cumsum(padded)
    pstart = pend - padded
    dest = (pstart[e_flat] + rank).astype(I32)
    n_rows = (n_assign + N_EXPERTS * (tile - 1) + tile - 1) // tile * tile
    n_blocks = n_rows // tile
    block_expert = jnp.minimum(
        jnp.searchsorted(pend, jnp.arange(n_blocks, dtype=I32) * tile, side="right"), N_EXPERTS - 1).astype(I32)
    last_tile = jnp.where(padded > 0, pend - tile, -1).astype(I32)
    nused = (pend[-1] // tile).astype(I32).reshape(1)
    return dest, block_expert, last_tile, nused, n_rows


def _stream(x, mod, kv_cache, weights, btiles, tm, tq):
    (w_packed, b_packed, wsb, wdsa, wout, ln1_g, ln1_b, wr_hi, wr_lo, br,
     wup, bup, wdn, bdn, ln2_g, ln2_b) = weights
    s, t, d = x.shape
    (qa, ka32, va32, ka16, va16, qb, kb32, vb32, kbd, vx, qi, ki32, kid, wi, sg) = _proj_call(x, mod, w_packed, b_packed, tm)

    if kv_cache is None:
        k_sb, v_sb, k_id, k_bd, v_x = ka16, va16, kid, kbd, vx
        total = t
        assert tq % ATT_BLOCK == 0
        diag_fn = lambda i: i * (tq // ATT_BLOCK)
        adm_fn = lambda i, r, key_pos: key_pos < i * tq + (r // CHUNK + 1) * CHUNK
    else:
        past = kv_cache[0].shape[1]
        total = past + t
        pad = (-total) % (2 * ATT_BLOCK)
        cat = lambda cache, new: jnp.concatenate(
            [cache, new, jnp.zeros((s, pad, new.shape[2]), new.dtype)], axis=1)
        k_sb, v_sb, k_id, k_bd, v_x = [cat(c_, n_) for c_, n_ in zip(kv_cache, (ka16, va16, kid, kbd, vx))]
        assert past % ATT_BLOCK == 0 and t <= ATT_BLOCK
        diag_fn = lambda i: i * 0 + past // ATT_BLOCK
        adm_fn = lambda i, r, key_pos: key_pos < total
    topk = max(1, min(TOPK_MAX, total // 4))

    if kv_cache is None:
        oa = _sb_call(qa, k_sb, v_sb, min(SB_BLOCK, t), lambda i: i)
    else:
        assert kv_cache[0].shape[1] % SB_BLOCK == 0 and t <= SB_BLOCK
        oa = _sb_call(qa, k_sb, v_sb, t, lambda i: i * 0 + kv_cache[0].shape[1] // SB_BLOCK)
    ob = _dsa_call(qi, wi, qb, k_id, k_bd, v_x, *btiles, tq, topk, diag_fn, adm_fn)
    x1, h2, ti, tg = _post_call(oa, ob, sg, x, mod, wsb, wdsa, wout, ln1_g, ln1_b, wr_hi, wr_lo, br, tm)

    n = s * t
    tile = 4 * MOE_BLOCK if n * TOP_K_EXPERTS >= N_EXPERTS * 8 * MOE_BLOCK else MOE_BLOCK
    dest, block_expert, last_tile, nused, n_rows = _routing(ti.reshape(n, LANES)[:, :TOP_K_EXPERTS], tile)
    xs = _dispatch_call(last_tile, nused, dest, h2.reshape(n, d), n_rows, tile)
    rows = _ffn_call(block_expert, xs, wup, bup, wdn, bdn, tile)
    y = _combine_call(dest, rows, x1.reshape(n, d), tg.reshape(n, LANES), mod, t, ln2_g, ln2_b, min(128, t))
    new_rows = (ka32.reshape(1, s, t, SB_HEADS, HEAD_DIM), va32.reshape(1, s, t, SB_HEADS, HEAD_DIM),
                kb32.reshape(1, s, t, DSA_KV_HEADS, HEAD_DIM), vb32.reshape(1, s, t, DSA_KV_HEADS, HEAD_DIM),
                ki32.reshape(1, s, t, IDX_DIM))
    return y.reshape(s, t, d), new_rows


def kernel(x_prompt, x_sample, cache_sb_k, cache_sb_v, cache_dsa_k, cache_dsa_v, cache_idx_k, c_prompt, c_sample, rel_bias, w_ada, b_ada, w_in, b_in, w_o_sb, w_o_dsa, w_out, ln1_g, ln1_b, w_router, b_router, w_up, b_up, w_down, b_down, ln2_g, ln2_b):
    d = D_MODEL
    nb, ns = x_prompt.shape[0], x_sample.shape[0]
    past = cache_sb_k.shape[2]

    mod = _mod_call(jnp.concatenate([c_prompt, c_sample], axis=0), w_ada[0], b_ada[0]).reshape(nb + ns, 6, d)
    btiles = _bias_call(rel_bias)

    w_packed, b_packed = _pack_w_in(w_in[0], b_in[0])
    wr = jnp.concatenate([w_router[0], jnp.zeros((d, LANES - N_EXPERTS), F32)], axis=1)
    wr_hi = wr.astype(BF16)
    wr_lo = (wr - wr_hi.astype(F32)).astype(BF16)
    br = jnp.concatenate([b_router[0], jnp.zeros((LANES - N_EXPERTS,), F32)]).reshape(1, LANES)
    wup = _deinterleave_call(w_up[0])
    bup = b_up[0].reshape(N_EXPERTS, 2 * D_FF // SWIGLU_TILE, LANES, 2).swapaxes(2, 3).reshape(N_EXPERTS, 1, 2 * D_FF)
    weights = (w_packed, b_packed, w_o_sb[0].astype(BF16), w_o_dsa[0].astype(BF16), w_out[0].astype(BF16),
               ln1_g[0].reshape(1, d), ln1_b[0].reshape(1, d), wr_hi, wr_lo, br,
               wup, bup, w_down[0].astype(BF16), b_down[0].reshape(N_EXPERTS, 1, d),
               ln2_g[0].reshape(1, d), ln2_b[0].reshape(1, d))

    dup = lambda a: jnp.concatenate([a[..., :64], a[..., :64], a[..., 64:], a[..., 64:]], axis=-1)
    idx_c = cache_idx_k[0]
    caches = (cache_sb_k[0].reshape(ns, past, SB_HEADS * HEAD_DIM).astype(BF16),
              cache_sb_v[0].reshape(ns, past, SB_HEADS * HEAD_DIM).astype(BF16),
              jnp.concatenate([idx_c, idx_c], axis=-1).astype(BF16),
              dup(cache_dsa_k[0].reshape(ns, past, DSA_KV_HEADS * HEAD_DIM)).astype(BF16),
              _with_ones(cache_dsa_v[0].reshape(ns, past, DSA_KV_HEADS * HEAD_DIM)).astype(BF16))

    t_p, t_s = x_prompt.shape[1], x_sample.shape[1]
    y_p, new_p = _stream(x_prompt, mod[:nb], None, weights, btiles, min(256, t_p), min(DSA_Q_BLOCK, t_p))
    y_s, new_s = _stream(x_sample, mod[nb:], caches, weights, btiles, t_s, t_s)
    return (y_p, y_s) + new_p + new_s
```

```python
import functools

import jax
import jax.numpy as jnp
import numpy as np
from jax import lax
from jax.experimental import pallas as pl
from jax.experimental.pallas import tpu as pltpu

F32 = jnp.float32
BF16 = jnp.bfloat16
I32 = jnp.int32

D_MODEL = 1024
CHUNK = 64
SB_HEADS = 8
HEAD_DIM = 64
DSA_HEADS = 8
DSA_KV_HEADS = 2
IDX_HEADS = 8
IDX_DIM = 64
TOPK_MAX = 256
N_BUCKETS = 32
N_EXPERTS = 32
TOP_K_EXPERTS = 4
D_FF = 1024
SWIGLU_LIMIT = 7.0
SWIGLU_ALPHA = 1.702
MOE_BLOCK = 128
LN_EPS = 1e-5
DEPTH = 1
DEEPNORM_ALPHA = (2.0 * DEPTH) ** 0.25

LANES = 128
ATT_BLOCK = 128
DSA_Q_BLOCK = 256
SB_BLOCK = 256
VMEM_LIMIT = 56 * 1024 * 1024

EXP_ZERO_BELOW = -104.0
NEG_INF_KEY = -2139095041
INT32_MAX = 2147483647
MASKED_LOGIT = -1e30
SOFTMAX_DEN_FLOOR = 1e-30
SWIGLU_TILE = 2 * LANES

_SEG = {}
_off = 0
for _name, _w in (("qa", 512), ("ka", 512), ("va", 512), ("qb", 512), ("kb", 128), ("vb", 128),
                  ("kbd", 256), ("vx", 256), ("qi", 512), ("kid", 128), ("wi", 128),
                  ("ga", 1024), ("gb", 1024)):
    _SEG[_name] = (_off, _off + _w)
    _off += _w
PACKED_COLS = _off


def _cparams(sem):
    return pltpu.CompilerParams(dimension_semantics=sem, vmem_limit_bytes=VMEM_LIMIT)


def _dot(a, b):
    return jnp.dot(a, b, preferred_element_type=F32)


def _dot_nt(a, b):
    return lax.dot_general(a, b, (((1,), (1,)), ((), ())), preferred_element_type=F32)


def _split_bf16(x):
    hi = x.astype(BF16)
    lo = (x - hi.astype(F32)).astype(BF16)
    return hi, lo


def _dot3(a, b_hi, b_lo):
    a_hi, a_lo = _split_bf16(a)
    return _dot(a_hi, b_hi) + (_dot(a_hi, b_lo) + _dot(a_lo, b_hi))


def _mod_kernel(c_ref, w_ref, b_ref, o_ref):
    c = c_ref[...]
    s = c * jax.nn.sigmoid(c)
    w_hi, w_lo = _split_bf16(w_ref[...])
    o_ref[...] = _dot3(s, w_hi, w_lo) + b_ref[...]


def _mod_call(c_all, w_ada, b_ada):
    n, d = c_all.shape
    cols = w_ada.shape[1]
    tn = 1024
    return pl.pallas_call(
        _mod_kernel,
        grid=(cols // tn,),
        in_specs=[pl.BlockSpec((n, d), lambda j: (0, 0)),
                  pl.BlockSpec((d, tn), lambda j: (0, j)),
                  pl.BlockSpec((1, tn), lambda j: (0, j))],
        out_specs=pl.BlockSpec((n, tn), lambda j: (0, j)),
        out_shape=jax.ShapeDtypeStruct((n, cols), F32),
        compiler_params=_cparams(("arbitrary",)),
        name="adaln_mod",
    )(c_all, w_ada, b_ada.reshape(1, cols))


def _proj_kernel(x_ref, mod_ref, w_ref, b_ref,
                 qa_ref, ka32_ref, va32_ref, ka16_ref, va16_ref,
                 qb_ref, kb32_ref, vb32_ref, kbd_ref, vx_ref,
                 qi_ref, ki32_ref, kid_ref, wi_ref, sg_ref):
    sh1 = mod_ref[0:1, :]
    sc1 = mod_ref[1:2, :]
    h = (x_ref[...] * (1.0 + sc1) + sh1).astype(BF16)

    def seg(name):
        a, b = _SEG[name]
        return _dot(h, w_ref[:, a:b]) + b_ref[:, a:b]

    qa_ref[...] = (seg("qa") * HEAD_DIM ** -0.5).astype(BF16)
    ka = seg("ka")
    ka32_ref[...] = ka
    ka16_ref[...] = ka.astype(BF16)
    va = seg("va")
    va32_ref[...] = va
    va16_ref[...] = va.astype(BF16)
    qb_ref[...] = (seg("qb") * HEAD_DIM ** -0.5).astype(BF16)
    kb32_ref[...] = seg("kb")
    vb32_ref[...] = seg("vb")
    kbd_ref[...] = seg("kbd").astype(BF16)
    vx_ref[...] = seg("vx").astype(BF16)
    qi_ref[...] = (seg("qi") * IDX_DIM ** -0.5).astype(BF16)
    kid = seg("kid")
    ki32_ref[...] = kid[:, :IDX_DIM]
    kid_ref[...] = kid.astype(BF16)
    wi_ref[...] = seg("wi") * IDX_HEADS ** -0.5
    a, _ = _SEG["ga"]
    _, b = _SEG["gb"]
    sg_ref[...] = jax.nn.sigmoid(_dot(h, w_ref[:, a:b]) + b_ref[:, a:b])


def _pack_w_in(w_in, b_in):
    offs = np.cumsum((0, 512, 512, 512, 512, 128, 128, 512, 8, 64, 1024, 1024))
    qa, ka, va, qb, kb, vb, qi, wi, ki, ga, gb = [slice(int(offs[i]), int(offs[i + 1])) for i in range(11)]

    def pack(m, fill):
        kb_m, vb_m = m[..., kb], m[..., vb]
        dup = lambda t: jnp.concatenate([t[..., :64], t[..., :64], t[..., 64:], t[..., 64:]], axis=-1)
        wi_m = jnp.concatenate([m[..., wi], jnp.zeros(m.shape[:-1] + (LANES - IDX_HEADS,), m.dtype)], axis=-1)
        return jnp.concatenate([m[..., qa], m[..., ka], m[..., va], m[..., qb], kb_m, vb_m, dup(kb_m),
                                _with_ones(vb_m, fill),
                                m[..., qi], m[..., ki], m[..., ki], wi_m, m[..., ga], m[..., gb]], axis=-1)

    return pack(w_in, 0.0).astype(BF16), pack(b_in.reshape(1, -1), 1.0)


def _with_ones(v, fill=1.0):
    f = jnp.full(v.shape[:-1] + (HEAD_DIM,), fill, v.dtype)
    return jnp.concatenate([v[..., :HEAD_DIM], f, v[..., HEAD_DIM:], f], axis=-1)


def _proj_call(x, mod, w_packed, b_packed, tm):
    s, t, d = x.shape
    nt = t // tm
    row = lambda w: pl.BlockSpec((None, tm, w), lambda b, i: (b, i, 0))
    shp = lambda w, dt: jax.ShapeDtypeStruct((s, t, w), dt)
    outs = [(512, BF16), (512, F32), (512, F32), (512, BF16), (512, BF16),
            (512, BF16), (128, F32), (128, F32), (256, BF16), (256, BF16),
            (512, BF16), (IDX_DIM, F32), (128, BF16), (128, F32), (2048, F32)]
    return pl.pallas_call(
        _proj_kernel,
        grid=(s, nt),
        in_specs=[row(d),
                  pl.BlockSpec((None, 6, d), lambda b, i: (b, 0, 0)),
                  pl.BlockSpec((d, PACKED_COLS), lambda b, i: (0, 0)),
                  pl.BlockSpec((1, PACKED_COLS), lambda b, i: (0, 0))],
        out_specs=[row(w) for w, _ in outs],
        out_shape=[shp(w, dt) for w, dt in outs],
        compiler_params=_cparams(("arbitrary", "arbitrary")),
        name="in_proj",
    )(x, mod, w_packed, b_packed)


def _bias_kernel(tab_ref, o_ref, max_ref):
    r = lax.broadcasted_iota(I32, (DSA_Q_BLOCK, ATT_BLOCK), 0)
    c = lax.broadcasted_iota(I32, (DSA_Q_BLOCK, ATT_BLOCK), 1)
    half = N_BUCKETS // 2
    max_exact = half // 2
    for j, off in enumerate((ATT_BLOCK, 0, -ATT_BLOCK, -(1 << 20))):
        rel = c - r + off
        n = jnp.abs(rel)
        large = jnp.full_like(n, max_exact)
        for thr in (12, 16, 23, 32, 46, 64, 91):
            large = large + (n >= thr).astype(I32)
        bucket = jnp.where(rel > 0, half, 0) + jnp.where(n < max_exact, n, large)
        for h in range(DSA_HEADS):
            acc = jnp.zeros((DSA_Q_BLOCK, ATT_BLOCK), F32)
            for b in range(N_BUCKETS):
                acc = jnp.where(bucket == b, tab_ref[b, h], acc)
            o_ref[h, j] = acc
    for h in range(DSA_HEADS):
        top = tab_ref[0, h]
        for b in range(1, N_BUCKETS):
            top = jnp.maximum(top, tab_ref[b, h])
        max_ref[h] = jnp.full((8, LANES), top, F32)


def _bias_call(rel_bias):
    return pl.pallas_call(
        _bias_kernel,
        in_specs=[pl.BlockSpec(memory_space=pltpu.SMEM)],
        out_specs=[pl.BlockSpec(memory_space=pltpu.VMEM), pl.BlockSpec(memory_space=pltpu.VMEM)],
        out_shape=[jax.ShapeDtypeStruct((DSA_HEADS, 4, DSA_Q_BLOCK, ATT_BLOCK), F32),
                   jax.ShapeDtypeStruct((DSA_HEADS, 8, LANES), F32)],
        name="t5_bias_tiles",
    )(rel_bias)


def _softplus(z):
    return jnp.maximum(z, 0.0) + jnp.log1p(jnp.exp(-jnp.abs(z)))


def _sb_kernel(q_ref, k_ref, v_ref, o_ref, *, tq, last_fn):
    tk = SB_BLOCK
    last = last_fn(pl.program_id(2))
    lane = lax.broadcasted_iota(I32, (1, LANES), 1)
    low = lane < HEAD_DIM
    q = q_ref[...]
    zero = jnp.zeros_like(q)
    qh = (jnp.where(low, q, zero), jnp.where(low, zero, q))
    uj = lax.broadcasted_iota(I32, (tk, tk), 0)
    us = lax.broadcasted_iota(I32, (tk, tk), 1)
    u_mat = jnp.where(uj > us, 1.0, 0.0).astype(BF16)

    def tile(kb, vis, carries):
        start = pl.multiple_of(kb * tk, tk)
        kblk = k_ref[pl.ds(start, tk), :]
        vblk = v_ref[pl.ds(start, tk), :]
        pv = []
        new_carries = []
        for h in range(2):
            z = _dot_nt(qh[h], kblk)
            sp = _softplus(z)
            lk = -sp if vis is None else jnp.where(vis, -sp, 0.0)
            hi, lo = _split_bf16(lk)
            after = _dot(hi, u_mat) + _dot(lo, u_mat)
            w = jnp.exp((z - sp) + (after + carries[h]))
            if vis is not None:
                w = jnp.where(vis, w, 0.0)
            pv.append(_dot(w.astype(BF16), vblk))
            new_carries.append(carries[h] + jnp.sum(lk, axis=1, keepdims=True))
        return jnp.where(low, pv[0], pv[1]), new_carries

    r = lax.broadcasted_iota(I32, (tq, tk), 0)
    c = lax.broadcasted_iota(I32, (tq, tk), 1)
    zc = jnp.zeros((tq, 1), F32)
    acc, carries = tile(last, c < r, [zc, zc])

    def live(carries):
        return jnp.max(jnp.maximum(carries[0], carries[1])) > EXP_ZERO_BELOW

    def cond(st):
        kb, go, _, _, _ = st
        return jnp.logical_and(kb >= 0, go)

    def body(st):
        kb, _, acc, c0, c1 = st
        pv, nc = tile(kb, None, [c0, c1])
        return kb - 1, live(nc), acc + pv, nc[0], nc[1]

    st = lax.while_loop(cond, body, (last - 1, live(carries), acc, carries[0], carries[1]))
    o_ref[...] = st[2].astype(o_ref.dtype)


def _sb_call(q, k, v, tq, last_fn):
    s, t, _ = q.shape
    tkk = k.shape[1]
    return pl.pallas_call(
        functools.partial(_sb_kernel, tq=tq, last_fn=last_fn),
        grid=(s, SB_HEADS // 2, t // tq),
        in_specs=[pl.BlockSpec((None, tq, LANES), lambda b, p, i: (b, i, p)),
                  pl.BlockSpec((None, tkk, LANES), lambda b, p, i: (b, 0, p)),
                  pl.BlockSpec((None, tkk, LANES), lambda b, p, i: (b, 0, p))],
        out_specs=pl.BlockSpec((None, tq, LANES), lambda b, p, i: (b, i, p)),
        out_shape=jax.ShapeDtypeStruct(q.shape, BF16),
        compiler_params=_cparams(("arbitrary", "arbitrary", "arbitrary")),
        name="stick_breaking",
    )(q, k, v)


def _dsa_kernel(qi_ref, wi_ref, qb_ref, ki_ref, kb_ref, vx_ref, bt_ref, bmax_ref, o_ref,
                qis_sc, wb_sc, qbs_sc, key_sc, smax_sc, mx_sc, acc_sc, kn_sc, thr_sc, *, tq, topk, diag_fn, adm_fn):
    tk = ATT_BLOCK
    group = DSA_HEADS // DSA_KV_HEADS

    @pl.when(pl.program_id(1) == 0)
    def _():
        for n in range(DSA_KV_HEADS):
            def body(pi, best):
                start = pl.multiple_of(pi * (2 * tk), 2 * tk)
                x = kb_ref[pl.ds(start, 2 * tk), n * LANES:(n + 1) * LANES].astype(F32)
                return jnp.maximum(best, jnp.sum(x * x, axis=1, keepdims=True))
            best = lax.fori_loop(0, kb_ref.shape[0] // (2 * tk), body, jnp.zeros((2 * tk, 1), F32))
            kn_sc[n] = jnp.broadcast_to(jnp.sqrt(0.5 * jnp.max(best, axis=0, keepdims=True)), (8, LANES))

    diag = diag_fn(pl.program_id(1))
    last = diag + (tq + tk - 1) // tk - 1
    lane = lax.broadcasted_iota(I32, (1, LANES), 1)
    low = lane < HEAD_DIM

    for h in range(IDX_HEADS):
        p = h // 2
        t = qi_ref[:, p * LANES:(p + 1) * LANES]
        z = jnp.zeros_like(t)
        qis_sc[h * tq:(h + 1) * tq, :] = jnp.where(low, t, z) if h % 2 == 0 else jnp.where(low, z, t)
        t = qb_ref[:, p * LANES:(p + 1) * LANES]
        g = h % group
        qbs_sc[h // group, g * tq:(g + 1) * tq, :] = jnp.where(low, t, z) if h % 2 == 0 else jnp.where(low, z, t)
        wb_sc[h] = jnp.broadcast_to(wi_ref[:, h:h + 1], (tq, LANES))

    r = lax.broadcasted_iota(I32, (tq, tk), 0)
    c = lax.broadcasted_iota(I32, (tq, tk), 1)
    last_pair = last // 2
    smax_sc[...] = jnp.full((tq, tk), -jnp.inf, F32)

    def score_pair(pi, masked):
        start = pl.multiple_of(pi * (2 * tk), 2 * tk)
        d = _dot_nt(qis_sc[...], ki_ref[pl.ds(start, 2 * tk), :])
        for half in range(2):
            s = jnp.zeros((tq, tk), F32)
            for h in range(IDX_HEADS):
                s = s + wb_sc[h] * jnp.maximum(d[h * tq:(h + 1) * tq, half * tk:(half + 1) * tk], 0.0)
            bits = pltpu.bitcast(s, I32)
            key = jnp.where(bits < 0, bits ^ INT32_MAX, bits)
            if masked:
                adm = adm_fn(pl.program_id(1), r, c + (start + half * tk))
                key = jnp.where(adm, key, NEG_INF_KEY)
                s = jnp.where(adm, s, -jnp.inf)
            key_sc[2 * pi + half] = key
            smax_sc[...] = jnp.maximum(smax_sc[...], s)

    def score_body(pi, carry):
        score_pair(pi, False)
        return carry

    lax.fori_loop(0, last_pair, score_body, 0)
    score_pair(last_pair, True)

    rc = min(tq, ATT_BLOCK)
    ones_mat = jnp.ones((tk, tk), BF16)

    def row_total(x):
        return _dot(x.astype(BF16), ones_mat)

    def count_ge(*thrs):
        for j, t in enumerate(thrs):
            if hasattr(t, "shape"):
                thr_sc[j] = t
        chunks = list(range(0, tq, rc))

        def body(pi, accs):
            out = []
            for ci, r0 in enumerate(chunks):
                k0 = key_sc[2 * pi, r0:r0 + rc, :]
                k1 = key_sc[2 * pi + 1, r0:r0 + rc, :]
                for j, t in enumerate(thrs):
                    tt = thr_sc[j, r0:r0 + rc, :] if hasattr(t, "shape") else t
                    out.append(accs[ci * len(thrs) + j]
                               + (jnp.where(k0 >= tt, 1.0, 0.0) + jnp.where(k1 >= tt, 1.0, 0.0)))
            return tuple(out)

        accs = lax.fori_loop(0, last_pair + 1, body,
                             tuple(jnp.zeros((rc, tk), F32) for _ in range(len(chunks) * len(thrs))))
        totals = []
        for j in range(len(thrs)):
            parts = [row_total(accs[ci * len(thrs) + j]) for ci in range(len(chunks))]
            totals.append(parts[0] if len(parts) == 1 else jnp.concatenate(parts, axis=0))
        return totals

    def midpoint(lo, hi):
        return (lo >> 1) + (hi >> 1) + (lo & hi & 1)

    def float_to_key(v):
        bits = pltpu.bitcast(v, I32)
        return jnp.where(bits < 0, bits ^ INT32_MAX, bits)

    kf = float(topk)

    def bis_cond(st):
        return st[0]

    def is_open(piv, lo, cnt_lo):
        return jnp.logical_and(piv != lo, cnt_lo != kf)

    def bis_body(st):
        _, lo, hi, cnt_lo = st
        for _ in range(2):
            piv = midpoint(lo, hi)
            open_ = is_open(piv, lo, cnt_lo)
            cnt, = count_ge(piv)
            ge = cnt >= kf
            up = jnp.logical_and(open_, ge)
            lo = jnp.where(up, piv, lo)
            cnt_lo = jnp.where(up, cnt, cnt_lo)
            hi = jnp.where(jnp.logical_and(open_, jnp.logical_not(ge)), piv, hi)
        go = jnp.max(jnp.where(is_open(midpoint(lo, hi), lo, cnt_lo), 1.0, 0.0)) > 0.0
        return go, lo, hi, cnt_lo

    c_zero, c_pos = count_ge(0, 1)
    pos = c_pos >= kf
    zer = jnp.logical_and(c_zero >= kf, jnp.logical_not(pos))
    hi_max = jnp.broadcast_to(float_to_key(jnp.max(smax_sc[...], axis=1, keepdims=True)) + 1, (tq, tk))
    lo0 = jnp.where(pos, 1, jnp.where(zer, 0, NEG_INF_KEY))
    hi0 = jnp.where(pos, hi_max, jnp.where(zer, 1, 0))
    cnt_lo0 = jnp.where(pos, c_pos, jnp.where(zer, c_zero, -1.0))
    go0 = jnp.max(jnp.where(is_open(midpoint(lo0, hi0), lo0, cnt_lo0), 1.0, 0.0)) > 0.0
    _, thr, _, cnt_lo = lax.while_loop(bis_cond, bis_body, (go0, lo0, hi0, cnt_lo0))

    tied = cnt_lo > kf

    @pl.when(jnp.max(jnp.where(tied, 1.0, 0.0)) > 0.0)
    def _():
        above, = count_ge(thr + 1)
        need = kf - above
        prefix_mat = jnp.where(lax.broadcasted_iota(I32, (tk, tk), 0) <= lax.broadcasted_iota(I32, (tk, tk), 1),
                               1.0, 0.0).astype(BF16)

        def demote(kb, before):
            key = key_sc[kb]
            eq = jnp.logical_and(tied, key == thr)
            ind = jnp.where(eq, 1.0, 0.0)
            rank = _dot(ind.astype(BF16), prefix_mat) + before
            key_sc[kb] = jnp.where(jnp.logical_and(eq, rank > need), NEG_INF_KEY, key)
            return before + row_total(ind)

        lax.fori_loop(0, last + 1, demote, jnp.zeros((tq, tk), F32))

    thr_eff = jnp.maximum(thr, NEG_INF_KEY + 1)

    def masked_logits(pi, n):
        start = pl.multiple_of(pi * (2 * tk), 2 * tk)
        lg = _dot_nt(qbs_sc[n], kb_ref[pl.ds(start, 2 * tk), n * LANES:(n + 1) * LANES])
        sel = [key_sc[2 * pi + half] >= thr_eff for half in range(2)]
        j = [jnp.clip(diag - (2 * pi + half) + 1, 0, 3) for half in range(2)]
        out = []
        for g in range(group):
            h = n * group + g
            out.append([jnp.where(sel[half],
                                  lg[g * tq:(g + 1) * tq, half * tk:(half + 1) * tk] + bt_ref[h, j[half], 0:tq, :],
                                  MASKED_LOGIT) for half in range(2)])
        return out

    def max_body(pi, carry):
        for n in range(DSA_KV_HEADS):
            for g, (la, lb) in enumerate(masked_logits(pi, n)):
                h = n * group + g
                mx_sc[h] = jnp.maximum(mx_sc[h], jnp.maximum(la, lb))
        return carry

    def exact_row_maxima():
        for h in range(DSA_HEADS):
            mx_sc[h] = jnp.full((tq, LANES), MASKED_LOGIT, F32)
        lax.fori_loop(0, last_pair + 1, max_body, 0)
        for h in range(DSA_HEADS):
            mx_sc[h] = jnp.broadcast_to(jnp.max(mx_sc[h], axis=1, keepdims=True), (tq, LANES))

    def bounded_row_maxima():
        for h in range(DSA_HEADS):
            n, g = h // group, h % group
            q = qbs_sc[n, g * tq:(g + 1) * tq, :].astype(F32)
            q_norm = jnp.sqrt(jnp.sum(q * q, axis=1, keepdims=True))
            mx_sc[h] = q_norm * kn_sc[n, 0:1, :] + bmax_ref[h, 0:1, :]

    def pv_body(pi, carry):
        start = pl.multiple_of(pi * (2 * tk), 2 * tk)
        for n in range(DSA_KV_HEADS):
            ps = []
            for g, (la, lb) in enumerate(masked_logits(pi, n)):
                m = mx_sc[n * group + g]
                ps.append(jnp.concatenate([jnp.exp(la - m).astype(BF16), jnp.exp(lb - m).astype(BF16)], axis=1))
            pv = _dot(jnp.concatenate(ps, axis=0), vx_ref[pl.ds(start, 2 * tk), n * LANES:(n + 1) * LANES])
            for g in range(group):
                h = n * group + g
                acc_sc[h] = acc_sc[h] + pv[g * tq:(g + 1) * tq]
        return carry

    def weights_sweep():
        for h in range(DSA_HEADS):
            acc_sc[h] = jnp.zeros((tq, LANES), F32)
        lax.fori_loop(0, last_pair + 1, pv_body, 0)

    bounded_row_maxima()
    weights_sweep()
    den = acc_sc[0][:, HEAD_DIM:HEAD_DIM + 1]
    for h in range(1, DSA_HEADS):
        den = jnp.minimum(den, acc_sc[h][:, HEAD_DIM:HEAD_DIM + 1])
    healthy = jnp.min(jnp.where(den >= SOFTMAX_DEN_FLOOR, 1.0, 0.0)) > 0.0

    @pl.when(jnp.logical_not(healthy))
    def _():
        exact_row_maxima()
        weights_sweep()

    def normalised(h):
        a = acc_sc[h]
        return a / pltpu.roll(a, HEAD_DIM, axis=1)

    for p in range(DSA_HEADS // 2):
        o1 = pltpu.roll(normalised(2 * p + 1), HEAD_DIM, axis=1)
        o_ref[:, p * LANES:(p + 1) * LANES] = jnp.where(low, normalised(2 * p), o1).astype(o_ref.dtype)


def _dsa_call(qi, wi, qb, kid, kbd, vx, btiles, bmax, tq, topk, diag_fn, adm_fn):
    s, t, _ = qi.shape
    tkk = kid.shape[1]
    assert tkk % (2 * ATT_BLOCK) == 0, "keys must come in whole pairs of blocks"
    nkb = tkk // ATT_BLOCK
    rowq = lambda w: pl.BlockSpec((None, tq, w), lambda b, i: (b, i, 0))
    full = lambda w: pl.BlockSpec((None, tkk, w), lambda b, i: (b, 0, 0))
    return pl.pallas_call(
        functools.partial(_dsa_kernel, tq=tq, topk=topk, diag_fn=diag_fn, adm_fn=adm_fn),
        grid=(s, t // tq),
        in_specs=[rowq(512), rowq(LANES), rowq(512), full(LANES), full(2 * LANES), full(2 * LANES),
                  pl.BlockSpec(btiles.shape, lambda b, i: (0, 0, 0, 0)),
                  pl.BlockSpec(bmax.shape, lambda b, i: (0, 0, 0))],
        out_specs=rowq(512),
        out_shape=jax.ShapeDtypeStruct((s, t, 512), BF16),
        scratch_shapes=[pltpu.VMEM((IDX_HEADS * tq, LANES), BF16),
                        pltpu.VMEM((IDX_HEADS, tq, LANES), F32),
                        pltpu.VMEM((DSA_KV_HEADS, DSA_HEADS // DSA_KV_HEADS * tq, LANES), BF16),
                        pltpu.VMEM((nkb, tq, ATT_BLOCK), I32),
                        pltpu.VMEM((tq, ATT_BLOCK), F32),
                        pltpu.VMEM((DSA_HEADS, tq, LANES), F32),
                        pltpu.VMEM((DSA_HEADS, tq, LANES), F32),
                        pltpu.VMEM((DSA_KV_HEADS, 8, LANES), F32),
                        pltpu.VMEM((2, tq, ATT_BLOCK), I32)],
        compiler_params=_cparams(("arbitrary", "arbitrary")),
        name="dsa_attention",
    )(qi, wi, qb, kid, kbd, vx, btiles, bmax)


def _layer_norm(x, g, b):
    mu = jnp.mean(x, axis=-1, keepdims=True)
    xc = x - mu
    var = jnp.mean(xc * xc, axis=-1, keepdims=True)
    return xc * lax.rsqrt(var + LN_EPS) * g + b


def _post_kernel(oa_ref, ob_ref, sg_ref, x_ref, mod_ref, wsb_ref, wdsa_ref, wout_ref, g_ref, b_ref,
                 wrh_ref, wrl_ref, br_ref, x1_ref, h2_ref, ti_ref, tg_ref):
    d = D_MODEL
    ya = _dot(oa_ref[...], wsb_ref[...])
    yb = _dot(ob_ref[...], wdsa_ref[...])
    merged = sg_ref[:, :d] * ya + sg_ref[:, d:] * yb
    mix = _dot(merged.astype(BF16), wout_ref[...])
    g1 = mod_ref[2:3, :]
    x1 = _layer_norm(DEEPNORM_ALPHA * x_ref[...] + g1 * mix, g_ref[...], b_ref[...])
    x1_ref[...] = x1
    h2 = x1 * (1.0 + mod_ref[4:5, :]) + mod_ref[3:4, :]
    h2_ref[...] = h2
    logits = _dot3(h2, wrh_ref[...], wrl_ref[...]) + br_ref[...]
    lane = lax.broadcasted_iota(I32, logits.shape, 1).astype(F32)
    neg = -jnp.inf
    cur = jnp.where(lane < N_EXPERTS, logits, neg)
    vals, idxs = [], []
    for _ in range(TOP_K_EXPERTS):
        m = jnp.max(cur, axis=1, keepdims=True)
        idx = jnp.min(jnp.where(cur == m, lane, float(LANES)), axis=1, keepdims=True)
        vals.append(m)
        idxs.append(idx)
        cur = jnp.where(lane == idx, neg, cur)
    es = [jnp.exp(v - vals[0]) for v in vals]
    tot = es[0] + es[1] + es[2] + es[3]
    ti = jnp.zeros(logits.shape, F32)
    tg = jnp.zeros(logits.shape, F32)
    for k in range(TOP_K_EXPERTS):
        ti = jnp.where(lane == k, idxs[k], ti)
        tg = jnp.where(lane == k, es[k] / tot, tg)
    ti_ref[...] = ti.astype(I32)
    tg_ref[...] = tg


def _post_call(oa, ob, sg, x, mod, wsb, wdsa, wout, ln_g, ln_b, wr_hi, wr_lo, br, tm):
    s, t, d = x.shape
    row = lambda w: pl.BlockSpec((None, tm, w), lambda b, i: (b, i, 0))
    const = lambda a: pl.BlockSpec(a.shape, lambda b, i: (0,) * a.ndim)
    shp = lambda w, dt: jax.ShapeDtypeStruct((s, t, w), dt)
    return pl.pallas_call(
        _post_kernel,
        grid=(s, t // tm),
        in_specs=[row(512), row(512), row(2 * d), row(d),
                  pl.BlockSpec((None, 6, d), lambda b, i: (b, 0, 0)),
                  const(wsb), const(wdsa), const(wout), const(ln_g), const(ln_b),
                  const(wr_hi), const(wr_lo), const(br)],
        out_specs=[row(d), row(d), row(LANES), row(LANES)],
        out_shape=[shp(d, F32), shp(d, F32), shp(LANES, I32), shp(LANES, F32)],
        compiler_params=_cparams(("arbitrary", "arbitrary")),
        name="post_attention_router",
    )(oa, ob, sg, x, mod, wsb, wdsa, wout, ln_g, ln_b, wr_hi, wr_lo, br)


def _dispatch_kernel(zoff_ref, nused_ref, dest_ref, h_ref, xs_hbm, buf, zbuf, sem, zsem, *, tm, tile, n_tiles):
    i = pl.program_id(0)
    nb = pl.num_programs(0)
    k = TOP_K_EXPERTS
    slot = i % 2

    def row_copy(dst, slot, r):
        return pltpu.make_async_copy(buf.at[slot, pl.ds(r, 1), :], xs_hbm.at[pl.ds(dst, 1), :], sem.at[slot])

    def wait_all(slot):
        def body(r, carry):
            for _ in range(k):
                row_copy(0, slot, r).wait()
            return carry
        lax.fori_loop(0, tm, body, 0, unroll=4)

    def zero_fill(start):
        return pltpu.make_async_copy(zbuf, xs_hbm.at[pl.ds(start, tile), :], zsem)

    @pl.when(i == 0)
    def _():
        zbuf[...] = jnp.zeros_like(zbuf)
        fills = [(zoff_ref[e] >= 0, pl.multiple_of(jnp.maximum(zoff_ref[e], 0), MOE_BLOCK)) for e in range(N_EXPERTS)]
        fills += [(t >= nused_ref[0], t * tile) for t in range(max(n_tiles - N_EXPERTS - 1, 0), n_tiles)]
        for on, start in fills:
            @pl.when(on)
            def _():
                zero_fill(start).start()
        for on, start in fills:
            @pl.when(on)
            def _():
                zero_fill(start).wait()

    @pl.when(i >= 2)
    def _():
        wait_all(slot)

    buf[slot] = h_ref[...]
    for r in range(tm):
        for j in range(k):
            row_copy(dest_ref[0, r * k + j], slot, r).start()

    @pl.when(i == nb - 1)
    def _():
        wait_all(slot)

        @pl.when(nb >= 2)
        def _():
            wait_all(1 - slot)


def _dispatch_call(zoff, nused, dest, h2, n_rows, tile):
    n, d = h2.shape
    tm = min(MOE_BLOCK, n)
    nb = n // tm
    dest3 = dest.reshape(nb, 1, tm * TOP_K_EXPERTS)
    grid_spec = pltpu.PrefetchScalarGridSpec(
        num_scalar_prefetch=2,
        grid=(nb,),
        in_specs=[pl.BlockSpec((None, 1, tm * TOP_K_EXPERTS), lambda i, zo, nu: (i, 0, 0), memory_space=pltpu.SMEM),
                  pl.BlockSpec((tm, d), lambda i, zo, nu: (i, 0))],
        out_specs=pl.BlockSpec(memory_space=pl.ANY),
        scratch_shapes=[pltpu.VMEM((2, tm, d), F32), pltpu.VMEM((tile, d), F32),
                        pltpu.SemaphoreType.DMA((2,)), pltpu.SemaphoreType.DMA(())],
    )
    return pl.pallas_call(
        functools.partial(_dispatch_kernel, tm=tm, tile=tile, n_tiles=n_rows // tile),
        grid_spec=grid_spec,
        out_shape=jax.ShapeDtypeStruct((n_rows, d), F32),
        compiler_params=_cparams(("arbitrary",)),
        name="moe_dispatch",
    )(zoff, nused, dest3, h2)


def _ffn_kernel(be_ref, x_ref, wup_ref, bup_ref, wdn_ref, bdn_ref, o_ref):
    u = _dot(x_ref[...].astype(BF16), wup_ref[...]) + bup_ref[...]
    acts = []
    for t in range(2 * D_FF // SWIGLU_TILE):
        a = t * SWIGLU_TILE
        glu = jnp.minimum(u[:, a:a + LANES], SWIGLU_LIMIT)
        lin = jnp.clip(u[:, a + LANES:a + SWIGLU_TILE], -SWIGLU_LIMIT, SWIGLU_LIMIT)
        acts.append((glu * jax.nn.sigmoid(SWIGLU_ALPHA * glu) * (lin + 1.0)).astype(BF16))
    o_ref[...] = _dot(jnp.concatenate(acts, axis=1), wdn_ref[...]) + bdn_ref[...]


def _deinterleave_kernel(w_ref, o_ref):
    j = lax.broadcasted_iota(I32, (SWIGLU_TILE, SWIGLU_TILE), 0)
    s = lax.broadcasted_iota(I32, (SWIGLU_TILE, SWIGLU_TILE), 1)
    src = jnp.where(s < LANES, 2 * s, 2 * (s - LANES) + 1)
    perm = jnp.where(j == src, 1.0, 0.0).astype(BF16)
    for t in range(w_ref.shape[1] // SWIGLU_TILE):
        a = t * SWIGLU_TILE
        o_ref[:, a:a + SWIGLU_TILE] = _dot(w_ref[:, a:a + SWIGLU_TILE].astype(BF16), perm).astype(BF16)


def _deinterleave_call(w_up):
    e, d, f = w_up.shape
    tr = 512
    return pl.pallas_call(
        _deinterleave_kernel,
        grid=(e, d // tr),
        in_specs=[pl.BlockSpec((None, tr, f), lambda a, b: (a, b, 0))],
        out_specs=pl.BlockSpec((None, tr, f), lambda a, b: (a, b, 0)),
        out_shape=jax.ShapeDtypeStruct((e, d, f), BF16),
        compiler_params=_cparams(("arbitrary", "arbitrary")),
        name="w_up_tiles",
    )(w_up)


def _ffn_call(block_expert, xs, wup, bup, wdn, bdn, tile):
    n_rows, d = xs.shape
    n_blocks = n_rows // tile
    grid_spec = pltpu.PrefetchScalarGridSpec(
        num_scalar_prefetch=1,
        grid=(n_blocks,),
        in_specs=[pl.BlockSpec((tile, d), lambda i, be: (i, 0)),
                  pl.BlockSpec((None, d, 2 * D_FF), lambda i, be: (be[i], 0, 0)),
                  pl.BlockSpec((None, 1, 2 * D_FF), lambda i, be: (be[i], 0, 0)),
                  pl.BlockSpec((None, D_FF, d), lambda i, be: (be[i], 0, 0)),
                  pl.BlockSpec((None, 1, d), lambda i, be: (be[i], 0, 0))],
        out_specs=pl.BlockSpec((tile, d), lambda i, be: (i, 0)),
    )
    return pl.pallas_call(
        _ffn_kernel,
        grid_spec=grid_spec,
        out_shape=jax.ShapeDtypeStruct((n_rows, d), F32),
        compiler_params=_cparams(("arbitrary",)),
        name="expert_ffn",
    )(block_expert, xs, wup, bup, wdn, bdn)


def _combine_kernel(pos0_ref, posn_ref, rows_hbm, x1_ref, tg_ref, mod_ref, g_ref, b_ref, o_ref, buf, sem, *, tm):
    i = pl.program_id(0)
    nb = pl.num_programs(0)
    k = TOP_K_EXPERTS

    def row_copy(src, slot, r, j):
        return pltpu.make_async_copy(rows_hbm.at[pl.ds(src, 1), :], buf.at[slot, j, pl.ds(r, 1), :], sem.at[slot])

    def issue(pos_ref, slot):
        def body(r, carry):
            for j in range(k):
                row_copy(pos_ref[0, r * k + j], slot, r, j).start()
            return carry
        lax.fori_loop(0, tm, body, 0, unroll=4)

    slot = i % 2

    @pl.when(i == 0)
    def _():
        issue(pos0_ref, 0)

    @pl.when(i + 1 < nb)
    def _():
        issue(posn_ref, 1 - slot)

    def wait_body(r, carry):
        for j in range(k):
            row_copy(0, slot, r, j).wait()
        return carry
    lax.fori_loop(0, tm, wait_body, 0, unroll=4)

    gated = [buf[slot, j] * tg_ref[:, j:j + 1] for j in range(k)]
    y = (gated[0] + gated[1]) + (gated[2] + gated[3])
    o_ref[...] = _layer_norm(DEEPNORM_ALPHA * x1_ref[...] + mod_ref[5:6, :] * y, g_ref[...], b_ref[...])


def _combine_call(pos, rows, x1, tg, mod, seq_len, ln_g, ln_b, tm):
    n, d = x1.shape
    nb = n // tm
    per_seq = seq_len // tm
    pos3 = pos.reshape(nb, 1, tm * TOP_K_EXPERTS)
    return pl.pallas_call(
        functools.partial(_combine_kernel, tm=tm),
        grid=(nb,),
        in_specs=[pl.BlockSpec((None, 1, tm * TOP_K_EXPERTS), lambda i: (0, 0, 0), memory_space=pltpu.SMEM),
                  pl.BlockSpec((None, 1, tm * TOP_K_EXPERTS), lambda i: (jnp.minimum(i + 1, nb - 1), 0, 0),
                               memory_space=pltpu.SMEM),
                  pl.BlockSpec(memory_space=pl.ANY),
                  pl.BlockSpec((tm, d), lambda i: (i, 0)),
                  pl.BlockSpec((tm, LANES), lambda i: (i, 0)),
                  pl.BlockSpec((None, 6, d), lambda i: (i // per_seq, 0, 0)),
                  pl.BlockSpec((1, d), lambda i: (0, 0)),
                  pl.BlockSpec((1, d), lambda i: (0, 0))],
        out_specs=pl.BlockSpec((tm, d), lambda i: (i, 0)),
        out_shape=jax.ShapeDtypeStruct((n, d), F32),
        scratch_shapes=[pltpu.VMEM((2, TOP_K_EXPERTS, tm, d), F32), pltpu.SemaphoreType.DMA((2,))],
        compiler_params=_cparams(("arbitrary",)),
        name="moe_combine_ln2",
    )(pos3, pos3, rows, x1, tg, mod, ln_g, ln_b)


def _rank_kernel(ti_ref, rank_ref, cnt_ref, carry_sc):
    tb = ti_ref.shape[0]

    @pl.when(pl.program_id(0) == 0)
    def _():
        carry_sc[...] = jnp.zeros_like(carry_sc)

    lane = lax.broadcasted_iota(I32, (tb, LANES), 1)
    ti = ti_ref[...]
    hits = [ti[:, k:k + 1] == lane for k in range(TOP_K_EXPERTS)]
    tot = jnp.zeros((tb, LANES), F32)
    for hit in hits:
        tot = tot + jnp.where(hit, 1.0, 0.0)
    earlier = jnp.where(lax.broadcasted_iota(I32, (tb, tb), 1) < lax.broadcasted_iota(I32, (tb, tb), 0),
                        1.0, 0.0).astype(BF16)
    before = _dot(earlier, tot.astype(BF16)) + carry_sc[0:1, :]
    out = jnp.zeros((tb, LANES), F32)
    for k, hit in enumerate(hits):
        out = jnp.where(lane == k, jnp.sum(jnp.where(hit, before, 0.0), axis=1, keepdims=True), out)
    rank_ref[...] = out.astype(I32)
    carry_sc[...] = carry_sc[...] + jnp.sum(tot, axis=0, keepdims=True)
    cnt_ref[...] = carry_sc[...]


def _rank_call(ti):
    n = ti.shape[0]
    tb = min(512, n)
    return pl.pallas_call(
        _rank_kernel,
        grid=(n // tb,),
        in_specs=[pl.BlockSpec((tb, LANES), lambda i: (i, 0))],
        out_specs=[pl.BlockSpec((tb, LANES), lambda i: (i, 0)), pl.BlockSpec((8, LANES), lambda i: (0, 0))],
        out_shape=[jax.ShapeDtypeStruct((n, LANES), I32), jax.ShapeDtypeStruct((8, LANES), F32)],
        scratch_shapes=[pltpu.VMEM((8, LANES), F32)],
        compiler_params=_cparams(("arbitrary",)),
        name="moe_rank",
    )(ti)


def _routing(ti, tile):
    n_tok = ti.shape[0]
    n_assign = n_tok * TOP_K_EXPERTS
    e_flat = ti[:, :TOP_K_EXPERTS].reshape(-1)
    rank_tile, cnt = _rank_call(ti)
    rank = rank_tile[:, :TOP_K_EXPERTS].reshape(-1)
    counts = cnt[0, :N_EXPERTS].astype(I32)
    padded = (counts + tile - 1) // tile * tile
    pend = jnp.cumsum(padded)
    pstart = pend - padded
    dest = (pstart[e_flat] + rank).astype(I32)
    n_rows = (n_assign + N_EXPERTS * (tile - 1) + tile - 1) // tile * tile
    n_blocks = n_rows // tile
    block_expert = jnp.minimum(
        jnp.searchsorted(pend, jnp.arange(n_blocks, dtype=I32) * tile, side="right"), N_EXPERTS - 1).astype(I32)
    last_tile = jnp.where(padded > 0, pend - tile, -1).astype(I32)
    nused = (pend[-1] // tile).astype(I32).reshape(1)
    return dest, block_expert, last_tile, nused, n_rows


def _stream(x, mod, kv_cache, weights, btiles, tm, tq):
    (w_packed, b_packed, wsb, wdsa, wout, ln1_g, ln1_b, wr_hi, wr_lo, br,
     wup, bup, wdn, bdn, ln2_g, ln2_b) = weights
    s, t, d = x.shape
    (qa, ka32, va32, ka16, va16, qb, kb32, vb32, kbd, vx, qi, ki32, kid, wi, sg) = _proj_call(x, mod, w_packed, b_packed, tm)

    if kv_cache is None:
        k_sb, v_sb, k_id, k_bd, v_x = ka16, va16, kid, kbd, vx
        total = t
        assert tq % ATT_BLOCK == 0
        diag_fn = lambda i: i * (tq // ATT_BLOCK)
        adm_fn = lambda i, r, key_pos: key_pos < i * tq + (r // CHUNK + 1) * CHUNK
    else:
        past = kv_cache[0].shape[1]
        total = past + t
        pad = (-total) % (2 * ATT_BLOCK)
        cat = lambda cache, new: jnp.concatenate(
            [cache, new, jnp.zeros((s, pad, new.shape[2]), new.dtype)], axis=1)
        k_sb, v_sb, k_id, k_bd, v_x = [cat(c_, n_) for c_, n_ in zip(kv_cache, (ka16, va16, kid, kbd, vx))]
        assert past % ATT_BLOCK == 0 and t <= ATT_BLOCK
        diag_fn = lambda i: i * 0 + past // ATT_BLOCK
        adm_fn = lambda i, r, key_pos: key_pos < total
    topk = max(1, min(TOPK_MAX, total // 4))

    if kv_cache is None:
        oa = _sb_call(qa, k_sb, v_sb, min(SB_BLOCK, t), lambda i: i)
    else:
        assert kv_cache[0].shape[1] % SB_BLOCK == 0 and t <= SB_BLOCK
        oa = _sb_call(qa, k_sb, v_sb, t, lambda i: i * 0 + kv_cache[0].shape[1] // SB_BLOCK)
    ob = _dsa_call(qi, wi, qb, k_id, k_bd, v_x, *btiles, tq, topk, diag_fn, adm_fn)
    x1, h2, ti, tg = _post_call(oa, ob, sg, x, mod, wsb, wdsa, wout, ln1_g, ln1_b, wr_hi, wr_lo, br, tm)

    n = s * t
    tile = 4 * MOE_BLOCK if n * TOP_K_EXPERTS >= N_EXPERTS * 8 * MOE_BLOCK else MOE_BLOCK
    dest, block_expert, last_tile, nused, n_rows = _routing(ti.reshape(n, LANES), tile)
    xs = _dispatch_call(last_tile, nused, dest, h2.reshape(n, d), n_rows, tile)
    rows = _ffn_call(block_expert, xs, wup, bup, wdn, bdn, tile)
    y = _combine_call(dest, rows, x1.reshape(n, d), tg.reshape(n, LANES), mod, t, ln2_g, ln2_b, min(128, t))
    new_rows = (ka32.reshape(1, s, t, SB_HEADS, HEAD_DIM), va32.reshape(1, s, t, SB_HEADS, HEAD_DIM),
                kb32.reshape(1, s, t, DSA_KV_HEADS, HEAD_DIM), vb32.reshape(1, s, t, DSA_KV_HEADS, HEAD_DIM),
                ki32.reshape(1, s, t, IDX_DIM))
    return y.reshape(s, t, d), new_rows


def kernel(x_prompt, x_sample, cache_sb_k, cache_sb_v, cache_dsa_k, cache_dsa_v, cache_idx_k, c_prompt, c_sample, rel_bias, w_ada, b_ada, w_in, b_in, w_o_sb, w_o_dsa, w_out, ln1_g, ln1_b, w_router, b_router, w_up, b_up, w_down, b_down, ln2_g, ln2_b):
    d = D_MODEL
    nb, ns = x_prompt.shape[0], x_sample.shape[0]
    past = cache_sb_k.shape[2]

    mod = _mod_call(jnp.concatenate([c_prompt, c_sample], axis=0), w_ada[0], b_ada[0]).reshape(nb + ns, 6, d)
    btiles = _bias_call(rel_bias)

    w_packed, b_packed = _pack_w_in(w_in[0], b_in[0])
    wr = jnp.concatenate([w_router[0], jnp.zeros((d, LANES - N_EXPERTS), F32)], axis=1)
    wr_hi = wr.astype(BF16)
    wr_lo = (wr - wr_hi.astype(F32)).astype(BF16)
    br = jnp.concatenate([b_router[0], jnp.zeros((LANES - N_EXPERTS,), F32)]).reshape(1, LANES)
    wup = _deinterleave_call(w_up[0])
    bup = b_up[0].reshape(N_EXPERTS, 2 * D_FF // SWIGLU_TILE, LANES, 2).swapaxes(2, 3).reshape(N_EXPERTS, 1, 2 * D_FF)
    weights = (w_packed, b_packed, w_o_sb[0].astype(BF16), w_o_dsa[0].astype(BF16), w_out[0].astype(BF16),
               ln1_g[0].reshape(1, d), ln1_b[0].reshape(1, d), wr_hi, wr_lo, br,
               wup, bup, w_down[0].astype(BF16), b_down[0].reshape(N_EXPERTS, 1, d),
               ln2_g[0].reshape(1, d), ln2_b[0].reshape(1, d))

    dup = lambda a: jnp.concatenate([a[..., :64], a[..., :64], a[..., 64:], a[..., 64:]], axis=-1)
    idx_c = cache_idx_k[0]
    caches = (cache_sb_k[0].reshape(ns, past, SB_HEADS * HEAD_DIM).astype(BF16),
              cache_sb_v[0].reshape(ns, past, SB_HEADS * HEAD_DIM).astype(BF16),
              jnp.concatenate([idx_c, idx_c], axis=-1).astype(BF16),
              dup(cache_dsa_k[0].reshape(ns, past, DSA_KV_HEADS * HEAD_DIM)).astype(BF16),
              _with_ones(cache_dsa_v[0].reshape(ns, past, DSA_KV_HEADS * HEAD_DIM)).astype(BF16))

    t_p, t_s = x_prompt.shape[1], x_sample.shape[1]
    y_p, new_p = _stream(x_prompt, mod[:nb], None, weights, btiles, min(256, t_p), min(DSA_Q_BLOCK, t_p))
    y_s, new_s = _stream(x_sample, mod[nb:], caches, weights, btiles, t_s, t_s)
    return (y_p, y_s) + new_p + new_s
```

```python
import functools

import jax
import jax.numpy as jnp
import numpy as np
from jax import lax
from jax.experimental import pallas as pl
from jax.experimental.pallas import tpu as pltpu

F32 = jnp.float32
BF16 = jnp.bfloat16
I32 = jnp.int32

D_MODEL = 1024
CHUNK = 64
SB_HEADS = 8
HEAD_DIM = 64
DSA_HEADS = 8
DSA_KV_HEADS = 2
IDX_HEADS = 8
IDX_DIM = 64
TOPK_MAX = 256
N_BUCKETS = 32
N_EXPERTS = 32
TOP_K_EXPERTS = 4
D_FF = 1024
SWIGLU_LIMIT = 7.0
SWIGLU_ALPHA = 1.702
MOE_BLOCK = 128
LN_EPS = 1e-5
DEPTH = 1
DEEPNORM_ALPHA = (2.0 * DEPTH) ** 0.25

LANES = 128
ATT_BLOCK = 128
DSA_Q_BLOCK = 256
SB_BLOCK = 256
VMEM_LIMIT = 56 * 1024 * 1024

EXP_ZERO_BELOW = -104.0
NEG_INF_KEY = -2139095041
INT32_MAX = 2147483647
MASKED_LOGIT = -1e30
POST_SUB_ROWS = 256
SOFTMAX_DEN_FLOOR = 1e-30
SWIGLU_TILE = 2 * LANES

_SEG = {}
_off = 0
for _name, _w in (("qa", 512), ("ka", 512), ("va", 512), ("qb", 512), ("kb", 128), ("vb", 128),
                  ("kbd", 256), ("vx", 256), ("qi", 512), ("kid", 128), ("wi", 128),
                  ("ga", 1024), ("gb", 1024)):
    _SEG[_name] = (_off, _off + _w)
    _off += _w
PACKED_COLS = _off


def _cparams(sem):
    return pltpu.CompilerParams(dimension_semantics=sem, vmem_limit_bytes=VMEM_LIMIT)


def _dot(a, b):
    return jnp.dot(a, b, preferred_element_type=F32)


def _dot_nt(a, b):
    return lax.dot_general(a, b, (((1,), (1,)), ((), ())), preferred_element_type=F32)


def _split_bf16(x):
    hi = x.astype(BF16)
    lo = (x - hi.astype(F32)).astype(BF16)
    return hi, lo


def _dot3(a, b_hi, b_lo):
    a_hi, a_lo = _split_bf16(a)
    return _dot(a_hi, b_hi) + (_dot(a_hi, b_lo) + _dot(a_lo, b_hi))


def _mod_kernel(c_ref, w_ref, b_ref, o_ref):
    c = c_ref[...]
    s = c * jax.nn.sigmoid(c)
    w_hi, w_lo = _split_bf16(w_ref[...])
    o_ref[...] = _dot3(s, w_hi, w_lo) + b_ref[...]


def _mod_call(c_all, w_ada, b_ada):
    n, d = c_all.shape
    cols = w_ada.shape[1]
    tn = 1024
    return pl.pallas_call(
        _mod_kernel,
        grid=(cols // tn,),
        in_specs=[pl.BlockSpec((n, d), lambda j: (0, 0)),
                  pl.BlockSpec((d, tn), lambda j: (0, j)),
                  pl.BlockSpec((1, tn), lambda j: (0, j))],
        out_specs=pl.BlockSpec((n, tn), lambda j: (0, j)),
        out_shape=jax.ShapeDtypeStruct((n, cols), F32),
        compiler_params=_cparams(("arbitrary",)),
        name="adaln_mod",
    )(c_all, w_ada, b_ada.reshape(1, cols))


def _proj_kernel(x_ref, mod_ref, w_ref, b_ref,
                 qa_ref, ka32_ref, va32_ref, ka16_ref, va16_ref,
                 qb_ref, kb32_ref, vb32_ref, kbd_ref, vx_ref,
                 qi_ref, ki32_ref, kid_ref, wi_ref, sg_ref):
    sh1 = mod_ref[0:1, :]
    sc1 = mod_ref[1:2, :]
    h = (x_ref[...] * (1.0 + sc1) + sh1).astype(BF16)

    def seg(name):
        a, b = _SEG[name]
        return _dot(h, w_ref[:, a:b]) + b_ref[:, a:b]

    qa_ref[...] = (seg("qa") * HEAD_DIM ** -0.5).astype(BF16)
    ka = seg("ka")
    ka32_ref[...] = ka
    ka16_ref[...] = ka.astype(BF16)
    va = seg("va")
    va32_ref[...] = va
    va16_ref[...] = va.astype(BF16)
    qb_ref[...] = (seg("qb") * HEAD_DIM ** -0.5).astype(BF16)
    kb32_ref[...] = seg("kb")
    vb32_ref[...] = seg("vb")
    kbd_ref[...] = seg("kbd").astype(BF16)
    vx_ref[...] = seg("vx").astype(BF16)
    qi_ref[...] = (seg("qi") * IDX_DIM ** -0.5).astype(BF16)
    kid = seg("kid")
    ki32_ref[...] = kid[:, :IDX_DIM]
    kid_ref[...] = kid.astype(BF16)
    wi_ref[...] = seg("wi") * IDX_HEADS ** -0.5
    a, _ = _SEG["ga"]
    _, b = _SEG["gb"]
    sg_ref[...] = jax.nn.sigmoid(_dot(h, w_ref[:, a:b]) + b_ref[:, a:b])


def _pack_w_in(w_in, b_in):
    offs = np.cumsum((0, 512, 512, 512, 512, 128, 128, 512, 8, 64, 1024, 1024))
    qa, ka, va, qb, kb, vb, qi, wi, ki, ga, gb = [slice(int(offs[i]), int(offs[i + 1])) for i in range(11)]

    def pack(m, fill):
        kb_m, vb_m = m[..., kb], m[..., vb]
        dup = lambda t: jnp.concatenate([t[..., :64], t[..., :64], t[..., 64:], t[..., 64:]], axis=-1)
        wi_m = jnp.concatenate([m[..., wi], jnp.zeros(m.shape[:-1] + (LANES - IDX_HEADS,), m.dtype)], axis=-1)
        return jnp.concatenate([m[..., qa], m[..., ka], m[..., va], m[..., qb], kb_m, vb_m, dup(kb_m),
                                _with_ones(vb_m, fill),
                                m[..., qi], m[..., ki], m[..., ki], wi_m, m[..., ga], m[..., gb]], axis=-1)

    return pack(w_in, 0.0).astype(BF16), pack(b_in.reshape(1, -1), 1.0)


def _with_ones(v, fill=1.0):
    f = jnp.full(v.shape[:-1] + (HEAD_DIM,), fill, v.dtype)
    return jnp.concatenate([v[..., :HEAD_DIM], f, v[..., HEAD_DIM:], f], axis=-1)


def _proj_call(x, mod, w_packed, b_packed, tm):
    s, t, d = x.shape
    nt = t // tm
    row = lambda w: pl.BlockSpec((None, tm, w), lambda b, i: (b, i, 0))
    shp = lambda w, dt: jax.ShapeDtypeStruct((s, t, w), dt)
    outs = [(512, BF16), (512, F32), (512, F32), (512, BF16), (512, BF16),
            (512, BF16), (128, F32), (128, F32), (256, BF16), (256, BF16),
            (512, BF16), (IDX_DIM, F32), (128, BF16), (128, F32), (2048, F32)]
    return pl.pallas_call(
        _proj_kernel,
        grid=(s, nt),
        in_specs=[row(d),
                  pl.BlockSpec((None, 6, d), lambda b, i: (b, 0, 0)),
                  pl.BlockSpec((d, PACKED_COLS), lambda b, i: (0, 0)),
                  pl.BlockSpec((1, PACKED_COLS), lambda b, i: (0, 0))],
        out_specs=[row(w) for w, _ in outs],
        out_shape=[shp(w, dt) for w, dt in outs],
        compiler_params=_cparams(("arbitrary", "arbitrary")),
        name="in_proj",
    )(x, mod, w_packed, b_packed)


def _bias_kernel(tab_ref, o_ref, max_ref):
    r = lax.broadcasted_iota(I32, (DSA_Q_BLOCK, ATT_BLOCK), 0)
    c = lax.broadcasted_iota(I32, (DSA_Q_BLOCK, ATT_BLOCK), 1)
    half = N_BUCKETS // 2
    max_exact = half // 2
    for j, off in enumerate((ATT_BLOCK, 0, -ATT_BLOCK, -(1 << 20))):
        rel = c - r + off
        n = jnp.abs(rel)
        large = jnp.full_like(n, max_exact)
        for thr in (12, 16, 23, 32, 46, 64, 91):
            large = large + (n >= thr).astype(I32)
        bucket = jnp.where(rel > 0, half, 0) + jnp.where(n < max_exact, n, large)
        for h in range(DSA_HEADS):
            acc = jnp.zeros((DSA_Q_BLOCK, ATT_BLOCK), F32)
            for b in range(N_BUCKETS):
                acc = jnp.where(bucket == b, tab_ref[b, h], acc)
            o_ref[h, j] = acc
    for h in range(DSA_HEADS):
        top = tab_ref[0, h]
        for b in range(1, N_BUCKETS):
            top = jnp.maximum(top, tab_ref[b, h])
        max_ref[h] = jnp.full((8, LANES), top, F32)


def _bias_call(rel_bias):
    return pl.pallas_call(
        _bias_kernel,
        in_specs=[pl.BlockSpec(memory_space=pltpu.SMEM)],
        out_specs=[pl.BlockSpec(memory_space=pltpu.VMEM), pl.BlockSpec(memory_space=pltpu.VMEM)],
        out_shape=[jax.ShapeDtypeStruct((DSA_HEADS, 4, DSA_Q_BLOCK, ATT_BLOCK), F32),
                   jax.ShapeDtypeStruct((DSA_HEADS, 8, LANES), F32)],
        name="t5_bias_tiles",
    )(rel_bias)


def _softplus(z):
    return jnp.maximum(z, 0.0) + jnp.log1p(jnp.exp(-jnp.abs(z)))


def _sb_kernel(q_ref, k_ref, v_ref, o_ref, *, tq, last_fn):
    tk = SB_BLOCK
    last = last_fn(pl.program_id(2))
    lane = lax.broadcasted_iota(I32, (1, LANES), 1)
    low = lane < HEAD_DIM
    q = q_ref[...]
    zero = jnp.zeros_like(q)
    qh = (jnp.where(low, q, zero), jnp.where(low, zero, q))
    uj = lax.broadcasted_iota(I32, (tk, tk), 0)
    us = lax.broadcasted_iota(I32, (tk, tk), 1)
    u_mat = jnp.where(uj > us, 1.0, 0.0).astype(BF16)

    def tile(kb, vis, carries):
        start = pl.multiple_of(kb * tk, tk)
        kblk = k_ref[pl.ds(start, tk), :]
        vblk = v_ref[pl.ds(start, tk), :]
        pv = []
        new_carries = []
        for h in range(2):
            z = _dot_nt(qh[h], kblk)
            sp = _softplus(z)
            lk = -sp if vis is None else jnp.where(vis, -sp, 0.0)
            hi, lo = _split_bf16(lk)
            after = _dot(hi, u_mat) + _dot(lo, u_mat)
            w = jnp.exp((z - sp) + (after + carries[h]))
            if vis is not None:
                w = jnp.where(vis, w, 0.0)
            pv.append(_dot(w.astype(BF16), vblk))
            new_carries.append(carries[h] + jnp.sum(lk, axis=1, keepdims=True))
        return jnp.where(low, pv[0], pv[1]), new_carries

    r = lax.broadcasted_iota(I32, (tq, tk), 0)
    c = lax.broadcasted_iota(I32, (tq, tk), 1)
    zc = jnp.zeros((tq, 1), F32)
    acc, carries = tile(last, c < r, [zc, zc])

    def live(carries):
        return jnp.max(jnp.maximum(carries[0], carries[1])) > EXP_ZERO_BELOW

    def cond(st):
        kb, go, _, _, _ = st
        return jnp.logical_and(kb >= 0, go)

    def body(st):
        kb, _, acc, c0, c1 = st
        pv, nc = tile(kb, None, [c0, c1])
        return kb - 1, live(nc), acc + pv, nc[0], nc[1]

    st = lax.while_loop(cond, body, (last - 1, live(carries), acc, carries[0], carries[1]))
    o_ref[...] = st[2].astype(o_ref.dtype)


def _sb_call(q, k, v, tq, last_fn):
    s, t, _ = q.shape
    tkk = k.shape[1]
    return pl.pallas_call(
        functools.partial(_sb_kernel, tq=tq, last_fn=last_fn),
        grid=(s, SB_HEADS // 2, t // tq),
        in_specs=[pl.BlockSpec((None, tq, LANES), lambda b, p, i: (b, i, p)),
                  pl.BlockSpec((None, tkk, LANES), lambda b, p, i: (b, 0, p)),
                  pl.BlockSpec((None, tkk, LANES), lambda b, p, i: (b, 0, p))],
        out_specs=pl.BlockSpec((None, tq, LANES), lambda b, p, i: (b, i, p)),
        out_shape=jax.ShapeDtypeStruct(q.shape, BF16),
        compiler_params=_cparams(("arbitrary", "arbitrary", "arbitrary")),
        name="stick_breaking",
    )(q, k, v)


def _dsa_kernel(qi_ref, wi_ref, qb_ref, ki_ref, kb_ref, vx_ref, bt_ref, bmax_ref, o_ref,
                qis_sc, wb_sc, qbs_sc, key_sc, smax_sc, mx_sc, acc_sc, kn_sc, thr_sc, *, tq, topk, diag_fn, adm_fn):
    tk = ATT_BLOCK
    group = DSA_HEADS // DSA_KV_HEADS

    @pl.when(pl.program_id(1) == 0)
    def _():
        for n in range(DSA_KV_HEADS):
            def body(pi, best):
                start = pl.multiple_of(pi * (2 * tk), 2 * tk)
                x = kb_ref[pl.ds(start, 2 * tk), n * LANES:(n + 1) * LANES].astype(F32)
                return jnp.maximum(best, jnp.sum(x * x, axis=1, keepdims=True))
            best = lax.fori_loop(0, kb_ref.shape[0] // (2 * tk), body, jnp.zeros((2 * tk, 1), F32))
            kn_sc[n] = jnp.broadcast_to(jnp.sqrt(0.5 * jnp.max(best, axis=0, keepdims=True)), (8, LANES))

    diag = diag_fn(pl.program_id(1))
    last = diag + (tq + tk - 1) // tk - 1
    lane = lax.broadcasted_iota(I32, (1, LANES), 1)
    low = lane < HEAD_DIM

    for h in range(IDX_HEADS):
        p = h // 2
        t = qi_ref[:, p * LANES:(p + 1) * LANES]
        z = jnp.zeros_like(t)
        qis_sc[h * tq:(h + 1) * tq, :] = jnp.where(low, t, z) if h % 2 == 0 else jnp.where(low, z, t)
        t = qb_ref[:, p * LANES:(p + 1) * LANES]
        g = h % group
        qbs_sc[h // group, g * tq:(g + 1) * tq, :] = jnp.where(low, t, z) if h % 2 == 0 else jnp.where(low, z, t)
        wb_sc[h] = jnp.broadcast_to(wi_ref[:, h:h + 1], (tq, LANES))

    r = lax.broadcasted_iota(I32, (tq, tk), 0)
    c = lax.broadcasted_iota(I32, (tq, tk), 1)
    last_pair = last // 2
    smax_sc[...] = jnp.full((tq, tk), -jnp.inf, F32)

    def score_pair(pi, masked):
        start = pl.multiple_of(pi * (2 * tk), 2 * tk)
        d = _dot_nt(qis_sc[...], ki_ref[pl.ds(start, 2 * tk), :])
        for half in range(2):
            s = jnp.zeros((tq, tk), F32)
            for h in range(IDX_HEADS):
                s = s + wb_sc[h] * jnp.maximum(d[h * tq:(h + 1) * tq, half * tk:(half + 1) * tk], 0.0)
            bits = pltpu.bitcast(s, I32)
            key = jnp.where(bits < 0, bits ^ INT32_MAX, bits)
            if masked:
                adm = adm_fn(pl.program_id(1), r, c + (start + half * tk))
                key = jnp.where(adm, key, NEG_INF_KEY)
                s = jnp.where(adm, s, -jnp.inf)
            key_sc[2 * pi + half] = key
            smax_sc[...] = jnp.maximum(smax_sc[...], s)

    def score_body(pi, carry):
        score_pair(pi, False)
        return carry

    lax.fori_loop(0, last_pair, score_body, 0)
    score_pair(last_pair, True)

    rc = min(tq, ATT_BLOCK)
    ones_mat = jnp.ones((tk, tk), BF16)

    def row_total(x):
        return _dot(x.astype(BF16), ones_mat)

    def count_ge(*thrs):
        for j, t in enumerate(thrs):
            if hasattr(t, "shape"):
                thr_sc[j] = t
        chunks = list(range(0, tq, rc))

        def body(pi, accs):
            out = []
            for ci, r0 in enumerate(chunks):
                k0 = key_sc[2 * pi, r0:r0 + rc, :]
                k1 = key_sc[2 * pi + 1, r0:r0 + rc, :]
                for j, t in enumerate(thrs):
                    tt = thr_sc[j, r0:r0 + rc, :] if hasattr(t, "shape") else t
                    out.append(accs[ci * len(thrs) + j]
                               + (jnp.where(k0 >= tt, 1.0, 0.0) + jnp.where(k1 >= tt, 1.0, 0.0)))
            return tuple(out)

        accs = lax.fori_loop(0, last_pair + 1, body,
                             tuple(jnp.zeros((rc, tk), F32) for _ in range(len(chunks) * len(thrs))))
        totals = []
        for j in range(len(thrs)):
            parts = [row_total(accs[ci * len(thrs) + j]) for ci in range(len(chunks))]
            totals.append(parts[0] if len(parts) == 1 else jnp.concatenate(parts, axis=0))
        return totals

    def midpoint(lo, hi):
        return (lo >> 1) + (hi >> 1) + (lo & hi & 1)

    def float_to_key(v):
        bits = pltpu.bitcast(v, I32)
        return jnp.where(bits < 0, bits ^ INT32_MAX, bits)

    kf = float(topk)

    def bis_cond(st):
        return st[0]

    def is_open(piv, lo, cnt_lo):
        return jnp.logical_and(piv != lo, cnt_lo != kf)

    def bis_body(st):
        _, lo, hi, cnt_lo = st
        for _ in range(2):
            piv = midpoint(lo, hi)
            open_ = is_open(piv, lo, cnt_lo)
            cnt, = count_ge(piv)
            ge = cnt >= kf
            up = jnp.logical_and(open_, ge)
            lo = jnp.where(up, piv, lo)
            cnt_lo = jnp.where(up, cnt, cnt_lo)
            hi = jnp.where(jnp.logical_and(open_, jnp.logical_not(ge)), piv, hi)
        go = jnp.max(jnp.where(is_open(midpoint(lo, hi), lo, cnt_lo), 1.0, 0.0)) > 0.0
        return go, lo, hi, cnt_lo

    c_zero, c_pos = count_ge(0, 1)
    pos = c_pos >= kf
    zer = jnp.logical_and(c_zero >= kf, jnp.logical_not(pos))
    hi_max = jnp.broadcast_to(float_to_key(jnp.max(smax_sc[...], axis=1, keepdims=True)) + 1, (tq, tk))
    lo0 = jnp.where(pos, 1, jnp.where(zer, 0, NEG_INF_KEY))
    hi0 = jnp.where(pos, hi_max, jnp.where(zer, 1, 0))
    cnt_lo0 = jnp.where(pos, c_pos, jnp.where(zer, c_zero, -1.0))
    go0 = jnp.max(jnp.where(is_open(midpoint(lo0, hi0), lo0, cnt_lo0), 1.0, 0.0)) > 0.0
    _, thr, _, cnt_lo = lax.while_loop(bis_cond, bis_body, (go0, lo0, hi0, cnt_lo0))

    tied = cnt_lo > kf

    @pl.when(jnp.max(jnp.where(tied, 1.0, 0.0)) > 0.0)
    def _():
        above, = count_ge(thr + 1)
        need = kf - above
        prefix_mat = jnp.where(lax.broadcasted_iota(I32, (tk, tk), 0) <= lax.broadcasted_iota(I32, (tk, tk), 1),
                               1.0, 0.0).astype(BF16)

        def demote(kb, before):
            key = key_sc[kb]
            eq = jnp.logical_and(tied, key == thr)
            ind = jnp.where(eq, 1.0, 0.0)
            rank = _dot(ind.astype(BF16), prefix_mat) + before
            key_sc[kb] = jnp.where(jnp.logical_and(eq, rank > need), NEG_INF_KEY, key)
            return before + row_total(ind)

        lax.fori_loop(0, last + 1, demote, jnp.zeros((tq, tk), F32))

    thr_eff = jnp.maximum(thr, NEG_INF_KEY + 1)

    def masked_logits(pi, n):
        start = pl.multiple_of(pi * (2 * tk), 2 * tk)
        lg = _dot_nt(qbs_sc[n], kb_ref[pl.ds(start, 2 * tk), n * LANES:(n + 1) * LANES])
        sel = [key_sc[2 * pi + half] >= thr_eff for half in range(2)]
        j = [jnp.clip(diag - (2 * pi + half) + 1, 0, 3) for half in range(2)]
        out = []
        for g in range(group):
            h = n * group + g
            out.append([jnp.where(sel[half],
                                  lg[g * tq:(g + 1) * tq, half * tk:(half + 1) * tk] + bt_ref[h, j[half], 0:tq, :],
                                  MASKED_LOGIT) for half in range(2)])
        return out

    def max_body(pi, carry):
        for n in range(DSA_KV_HEADS):
            for g, (la, lb) in enumerate(masked_logits(pi, n)):
                h = n * group + g
                mx_sc[h] = jnp.maximum(mx_sc[h], jnp.maximum(la, lb))
        return carry

    def exact_row_maxima():
        for h in range(DSA_HEADS):
            mx_sc[h] = jnp.full((tq, LANES), MASKED_LOGIT, F32)
        lax.fori_loop(0, last_pair + 1, max_body, 0)
        for h in range(DSA_HEADS):
            mx_sc[h] = jnp.broadcast_to(jnp.max(mx_sc[h], axis=1, keepdims=True), (tq, LANES))

    def bounded_row_maxima():
        for h in range(DSA_HEADS):
            n, g = h // group, h % group
            q = qbs_sc[n, g * tq:(g + 1) * tq, :].astype(F32)
            q_norm = jnp.sqrt(jnp.sum(q * q, axis=1, keepdims=True))
            mx_sc[h] = q_norm * kn_sc[n, 0:1, :] + bmax_ref[h, 0:1, :]

    def pv_body(pi, carry):
        start = pl.multiple_of(pi * (2 * tk), 2 * tk)
        for n in range(DSA_KV_HEADS):
            ps = []
            for g, (la, lb) in enumerate(masked_logits(pi, n)):
                m = mx_sc[n * group + g]
                ps.append(jnp.concatenate([jnp.exp(la - m).astype(BF16), jnp.exp(lb - m).astype(BF16)], axis=1))
            pv = _dot(jnp.concatenate(ps, axis=0), vx_ref[pl.ds(start, 2 * tk), n * LANES:(n + 1) * LANES])
            for g in range(group):
                h = n * group + g
                acc_sc[h] = acc_sc[h] + pv[g * tq:(g + 1) * tq]
        return carry

    def weights_sweep():
        for h in range(DSA_HEADS):
            acc_sc[h] = jnp.zeros((tq, LANES), F32)
        lax.fori_loop(0, last_pair + 1, pv_body, 0)

    bounded_row_maxima()
    weights_sweep()
    den = acc_sc[0][:, HEAD_DIM:HEAD_DIM + 1]
    for h in range(1, DSA_HEADS):
        den = jnp.minimum(den, acc_sc[h][:, HEAD_DIM:HEAD_DIM + 1])
    healthy = jnp.min(jnp.where(den >= SOFTMAX_DEN_FLOOR, 1.0, 0.0)) > 0.0

    @pl.when(jnp.logical_not(healthy))
    def _():
        exact_row_maxima()
        weights_sweep()

    def normalised(h):
        a = acc_sc[h]
        return a / pltpu.roll(a, HEAD_DIM, axis=1)

    for p in range(DSA_HEADS // 2):
        o1 = pltpu.roll(normalised(2 * p + 1), HEAD_DIM, axis=1)
        o_ref[:, p * LANES:(p + 1) * LANES] = jnp.where(low, normalised(2 * p), o1).astype(o_ref.dtype)


def _dsa_call(qi, wi, qb, kid, kbd, vx, btiles, bmax, tq, topk, diag_fn, adm_fn):
    s, t, _ = qi.shape
    tkk = kid.shape[1]
    assert tkk % (2 * ATT_BLOCK) == 0, "keys must come in whole pairs of blocks"
    nkb = tkk // ATT_BLOCK
    rowq = lambda w: pl.BlockSpec((None, tq, w), lambda b, i: (b, i, 0))
    full = lambda w: pl.BlockSpec((None, tkk, w), lambda b, i: (b, 0, 0))
    return pl.pallas_call(
        functools.partial(_dsa_kernel, tq=tq, topk=topk, diag_fn=diag_fn, adm_fn=adm_fn),
        grid=(s, t // tq),
        in_specs=[rowq(512), rowq(LANES), rowq(512), full(LANES), full(2 * LANES), full(2 * LANES),
                  pl.BlockSpec(btiles.shape, lambda b, i: (0, 0, 0, 0)),
                  pl.BlockSpec(bmax.shape, lambda b, i: (0, 0, 0))],
        out_specs=rowq(512),
        out_shape=jax.ShapeDtypeStruct((s, t, 512), BF16),
        scratch_shapes=[pltpu.VMEM((IDX_HEADS * tq, LANES), BF16),
                        pltpu.VMEM((IDX_HEADS, tq, LANES), F32),
                        pltpu.VMEM((DSA_KV_HEADS, DSA_HEADS // DSA_KV_HEADS * tq, LANES), BF16),
                        pltpu.VMEM((nkb, tq, ATT_BLOCK), I32),
                        pltpu.VMEM((tq, ATT_BLOCK), F32),
                        pltpu.VMEM((DSA_HEADS, tq, LANES), F32),
                        pltpu.VMEM((DSA_HEADS, tq, LANES), F32),
                        pltpu.VMEM((DSA_KV_HEADS, 8, LANES), F32),
                        pltpu.VMEM((2, tq, ATT_BLOCK), I32)],
        compiler_params=_cparams(("arbitrary", "arbitrary")),
        name="dsa_attention",
    )(qi, wi, qb, kid, kbd, vx, btiles, bmax)


def _layer_norm(x, g, b):
    mu = jnp.mean(x, axis=-1, keepdims=True)
    xc = x - mu
    var = jnp.mean(xc * xc, axis=-1, keepdims=True)
    return xc * lax.rsqrt(var + LN_EPS) * g + b


def _post_kernel(oa_ref, ob_ref, sg_ref, x_ref, mod_ref, wsb_ref, wdsa_ref, wout_ref, g_ref, b_ref,
                 wrh_ref, wrl_ref, br_ref, x1_ref, h2_ref, ti_ref, tg_ref):
    tm = x_ref.shape[0]
    sub = min(tm, POST_SUB_ROWS)
    for r0 in range(0, tm, sub):
        _post_rows(slice(r0, r0 + sub), oa_ref, ob_ref, sg_ref, x_ref, mod_ref, wsb_ref, wdsa_ref, wout_ref,
                   g_ref, b_ref, wrh_ref, wrl_ref, br_ref, x1_ref, h2_ref, ti_ref, tg_ref)


def _post_rows(rows, oa_ref, ob_ref, sg_ref, x_ref, mod_ref, wsb_ref, wdsa_ref, wout_ref, g_ref, b_ref,
               wrh_ref, wrl_ref, br_ref, x1_ref, h2_ref, ti_ref, tg_ref):
    d = D_MODEL
    ya = _dot(oa_ref[rows, :], wsb_ref[...])
    yb = _dot(ob_ref[rows, :], wdsa_ref[...])
    merged = sg_ref[rows, :d] * ya + sg_ref[rows, d:] * yb
    mix = _dot(merged.astype(BF16), wout_ref[...])
    g1 = mod_ref[2:3, :]
    x1 = _layer_norm(DEEPNORM_ALPHA * x_ref[rows, :] + g1 * mix, g_ref[...], b_ref[...])
    x1_ref[rows, :] = x1
    h2 = x1 * (1.0 + mod_ref[4:5, :]) + mod_ref[3:4, :]
    h2_ref[rows, :] = h2
    logits = _dot3(h2, wrh_ref[...], wrl_ref[...]) + br_ref[...]
    lane = lax.broadcasted_iota(I32, logits.shape, 1).astype(F32)
    neg = -jnp.inf
    cur = jnp.where(lane < N_EXPERTS, logits, neg)
    vals, idxs = [], []
    for _ in range(TOP_K_EXPERTS):
        m = jnp.max(cur, axis=1, keepdims=True)
        idx = jnp.min(jnp.where(cur == m, lane, float(LANES)), axis=1, keepdims=True)
        vals.append(m)
        idxs.append(idx)
        cur = jnp.where(lane == idx, neg, cur)
    es = [jnp.exp(v - vals[0]) for v in vals]
    tot = es[0] + es[1] + es[2] + es[3]
    ti = jnp.zeros(logits.shape, F32)
    tg = jnp.zeros(logits.shape, F32)
    for k in range(TOP_K_EXPERTS):
        ti = jnp.where(lane == k, idxs[k], ti)
        tg = jnp.where(lane == k, es[k] / tot, tg)
    ti_ref[rows, :] = ti.astype(I32)
    tg_ref[rows, :] = tg


def _post_call(oa, ob, sg, x, mod, wsb, wdsa, wout, ln_g, ln_b, wr_hi, wr_lo, br, tm):
    s, t, d = x.shape
    row = lambda w: pl.BlockSpec((None, tm, w), lambda b, i: (b, i, 0))
    const = lambda a: pl.BlockSpec(a.shape, lambda b, i: (0,) * a.ndim)
    shp = lambda w, dt: jax.ShapeDtypeStruct((s, t, w), dt)
    return pl.pallas_call(
        _post_kernel,
        grid=(s, t // tm),
        in_specs=[row(512), row(512), row(2 * d), row(d),
                  pl.BlockSpec((None, 6, d), lambda b, i: (b, 0, 0)),
                  const(wsb), const(wdsa), const(wout), const(ln_g), const(ln_b),
                  const(wr_hi), const(wr_lo), const(br)],
        out_specs=[row(d), row(d), row(LANES), row(LANES)],
        out_shape=[shp(d, F32), shp(d, F32), shp(LANES, I32), shp(LANES, F32)],
        compiler_params=_cparams(("arbitrary", "arbitrary")),
        name="post_attention_router",
    )(oa, ob, sg, x, mod, wsb, wdsa, wout, ln_g, ln_b, wr_hi, wr_lo, br)


def _dispatch_kernel(zoff_ref, nused_ref, dest_ref, h_ref, xs_hbm, buf, zbuf, sem, zsem, *, tm, tile, n_tiles):
    i = pl.program_id(0)
    nb = pl.num_programs(0)
    k = TOP_K_EXPERTS
    slot = i % 2

    def row_copy(dst, slot, r):
        return pltpu.make_async_copy(buf.at[slot, pl.ds(r, 1), :], xs_hbm.at[pl.ds(dst, 1), :], sem.at[slot])

    def wait_all(slot):
        def body(r, carry):
            for _ in range(k):
                row_copy(0, slot, r).wait()
            return carry
        lax.fori_loop(0, tm, body, 0, unroll=4)

    def zero_fill(start):
        return pltpu.make_async_copy(zbuf, xs_hbm.at[pl.ds(start, tile), :], zsem)

    @pl.when(i == 0)
    def _():
        zbuf[...] = jnp.zeros_like(zbuf)
        fills = [(zoff_ref[e] >= 0, pl.multiple_of(jnp.maximum(zoff_ref[e], 0), MOE_BLOCK)) for e in range(N_EXPERTS)]
        fills += [(t >= nused_ref[0], t * tile) for t in range(max(n_tiles - N_EXPERTS - 1, 0), n_tiles)]
        for on, start in fills:
            @pl.when(on)
            def _():
                zero_fill(start).start()
        for on, start in fills:
            @pl.when(on)
            def _():
                zero_fill(start).wait()

    @pl.when(i >= 2)
    def _():
        wait_all(slot)

    buf[slot] = h_ref[...]
    for r in range(tm):
        for j in range(k):
            row_copy(dest_ref[0, r * k + j], slot, r).start()

    @pl.when(i == nb - 1)
    def _():
        wait_all(slot)

        @pl.when(nb >= 2)
        def _():
            wait_all(1 - slot)


def _dispatch_call(zoff, nused, dest, h2, n_rows, tile):
    n, d = h2.shape
    tm = min(MOE_BLOCK, n)
    nb = n // tm
    dest3 = dest.reshape(nb, 1, tm * TOP_K_EXPERTS)
    grid_spec = pltpu.PrefetchScalarGridSpec(
        num_scalar_prefetch=2,
        grid=(nb,),
        in_specs=[pl.BlockSpec((None, 1, tm * TOP_K_EXPERTS), lambda i, zo, nu: (i, 0, 0), memory_space=pltpu.SMEM),
                  pl.BlockSpec((tm, d), lambda i, zo, nu: (i, 0))],
        out_specs=pl.BlockSpec(memory_space=pl.ANY),
        scratch_shapes=[pltpu.VMEM((2, tm, d), F32), pltpu.VMEM((tile, d), F32),
                        pltpu.SemaphoreType.DMA((2,)), pltpu.SemaphoreType.DMA(())],
    )
    return pl.pallas_call(
        functools.partial(_dispatch_kernel, tm=tm, tile=tile, n_tiles=n_rows // tile),
        grid_spec=grid_spec,
        out_shape=jax.ShapeDtypeStruct((n_rows, d), F32),
        compiler_params=_cparams(("arbitrary",)),
        name="moe_dispatch",
    )(zoff, nused, dest3, h2)


def _ffn_kernel(be_ref, x_ref, wup_ref, bup_ref, wdn_ref, bdn_ref, o_ref):
    u = _dot(x_ref[...].astype(BF16), wup_ref[...]) + bup_ref[...]
    acts = []
    for t in range(2 * D_FF // SWIGLU_TILE):
        a = t * SWIGLU_TILE
        glu = jnp.minimum(u[:, a:a + LANES], SWIGLU_LIMIT)
        lin = jnp.clip(u[:, a + LANES:a + SWIGLU_TILE], -SWIGLU_LIMIT, SWIGLU_LIMIT)
        acts.append((glu * jax.nn.sigmoid(SWIGLU_ALPHA * glu) * (lin + 1.0)).astype(BF16))
    o_ref[...] = _dot(jnp.concatenate(acts, axis=1), wdn_ref[...]) + bdn_ref[...]


def _deinterleave_kernel(w_ref, o_ref):
    j = lax.broadcasted_iota(I32, (SWIGLU_TILE, SWIGLU_TILE), 0)
    s = lax.broadcasted_iota(I32, (SWIGLU_TILE, SWIGLU_TILE), 1)
    src = jnp.where(s < LANES, 2 * s, 2 * (s - LANES) + 1)
    perm = jnp.where(j == src, 1.0, 0.0).astype(BF16)
    for t in range(w_ref.shape[1] // SWIGLU_TILE):
        a = t * SWIGLU_TILE
        o_ref[:, a:a + SWIGLU_TILE] = _dot(w_ref[:, a:a + SWIGLU_TILE].astype(BF16), perm).astype(BF16)


def _deinterleave_call(w_up):
    e, d, f = w_up.shape
    tr = 512
    return pl.pallas_call(
        _deinterleave_kernel,
        grid=(e, d // tr),
        in_specs=[pl.BlockSpec((None, tr, f), lambda a, b: (a, b, 0))],
        out_specs=pl.BlockSpec((None, tr, f), lambda a, b: (a, b, 0)),
        out_shape=jax.ShapeDtypeStruct((e, d, f), BF16),
        compiler_params=_cparams(("arbitrary", "arbitrary")),
        name="w_up_tiles",
    )(w_up)


def _ffn_call(block_expert, xs, wup, bup, wdn, bdn, tile):
    n_rows, d = xs.shape
    n_blocks = n_rows // tile
    grid_spec = pltpu.PrefetchScalarGridSpec(
        num_scalar_prefetch=1,
        grid=(n_blocks,),
        in_specs=[pl.BlockSpec((tile, d), lambda i, be: (i, 0)),
                  pl.BlockSpec((None, d, 2 * D_FF), lambda i, be: (be[i], 0, 0)),
                  pl.BlockSpec((None, 1, 2 * D_FF), lambda i, be: (be[i], 0, 0)),
                  pl.BlockSpec((None, D_FF, d), lambda i, be: (be[i], 0, 0)),
                  pl.BlockSpec((None, 1, d), lambda i, be: (be[i], 0, 0))],
        out_specs=pl.BlockSpec((tile, d), lambda i, be: (i, 0)),
    )
    return pl.pallas_call(
        _ffn_kernel,
        grid_spec=grid_spec,
        out_shape=jax.ShapeDtypeStruct((n_rows, d), F32),
        compiler_params=_cparams(("arbitrary",)),
        name="expert_ffn",
    )(block_expert, xs, wup, bup, wdn, bdn)


def _combine_kernel(pos0_ref, posn_ref, rows_hbm, x1_ref, tg_ref, mod_ref, g_ref, b_ref, o_ref, buf, sem, *, tm):
    i = pl.program_id(0)
    nb = pl.num_programs(0)
    k = TOP_K_EXPERTS

    def row_copy(src, slot, r, j):
        return pltpu.make_async_copy(rows_hbm.at[pl.ds(src, 1), :], buf.at[slot, j, pl.ds(r, 1), :], sem.at[slot])

    def issue(pos_ref, slot):
        def body(r, carry):
            for j in range(k):
                row_copy(pos_ref[0, r * k + j], slot, r, j).start()
            return carry
        lax.fori_loop(0, tm, body, 0, unroll=4)

    slot = i % 2

    @pl.when(i == 0)
    def _():
        issue(pos0_ref, 0)

    @pl.when(i + 1 < nb)
    def _():
        issue(posn_ref, 1 - slot)

    def wait_body(r, carry):
        for j in range(k):
            row_copy(0, slot, r, j).wait()
        return carry
    lax.fori_loop(0, tm, wait_body, 0, unroll=4)

    gated = [buf[slot, j] * tg_ref[:, j:j + 1] for j in range(k)]
    y = (gated[0] + gated[1]) + (gated[2] + gated[3])
    o_ref[...] = _layer_norm(DEEPNORM_ALPHA * x1_ref[...] + mod_ref[5:6, :] * y, g_ref[...], b_ref[...])


def _combine_call(pos, rows, x1, tg, mod, seq_len, ln_g, ln_b, tm):
    n, d = x1.shape
    nb = n // tm
    per_seq = seq_len // tm
    pos3 = pos.reshape(nb, 1, tm * TOP_K_EXPERTS)
    return pl.pallas_call(
        functools.partial(_combine_kernel, tm=tm),
        grid=(nb,),
        in_specs=[pl.BlockSpec((None, 1, tm * TOP_K_EXPERTS), lambda i: (0, 0, 0), memory_space=pltpu.SMEM),
                  pl.BlockSpec((None, 1, tm * TOP_K_EXPERTS), lambda i: (jnp.minimum(i + 1, nb - 1), 0, 0),
                               memory_space=pltpu.SMEM),
                  pl.BlockSpec(memory_space=pl.ANY),
                  pl.BlockSpec((tm, d), lambda i: (i, 0)),
                  pl.BlockSpec((tm, LANES), lambda i: (i, 0)),
                  pl.BlockSpec((None, 6, d), lambda i: (i // per_seq, 0, 0)),
                  pl.BlockSpec((1, d), lambda i: (0, 0)),
                  pl.BlockSpec((1, d), lambda i: (0, 0))],
        out_specs=pl.BlockSpec((tm, d), lambda i: (i, 0)),
        out_shape=jax.ShapeDtypeStruct((n, d), F32),
        scratch_shapes=[pltpu.VMEM((2, TOP_K_EXPERTS, tm, d), F32), pltpu.SemaphoreType.DMA((2,))],
        compiler_params=_cparams(("arbitrary",)),
        name="moe_combine_ln2",
    )(pos3, pos3, rows, x1, tg, mod, ln_g, ln_b)


def _rank_kernel(ti_ref, rank_ref, cnt_ref, carry_sc):
    tb = ti_ref.shape[0]

    @pl.when(pl.program_id(0) == 0)
    def _():
        carry_sc[...] = jnp.zeros_like(carry_sc)

    lane = lax.broadcasted_iota(I32, (tb, LANES), 1)
    ti = ti_ref[...]
    hits = [ti[:, k:k + 1] == lane for k in range(TOP_K_EXPERTS)]
    tot = jnp.zeros((tb, LANES), F32)
    for hit in hits:
        tot = tot + jnp.where(hit, 1.0, 0.0)
    earlier = jnp.where(lax.broadcasted_iota(I32, (tb, tb), 1) < lax.broadcasted_iota(I32, (tb, tb), 0),
                        1.0, 0.0).astype(BF16)
    before = _dot(earlier, tot.astype(BF16)) + carry_sc[0:1, :]
    out = jnp.zeros((tb, LANES), F32)
    for k, hit in enumerate(hits):
        out = jnp.where(lane == k, jnp.sum(jnp.where(hit, before, 0.0), axis=1, keepdims=True), out)
    rank_ref[...] = out.astype(I32)
    carry_sc[...] = carry_sc[...] + jnp.sum(tot, axis=0, keepdims=True)
    cnt_ref[...] = carry_sc[...]


def _rank_call(ti):
    n = ti.shape[0]
    tb = min(512, n)
    return pl.pallas_call(
        _rank_kernel,
        grid=(n // tb,),
        in_specs=[pl.BlockSpec((tb, LANES), lambda i: (i, 0))],
        out_specs=[pl.BlockSpec((tb, LANES), lambda i: (i, 0)), pl.BlockSpec((8, LANES), lambda i: (0, 0))],
        out_shape=[jax.ShapeDtypeStruct((n, LANES), I32), jax.ShapeDtypeStruct((8, LANES), F32)],
        scratch_shapes=[pltpu.VMEM((8, LANES), F32)],
        compiler_params=_cparams(("arbitrary",)),
        name="moe_rank",
    )(ti)


def _routing(ti, tile):
    n_tok = ti.shape[0]
    n_assign = n_tok * TOP_K_EXPERTS
    e_flat = ti[:, :TOP_K_EXPERTS].reshape(-1)
    rank_tile, cnt = _rank_call(ti)
    rank = rank_tile[:, :TOP_K_EXPERTS].reshape(-1)
    counts = cnt[0, :N_EXPERTS].astype(I32)
    padded = (counts + tile - 1) // tile * tile
    pend = jnp.cumsum(padded)
    pstart = pend - padded
    dest = (pstart[e_flat] + rank).astype(I32)
    n_rows = (n_assign + N_EXPERTS * (tile - 1) + tile - 1) // tile * tile
    n_blocks = n_rows // tile
    tile_start = jnp.arange(n_blocks, dtype=I32) * tile
    block_expert = jnp.minimum(jnp.sum((pend[None, :] <= tile_start[:, None]).astype(I32), axis=1),
                               N_EXPERTS - 1).astype(I32)
    last_tile = jnp.where(padded > 0, pend - tile, -1).astype(I32)
    nused = (pend[-1] // tile).astype(I32).reshape(1)
    return dest, block_expert, last_tile, nused, n_rows


def _stream(x, mod, kv_cache, weights, btiles, tm, tq):
    (w_packed, b_packed, wsb, wdsa, wout, ln1_g, ln1_b, wr_hi, wr_lo, br,
     wup, bup, wdn, bdn, ln2_g, ln2_b) = weights
    s, t, d = x.shape
    (qa, ka32, va32, ka16, va16, qb, kb32, vb32, kbd, vx, qi, ki32, kid, wi, sg) = _proj_call(x, mod, w_packed, b_packed, tm)

    if kv_cache is None:
        k_sb, v_sb, k_id, k_bd, v_x = ka16, va16, kid, kbd, vx
        total = t
        assert tq % ATT_BLOCK == 0
        diag_fn = lambda i: i * (tq // ATT_BLOCK)
        adm_fn = lambda i, r, key_pos: key_pos < i * tq + (r // CHUNK + 1) * CHUNK
    else:
        past = kv_cache[0].shape[1]
        total = past + t
        pad = (-total) % (2 * ATT_BLOCK)
        cat = lambda cache, new: jnp.concatenate(
            [cache, new, jnp.zeros((s, pad, new.shape[2]), new.dtype)], axis=1)
        k_sb, v_sb, k_id, k_bd, v_x = [cat(c_, n_) for c_, n_ in zip(kv_cache, (ka16, va16, kid, kbd, vx))]
        assert past % ATT_BLOCK == 0 and t <= ATT_BLOCK
        diag_fn = lambda i: i * 0 + past // ATT_BLOCK
        adm_fn = lambda i, r, key_pos: key_pos < total
    topk = max(1, min(TOPK_MAX, total // 4))

    if kv_cache is None:
        oa = _sb_call(qa, k_sb, v_sb, min(SB_BLOCK, t), lambda i: i)
    else:
        assert kv_cache[0].shape[1] % SB_BLOCK == 0 and t <= SB_BLOCK
        oa = _sb_call(qa, k_sb, v_sb, t, lambda i: i * 0 + kv_cache[0].shape[1] // SB_BLOCK)
    ob = _dsa_call(qi, wi, qb, k_id, k_bd, v_x, *btiles, tq, topk, diag_fn, adm_fn)
    x1, h2, ti, tg = _post_call(oa, ob, sg, x, mod, wsb, wdsa, wout, ln1_g, ln1_b, wr_hi, wr_lo, br,
                                min(2 * POST_SUB_ROWS, t))

    n = s * t
    tile = 4 * MOE_BLOCK if n * TOP_K_EXPERTS >= N_EXPERTS * 8 * MOE_BLOCK else MOE_BLOCK
    dest, block_expert, last_tile, nused, n_rows = _routing(ti.reshape(n, LANES), tile)
    xs = _dispatch_call(last_tile, nused, dest, h2.reshape(n, d), n_rows, tile)
    rows = _ffn_call(block_expert, xs, wup, bup, wdn, bdn, tile)
    y = _combine_call(dest, rows, x1.reshape(n, d), tg.reshape(n, LANES), mod, t, ln2_g, ln2_b, min(128, t))
    new_rows = (ka32.reshape(1, s, t, SB_HEADS, HEAD_DIM), va32.reshape(1, s, t, SB_HEADS, HEAD_DIM),
                kb32.reshape(1, s, t, DSA_KV_HEADS, HEAD_DIM), vb32.reshape(1, s, t, DSA_KV_HEADS, HEAD_DIM),
                ki32.reshape(1, s, t, IDX_DIM))
    return y.reshape(s, t, d), new_rows


def kernel(x_prompt, x_sample, cache_sb_k, cache_sb_v, cache_dsa_k, cache_dsa_v, cache_idx_k, c_prompt, c_sample, rel_bias, w_ada, b_ada, w_in, b_in, w_o_sb, w_o_dsa, w_out, ln1_g, ln1_b, w_router, b_router, w_up, b_up, w_down, b_down, ln2_g, ln2_b):
    d = D_MODEL
    nb, ns = x_prompt.shape[0], x_sample.shape[0]
    past = cache_sb_k.shape[2]

    mod = _mod_call(jnp.concatenate([c_prompt, c_sample], axis=0), w_ada[0], b_ada[0]).reshape(nb + ns, 6, d)
    btiles = _bias_call(rel_bias)

    w_packed, b_packed = _pack_w_in(w_in[0], b_in[0])
    wr = jnp.concatenate([w_router[0], jnp.zeros((d, LANES - N_EXPERTS), F32)], axis=1)
    wr_hi = wr.astype(BF16)
    wr_lo = (wr - wr_hi.astype(F32)).astype(BF16)
    br = jnp.concatenate([b_router[0], jnp.zeros((LANES - N_EXPERTS,), F32)]).reshape(1, LANES)
    wup = _deinterleave_call(w_up[0])
    bup = b_up[0].reshape(N_EXPERTS, 2 * D_FF // SWIGLU_TILE, LANES, 2).swapaxes(2, 3).reshape(N_EXPERTS, 1, 2 * D_FF)
    weights = (w_packed, b_packed, w_o_sb[0].astype(BF16), w_o_dsa[0].astype(BF16), w_out[0].astype(BF16),
               ln1_g[0].reshape(1, d), ln1_b[0].reshape(1, d), wr_hi, wr_lo, br,
               wup, bup, w_down[0].astype(BF16), b_down[0].reshape(N_EXPERTS, 1, d),
               ln2_g[0].reshape(1, d), ln2_b[0].reshape(1, d))

    dup = lambda a: jnp.concatenate([a[..., :64], a[..., :64], a[..., 64:], a[..., 64:]], axis=-1)
    idx_c = cache_idx_k[0]
    caches = (cache_sb_k[0].reshape(ns, past, SB_HEADS * HEAD_DIM).astype(BF16),
              cache_sb_v[0].reshape(ns, past, SB_HEADS * HEAD_DIM).astype(BF16),
              jnp.concatenate([idx_c, idx_c], axis=-1).astype(BF16),
              dup(cache_dsa_k[0].reshape(ns, past, DSA_KV_HEADS * HEAD_DIM)).astype(BF16),
              _with_ones(cache_dsa_v[0].reshape(ns, past, DSA_KV_HEADS * HEAD_DIM)).astype(BF16))

    t_p, t_s = x_prompt.shape[1], x_sample.shape[1]
    y_p, new_p = _stream(x_prompt, mod[:nb], None, weights, btiles, min(256, t_p), min(DSA_Q_BLOCK, t_p))
    y_s, new_s = _stream(x_sample, mod[nb:], caches, weights, btiles, t_s, t_s)
    return (y_p, y_s) + new_p + new_s
```

```python
import functools

import jax
import jax.numpy as jnp
import numpy as np
from jax import lax
from jax.experimental import pallas as pl
from jax.experimental.pallas import tpu as pltpu

F32 = jnp.float32
BF16 = jnp.bfloat16
I32 = jnp.int32

D_MODEL = 1024
CHUNK = 64
SB_HEADS = 8
HEAD_DIM = 64
DSA_HEADS = 8
DSA_KV_HEADS = 2
IDX_HEADS = 8
IDX_DIM = 64
TOPK_MAX = 256
N_BUCKETS = 32
N_EXPERTS = 32
TOP_K_EXPERTS = 4
D_FF = 1024
SWIGLU_LIMIT = 7.0
SWIGLU_ALPHA = 1.702
MOE_BLOCK = 128
LN_EPS = 1e-5
DEPTH = 1
DEEPNORM_ALPHA = (2.0 * DEPTH) ** 0.25

LANES = 128
ATT_BLOCK = 128
DSA_Q_BLOCK = 256
SB_BLOCK = 256
VMEM_LIMIT = 56 * 1024 * 1024

EXP_ZERO_BELOW = -104.0
NEG_INF_KEY = -2139095041
INT32_MAX = 2147483647
MASKED_LOGIT = -1e30
POST_SUB_ROWS = 256
SOFTMAX_DEN_FLOOR = 1e-30
SWIGLU_TILE = 2 * LANES

_SEG = {}
_off = 0
for _name, _w in (("qa", 512), ("ka", 512), ("va", 512), ("qb", 512), ("kb", 128), ("vb", 128),
                  ("kbd", 256), ("vx", 256), ("qi", 512), ("kid", 128), ("wi", 128),
                  ("ga", 1024), ("gb", 1024)):
    _SEG[_name] = (_off, _off + _w)
    _off += _w
PACKED_COLS = _off


def _cparams(sem):
    return pltpu.CompilerParams(dimension_semantics=sem, vmem_limit_bytes=VMEM_LIMIT)


def _dot(a, b):
    return jnp.dot(a, b, preferred_element_type=F32)


def _dot_nt(a, b):
    return lax.dot_general(a, b, (((1,), (1,)), ((), ())), preferred_element_type=F32)


def _split_bf16(x):
    hi = x.astype(BF16)
    lo = (x - hi.astype(F32)).astype(BF16)
    return hi, lo


def _dot3(a, b_hi, b_lo):
    a_hi, a_lo = _split_bf16(a)
    return _dot(a_hi, b_hi) + (_dot(a_hi, b_lo) + _dot(a_lo, b_hi))


def _mod_kernel(c_ref, w_ref, b_ref, o_ref):
    c = c_ref[...]
    s = c * jax.nn.sigmoid(c)
    w_hi, w_lo = _split_bf16(w_ref[...])
    o_ref[...] = _dot3(s, w_hi, w_lo) + b_ref[...]


def _mod_call(c_all, w_ada, b_ada):
    n, d = c_all.shape
    cols = w_ada.shape[1]
    tn = 1024
    return pl.pallas_call(
        _mod_kernel,
        grid=(cols // tn,),
        in_specs=[pl.BlockSpec((n, d), lambda j: (0, 0)),
                  pl.BlockSpec((d, tn), lambda j: (0, j)),
                  pl.BlockSpec((1, tn), lambda j: (0, j))],
        out_specs=pl.BlockSpec((n, tn), lambda j: (0, j)),
        out_shape=jax.ShapeDtypeStruct((n, cols), F32),
        compiler_params=_cparams(("arbitrary",)),
        name="adaln_mod",
    )(c_all, w_ada, b_ada.reshape(1, cols))


def _proj_kernel(x_ref, mod_ref, w_ref, b_ref,
                 qa_ref, ka32_ref, va32_ref, ka16_ref, va16_ref,
                 qb_ref, kb32_ref, vb32_ref, kbd_ref, vx_ref,
                 qi_ref, ki32_ref, kid_ref, wi_ref, sg_ref):
    sh1 = mod_ref[0:1, :]
    sc1 = mod_ref[1:2, :]
    h = (x_ref[...] * (1.0 + sc1) + sh1).astype(BF16)

    def seg(name):
        a, b = _SEG[name]
        return _dot(h, w_ref[:, a:b]) + b_ref[:, a:b]

    qa_ref[...] = (seg("qa") * HEAD_DIM ** -0.5).astype(BF16)
    ka = seg("ka")
    ka32_ref[...] = ka
    ka16_ref[...] = ka.astype(BF16)
    va = seg("va")
    va32_ref[...] = va
    va16_ref[...] = va.astype(BF16)
    qb_ref[...] = (seg("qb") * HEAD_DIM ** -0.5).astype(BF16)
    kb32_ref[...] = seg("kb")
    vb32_ref[...] = seg("vb")
    kbd_ref[...] = seg("kbd").astype(BF16)
    vx_ref[...] = seg("vx").astype(BF16)
    qi_ref[...] = (seg("qi") * IDX_DIM ** -0.5).astype(BF16)
    kid = seg("kid")
    ki32_ref[...] = kid[:, :IDX_DIM]
    kid_ref[...] = kid.astype(BF16)
    wi_ref[...] = seg("wi") * IDX_HEADS ** -0.5
    a, _ = _SEG["ga"]
    _, b = _SEG["gb"]
    sg_ref[...] = jax.nn.sigmoid(_dot(h, w_ref[:, a:b]) + b_ref[:, a:b])


def _pack_w_in(w_in, b_in):
    offs = np.cumsum((0, 512, 512, 512, 512, 128, 128, 512, 8, 64, 1024, 1024))
    qa, ka, va, qb, kb, vb, qi, wi, ki, ga, gb = [slice(int(offs[i]), int(offs[i + 1])) for i in range(11)]

    def pack(m, fill):
        kb_m, vb_m = m[..., kb], m[..., vb]
        dup = lambda t: jnp.concatenate([t[..., :64], t[..., :64], t[..., 64:], t[..., 64:]], axis=-1)
        wi_m = jnp.concatenate([m[..., wi], jnp.zeros(m.shape[:-1] + (LANES - IDX_HEADS,), m.dtype)], axis=-1)
        return jnp.concatenate([m[..., qa], m[..., ka], m[..., va], m[..., qb], kb_m, vb_m, dup(kb_m),
                                _with_ones(vb_m, fill),
                                m[..., qi], m[..., ki], m[..., ki], wi_m, m[..., ga], m[..., gb]], axis=-1)

    return pack(w_in, 0.0).astype(BF16), pack(b_in.reshape(1, -1), 1.0)


def _with_ones(v, fill=1.0):
    f = jnp.full(v.shape[:-1] + (HEAD_DIM,), fill, v.dtype)
    return jnp.concatenate([v[..., :HEAD_DIM], f, v[..., HEAD_DIM:], f], axis=-1)


def _proj_call(x, mod, w_packed, b_packed, tm):
    s, t, d = x.shape
    nt = t // tm
    row = lambda w: pl.BlockSpec((None, tm, w), lambda b, i: (b, i, 0))
    shp = lambda w, dt: jax.ShapeDtypeStruct((s, t, w), dt)
    outs = [(512, BF16), (512, F32), (512, F32), (512, BF16), (512, BF16),
            (512, BF16), (128, F32), (128, F32), (256, BF16), (256, BF16),
            (512, BF16), (IDX_DIM, F32), (128, BF16), (128, F32), (2048, F32)]
    return pl.pallas_call(
        _proj_kernel,
        grid=(s, nt),
        in_specs=[row(d),
                  pl.BlockSpec((None, 6, d), lambda b, i: (b, 0, 0)),
                  pl.BlockSpec((d, PACKED_COLS), lambda b, i: (0, 0)),
                  pl.BlockSpec((1, PACKED_COLS), lambda b, i: (0, 0))],
        out_specs=[row(w) for w, _ in outs],
        out_shape=[shp(w, dt) for w, dt in outs],
        compiler_params=_cparams(("arbitrary", "arbitrary")),
        name="in_proj",
    )(x, mod, w_packed, b_packed)


def _bias_kernel(tab_ref, o_ref, max_ref):
    r = lax.broadcasted_iota(I32, (DSA_Q_BLOCK, ATT_BLOCK), 0)
    c = lax.broadcasted_iota(I32, (DSA_Q_BLOCK, ATT_BLOCK), 1)
    half = N_BUCKETS // 2
    max_exact = half // 2
    for j, off in enumerate((ATT_BLOCK, 0, -ATT_BLOCK, -(1 << 20))):
        rel = c - r + off
        n = jnp.abs(rel)
        large = jnp.full_like(n, max_exact)
        for thr in (12, 16, 23, 32, 46, 64, 91):
            large = large + (n >= thr).astype(I32)
        bucket = jnp.where(rel > 0, half, 0) + jnp.where(n < max_exact, n, large)
        for h in range(DSA_HEADS):
            acc = jnp.zeros((DSA_Q_BLOCK, ATT_BLOCK), F32)
            for b in range(N_BUCKETS):
                acc = jnp.where(bucket == b, tab_ref[b, h], acc)
            o_ref[h, j] = acc
    for h in range(DSA_HEADS):
        top = tab_ref[0, h]
        for b in range(1, N_BUCKETS):
            top = jnp.maximum(top, tab_ref[b, h])
        max_ref[h] = jnp.full((8, LANES), top, F32)


def _bias_call(rel_bias):
    return pl.pallas_call(
        _bias_kernel,
        in_specs=[pl.BlockSpec(memory_space=pltpu.SMEM)],
        out_specs=[pl.BlockSpec(memory_space=pltpu.VMEM), pl.BlockSpec(memory_space=pltpu.VMEM)],
        out_shape=[jax.ShapeDtypeStruct((DSA_HEADS, 4, DSA_Q_BLOCK, ATT_BLOCK), F32),
                   jax.ShapeDtypeStruct((DSA_HEADS, 8, LANES), F32)],
        name="t5_bias_tiles",
    )(rel_bias)


def _softplus(z):
    return jnp.maximum(z, 0.0) + jnp.log1p(jnp.exp(-jnp.abs(z)))


def _sb_kernel(q_ref, k_ref, v_ref, o_ref, *, tq, last_fn):
    tk = SB_BLOCK
    last = last_fn(pl.program_id(2))
    lane = lax.broadcasted_iota(I32, (1, LANES), 1)
    low = lane < HEAD_DIM
    q = q_ref[...]
    zero = jnp.zeros_like(q)
    qh = (jnp.where(low, q, zero), jnp.where(low, zero, q))
    uj = lax.broadcasted_iota(I32, (tk, tk), 0)
    us = lax.broadcasted_iota(I32, (tk, tk), 1)
    u_mat = jnp.where(uj > us, 1.0, 0.0).astype(BF16)

    def tile(kb, vis, carries):
        start = pl.multiple_of(kb * tk, tk)
        kblk = k_ref[pl.ds(start, tk), :]
        vblk = v_ref[pl.ds(start, tk), :]
        pv = []
        new_carries = []
        for h in range(2):
            z = _dot_nt(qh[h], kblk)
            sp = _softplus(z)
            lk = -sp if vis is None else jnp.where(vis, -sp, 0.0)
            hi, lo = _split_bf16(lk)
            after = _dot(hi, u_mat) + _dot(lo, u_mat)
            w = jnp.exp((z - sp) + (after + carries[h]))
            if vis is not None:
                w = jnp.where(vis, w, 0.0)
            pv.append(_dot(w.astype(BF16), vblk))
            new_carries.append(carries[h] + jnp.sum(lk, axis=1, keepdims=True))
        return jnp.where(low, pv[0], pv[1]), new_carries

    r = lax.broadcasted_iota(I32, (tq, tk), 0)
    c = lax.broadcasted_iota(I32, (tq, tk), 1)
    zc = jnp.zeros((tq, 1), F32)
    acc, carries = tile(last, c < r, [zc, zc])
    has_prev = jnp.full((tq, tk), last > 0)
    pv, carries = tile(jnp.maximum(last - 1, 0), has_prev, carries)
    acc = acc + pv

    def live(carries):
        return jnp.max(jnp.maximum(carries[0], carries[1])) > EXP_ZERO_BELOW

    def cond(st):
        kb, go, _, _, _ = st
        return jnp.logical_and(kb >= 0, go)

    def body(st):
        kb, _, acc, c0, c1 = st
        pv, nc = tile(kb, None, [c0, c1])
        return kb - 1, live(nc), acc + pv, nc[0], nc[1]

    st = lax.while_loop(cond, body, (last - 2, live(carries), acc, carries[0], carries[1]))
    o_ref[...] = st[2].astype(o_ref.dtype)


def _sb_call(q, k, v, tq, last_fn):
    s, t, _ = q.shape
    tkk = k.shape[1]
    return pl.pallas_call(
        functools.partial(_sb_kernel, tq=tq, last_fn=last_fn),
        grid=(s, SB_HEADS // 2, t // tq),
        in_specs=[pl.BlockSpec((None, tq, LANES), lambda b, p, i: (b, i, p)),
                  pl.BlockSpec((None, tkk, LANES), lambda b, p, i: (b, 0, p)),
                  pl.BlockSpec((None, tkk, LANES), lambda b, p, i: (b, 0, p))],
        out_specs=pl.BlockSpec((None, tq, LANES), lambda b, p, i: (b, i, p)),
        out_shape=jax.ShapeDtypeStruct(q.shape, BF16),
        compiler_params=_cparams(("arbitrary", "arbitrary", "arbitrary")),
        name="stick_breaking",
    )(q, k, v)


def _dsa_kernel(qi_ref, wi_ref, qb_ref, ki_ref, kb_ref, vx_ref, bt_ref, bmax_ref, o_ref,
                qis_sc, wb_sc, qbs_sc, key_sc, smax_sc, mx_sc, acc_sc, kn_sc, thr_sc, *, tq, topk, diag_fn, adm_fn):
    tk = ATT_BLOCK
    group = DSA_HEADS // DSA_KV_HEADS

    @pl.when(pl.program_id(1) == 0)
    def _():
        for n in range(DSA_KV_HEADS):
            def body(pi, best):
                start = pl.multiple_of(pi * (2 * tk), 2 * tk)
                x = kb_ref[pl.ds(start, 2 * tk), n * LANES:(n + 1) * LANES].astype(F32)
                return jnp.maximum(best, jnp.sum(x * x, axis=1, keepdims=True))
            best = lax.fori_loop(0, kb_ref.shape[0] // (2 * tk), body, jnp.zeros((2 * tk, 1), F32))
            kn_sc[n] = jnp.broadcast_to(jnp.sqrt(0.5 * jnp.max(best, axis=0, keepdims=True)), (8, LANES))

    diag = diag_fn(pl.program_id(1))
    last = diag + (tq + tk - 1) // tk - 1
    lane = lax.broadcasted_iota(I32, (1, LANES), 1)
    low = lane < HEAD_DIM

    for h in range(IDX_HEADS):
        p = h // 2
        t = qi_ref[:, p * LANES:(p + 1) * LANES]
        z = jnp.zeros_like(t)
        qis_sc[h * tq:(h + 1) * tq, :] = jnp.where(low, t, z) if h % 2 == 0 else jnp.where(low, z, t)
        t = qb_ref[:, p * LANES:(p + 1) * LANES]
        g = h % group
        qbs_sc[h // group, g * tq:(g + 1) * tq, :] = jnp.where(low, t, z) if h % 2 == 0 else jnp.where(low, z, t)
        wb_sc[h] = jnp.broadcast_to(wi_ref[:, h:h + 1], (tq, LANES))

    r = lax.broadcasted_iota(I32, (tq, tk), 0)
    c = lax.broadcasted_iota(I32, (tq, tk), 1)
    last_pair = last // 2
    smax_sc[...] = jnp.full((tq, tk), -jnp.inf, F32)

    def score_pair(pi, masked):
        start = pl.multiple_of(pi * (2 * tk), 2 * tk)
        d = _dot_nt(qis_sc[...], ki_ref[pl.ds(start, 2 * tk), :])
        for half in range(2):
            s = jnp.zeros((tq, tk), F32)
            for h in range(IDX_HEADS):
                s = s + wb_sc[h] * jnp.maximum(d[h * tq:(h + 1) * tq, half * tk:(half + 1) * tk], 0.0)
            bits = pltpu.bitcast(s, I32)
            key = jnp.where(bits < 0, bits ^ INT32_MAX, bits)
            if masked:
                adm = adm_fn(pl.program_id(1), r, c + (start + half * tk))
                key = jnp.where(adm, key, NEG_INF_KEY)
                s = jnp.where(adm, s, -jnp.inf)
            key_sc[2 * pi + half] = key
            smax_sc[...] = jnp.maximum(smax_sc[...], s)

    def score_body(pi, carry):
        score_pair(pi, False)
        return carry

    lax.fori_loop(0, last_pair, score_body, 0)
    score_pair(last_pair, True)

    rc = min(tq, ATT_BLOCK)
    ones_mat = jnp.ones((tk, tk), BF16)

    def row_total(x):
        return _dot(x.astype(BF16), ones_mat)

    def count_ge(*thrs):
        for j, t in enumerate(thrs):
            if hasattr(t, "shape"):
                thr_sc[j] = t
        chunks = list(range(0, tq, rc))

        def body(pi, accs):
            out = []
            for ci, r0 in enumerate(chunks):
                k0 = key_sc[2 * pi, r0:r0 + rc, :]
                k1 = key_sc[2 * pi + 1, r0:r0 + rc, :]
                for j, t in enumerate(thrs):
                    tt = thr_sc[j, r0:r0 + rc, :] if hasattr(t, "shape") else t
                    out.append(accs[ci * len(thrs) + j]
                               + (jnp.where(k0 >= tt, 1.0, 0.0) + jnp.where(k1 >= tt, 1.0, 0.0)))
            return tuple(out)

        accs = lax.fori_loop(0, last_pair + 1, body,
                             tuple(jnp.zeros((rc, tk), F32) for _ in range(len(chunks) * len(thrs))))
        totals = []
        for j in range(len(thrs)):
            parts = [row_total(accs[ci * len(thrs) + j]) for ci in range(len(chunks))]
            totals.append(parts[0] if len(parts) == 1 else jnp.concatenate(parts, axis=0))
        return totals

    def midpoint(lo, hi):
        return (lo >> 1) + (hi >> 1) + (lo & hi & 1)

    def float_to_key(v):
        bits = pltpu.bitcast(v, I32)
        return jnp.where(bits < 0, bits ^ INT32_MAX, bits)

    kf = float(topk)

    def bis_cond(st):
        return st[0]

    def is_open(piv, lo, cnt_lo):
        return jnp.logical_and(piv != lo, cnt_lo != kf)

    def bis_body(st):
        _, lo, hi, cnt_lo = st
        for _ in range(2):
            piv = midpoint(lo, hi)
            open_ = is_open(piv, lo, cnt_lo)
            cnt, = count_ge(piv)
            ge = cnt >= kf
            up = jnp.logical_and(open_, ge)
            lo = jnp.where(up, piv, lo)
            cnt_lo = jnp.where(up, cnt, cnt_lo)
            hi = jnp.where(jnp.logical_and(open_, jnp.logical_not(ge)), piv, hi)
        go = jnp.max(jnp.where(is_open(midpoint(lo, hi), lo, cnt_lo), 1.0, 0.0)) > 0.0
        return go, lo, hi, cnt_lo

    c_zero, c_pos = count_ge(0, 1)
    pos = c_pos >= kf
    zer = jnp.logical_and(c_zero >= kf, jnp.logical_not(pos))
    hi_max = jnp.broadcast_to(float_to_key(jnp.max(smax_sc[...], axis=1, keepdims=True)) + 1, (tq, tk))
    lo0 = jnp.where(pos, 1, jnp.where(zer, 0, NEG_INF_KEY))
    hi0 = jnp.where(pos, hi_max, jnp.where(zer, 1, 0))
    cnt_lo0 = jnp.where(pos, c_pos, jnp.where(zer, c_zero, -1.0))
    go0 = jnp.max(jnp.where(is_open(midpoint(lo0, hi0), lo0, cnt_lo0), 1.0, 0.0)) > 0.0
    _, thr, _, cnt_lo = lax.while_loop(bis_cond, bis_body, (go0, lo0, hi0, cnt_lo0))

    tied = cnt_lo > kf

    @pl.when(jnp.max(jnp.where(tied, 1.0, 0.0)) > 0.0)
    def _():
        above, = count_ge(thr + 1)
        need = kf - above
        prefix_mat = jnp.where(lax.broadcasted_iota(I32, (tk, tk), 0) <= lax.broadcasted_iota(I32, (tk, tk), 1),
                               1.0, 0.0).astype(BF16)

        def demote(kb, before):
            key = key_sc[kb]
            eq = jnp.logical_and(tied, key == thr)
            ind = jnp.where(eq, 1.0, 0.0)
            rank = _dot(ind.astype(BF16), prefix_mat) + before
            key_sc[kb] = jnp.where(jnp.logical_and(eq, rank > need), NEG_INF_KEY, key)
            return before + row_total(ind)

        lax.fori_loop(0, last + 1, demote, jnp.zeros((tq, tk), F32))

    thr_eff = jnp.maximum(thr, NEG_INF_KEY + 1)

    def masked_logits(pi, n):
        start = pl.multiple_of(pi * (2 * tk), 2 * tk)
        lg = _dot_nt(qbs_sc[n], kb_ref[pl.ds(start, 2 * tk), n * LANES:(n + 1) * LANES])
        sel = [key_sc[2 * pi + half] >= thr_eff for half in range(2)]
        j = [jnp.clip(diag - (2 * pi + half) + 1, 0, 3) for half in range(2)]
        out = []
        for g in range(group):
            h = n * group + g
            out.append([jnp.where(sel[half],
                                  lg[g * tq:(g + 1) * tq, half * tk:(half + 1) * tk] + bt_ref[h, j[half], 0:tq, :],
                                  MASKED_LOGIT) for half in range(2)])
        return out

    def max_body(pi, carry):
        for n in range(DSA_KV_HEADS):
            for g, (la, lb) in enumerate(masked_logits(pi, n)):
                h = n * group + g
                mx_sc[h] = jnp.maximum(mx_sc[h], jnp.maximum(la, lb))
        return carry

    def exact_row_maxima():
        for h in range(DSA_HEADS):
            mx_sc[h] = jnp.full((tq, LANES), MASKED_LOGIT, F32)
        lax.fori_loop(0, last_pair + 1, max_body, 0)
        for h in range(DSA_HEADS):
            mx_sc[h] = jnp.broadcast_to(jnp.max(mx_sc[h], axis=1, keepdims=True), (tq, LANES))

    def bounded_row_maxima():
        for h in range(DSA_HEADS):
            n, g = h // group, h % group
            q = qbs_sc[n, g * tq:(g + 1) * tq, :].astype(F32)
            q_norm = jnp.sqrt(jnp.sum(q * q, axis=1, keepdims=True))
            mx_sc[h] = q_norm * kn_sc[n, 0:1, :] + bmax_ref[h, 0:1, :]

    def pv_body(pi, carry):
        start = pl.multiple_of(pi * (2 * tk), 2 * tk)
        for n in range(DSA_KV_HEADS):
            ps = []
            for g, (la, lb) in enumerate(masked_logits(pi, n)):
                m = mx_sc[n * group + g]
                ps.append(jnp.concatenate([jnp.exp(la - m).astype(BF16), jnp.exp(lb - m).astype(BF16)], axis=1))
            pv = _dot(jnp.concatenate(ps, axis=0), vx_ref[pl.ds(start, 2 * tk), n * LANES:(n + 1) * LANES])
            for g in range(group):
                h = n * group + g
                acc_sc[h] = acc_sc[h] + pv[g * tq:(g + 1) * tq]
        return carry

    def weights_sweep():
        for h in range(DSA_HEADS):
            acc_sc[h] = jnp.zeros((tq, LANES), F32)
        lax.fori_loop(0, last_pair + 1, pv_body, 0)

    bounded_row_maxima()
    weights_sweep()
    den = acc_sc[0][:, HEAD_DIM:HEAD_DIM + 1]
    for h in range(1, DSA_HEADS):
        den = jnp.minimum(den, acc_sc[h][:, HEAD_DIM:HEAD_DIM + 1])
    healthy = jnp.min(jnp.where(den >= SOFTMAX_DEN_FLOOR, 1.0, 0.0)) > 0.0

    @pl.when(jnp.logical_not(healthy))
    def _():
        exact_row_maxima()
        weights_sweep()

    def normalised(h):
        a = acc_sc[h]
        return a / pltpu.roll(a, HEAD_DIM, axis=1)

    for p in range(DSA_HEADS // 2):
        o1 = pltpu.roll(normalised(2 * p + 1), HEAD_DIM, axis=1)
        o_ref[:, p * LANES:(p + 1) * LANES] = jnp.where(low, normalised(2 * p), o1).astype(o_ref.dtype)


def _dsa_call(qi, wi, qb, kid, kbd, vx, btiles, bmax, tq, topk, diag_fn, adm_fn):
    s, t, _ = qi.shape
    tkk = kid.shape[1]
    assert tkk % (2 * ATT_BLOCK) == 0, "keys must come in whole pairs of blocks"
    nkb = tkk // ATT_BLOCK
    rowq = lambda w: pl.BlockSpec((None, tq, w), lambda b, i: (b, i, 0))
    full = lambda w: pl.BlockSpec((None, tkk, w), lambda b, i: (b, 0, 0))
    return pl.pallas_call(
        functools.partial(_dsa_kernel, tq=tq, topk=topk, diag_fn=diag_fn, adm_fn=adm_fn),
        grid=(s, t // tq),
        in_specs=[rowq(512), rowq(LANES), rowq(512), full(LANES), full(2 * LANES), full(2 * LANES),
                  pl.BlockSpec(btiles.shape, lambda b, i: (0, 0, 0, 0)),
                  pl.BlockSpec(bmax.shape, lambda b, i: (0, 0, 0))],
        out_specs=rowq(512),
        out_shape=jax.ShapeDtypeStruct((s, t, 512), BF16),
        scratch_shapes=[pltpu.VMEM((IDX_HEADS * tq, LANES), BF16),
                        pltpu.VMEM((IDX_HEADS, tq, LANES), F32),
                        pltpu.VMEM((DSA_KV_HEADS, DSA_HEADS // DSA_KV_HEADS * tq, LANES), BF16),
                        pltpu.VMEM((nkb, tq, ATT_BLOCK), I32),
                        pltpu.VMEM((tq, ATT_BLOCK), F32),
                        pltpu.VMEM((DSA_HEADS, tq, LANES), F32),
                        pltpu.VMEM((DSA_HEADS, tq, LANES), F32),
                        pltpu.VMEM((DSA_KV_HEADS, 8, LANES), F32),
                        pltpu.VMEM((2, tq, ATT_BLOCK), I32)],
        compiler_params=_cparams(("arbitrary", "arbitrary")),
        name="dsa_attention",
    )(qi, wi, qb, kid, kbd, vx, btiles, bmax)


def _layer_norm(x, g, b):
    mu = jnp.mean(x, axis=-1, keepdims=True)
    xc = x - mu
    var = jnp.mean(xc * xc, axis=-1, keepdims=True)
    return xc * lax.rsqrt(var + LN_EPS) * g + b


def _post_kernel(oa_ref, ob_ref, sg_ref, x_ref, mod_ref, wsb_ref, wdsa_ref, wout_ref, g_ref, b_ref,
                 wrh_ref, wrl_ref, br_ref, x1_ref, h2_ref, ti_ref, tg_ref):
    tm = x_ref.shape[0]
    sub = min(tm, POST_SUB_ROWS)
    for r0 in range(0, tm, sub):
        _post_rows(slice(r0, r0 + sub), oa_ref, ob_ref, sg_ref, x_ref, mod_ref, wsb_ref, wdsa_ref, wout_ref,
                   g_ref, b_ref, wrh_ref, wrl_ref, br_ref, x1_ref, h2_ref, ti_ref, tg_ref)


def _post_rows(rows, oa_ref, ob_ref, sg_ref, x_ref, mod_ref, wsb_ref, wdsa_ref, wout_ref, g_ref, b_ref,
               wrh_ref, wrl_ref, br_ref, x1_ref, h2_ref, ti_ref, tg_ref):
    d = D_MODEL
    ya = _dot(oa_ref[rows, :], wsb_ref[...])
    yb = _dot(ob_ref[rows, :], wdsa_ref[...])
    merged = sg_ref[rows, :d] * ya + sg_ref[rows, d:] * yb
    mix = _dot(merged.astype(BF16), wout_ref[...])
    g1 = mod_ref[2:3, :]
    x1 = _layer_norm(DEEPNORM_ALPHA * x_ref[rows, :] + g1 * mix, g_ref[...], b_ref[...])
    x1_ref[rows, :] = x1
    h2 = x1 * (1.0 + mod_ref[4:5, :]) + mod_ref[3:4, :]
    h2_ref[rows, :] = h2
    logits = _dot3(h2, wrh_ref[...], wrl_ref[...]) + br_ref[...]
    lane = lax.broadcasted_iota(I32, logits.shape, 1).astype(F32)
    neg = -jnp.inf
    cur = jnp.where(lane < N_EXPERTS, logits, neg)
    vals, idxs = [], []
    for _ in range(TOP_K_EXPERTS):
        m = jnp.max(cur, axis=1, keepdims=True)
        idx = jnp.min(jnp.where(cur == m, lane, float(LANES)), axis=1, keepdims=True)
        vals.append(m)
        idxs.append(idx)
        cur = jnp.where(lane == idx, neg, cur)
    es = [jnp.exp(v - vals[0]) for v in vals]
    tot = es[0] + es[1] + es[2] + es[3]
    ti = jnp.zeros(logits.shape, F32)
    tg = jnp.zeros(logits.shape, F32)
    for k in range(TOP_K_EXPERTS):
        ti = jnp.where(lane == k, idxs[k], ti)
        tg = jnp.where(lane == k, es[k] / tot, tg)
    ti_ref[rows, :] = ti.astype(I32)
    tg_ref[rows, :] = tg


def _post_call(oa, ob, sg, x, mod, wsb, wdsa, wout, ln_g, ln_b, wr_hi, wr_lo, br, tm):
    s, t, d = x.shape
    row = lambda w: pl.BlockSpec((None, tm, w), lambda b, i: (b, i, 0))
    const = lambda a: pl.BlockSpec(a.shape, lambda b, i: (0,) * a.ndim)
    shp = lambda w, dt: jax.ShapeDtypeStruct((s, t, w), dt)
    return pl.pallas_call(
        _post_kernel,
        grid=(s, t // tm),
        in_specs=[row(512), row(512), row(2 * d), row(d),
                  pl.BlockSpec((None, 6, d), lambda b, i: (b, 0, 0)),
                  const(wsb), const(wdsa), const(wout), const(ln_g), const(ln_b),
                  const(wr_hi), const(wr_lo), const(br)],
        out_specs=[row(d), row(d), row(LANES), row(LANES)],
        out_shape=[shp(d, F32), shp(d, F32), shp(LANES, I32), shp(LANES, F32)],
        compiler_params=_cparams(("arbitrary", "arbitrary")),
        name="post_attention_router",
    )(oa, ob, sg, x, mod, wsb, wdsa, wout, ln_g, ln_b, wr_hi, wr_lo, br)


def _dispatch_kernel(zoff_ref, nused_ref, dest_ref, h_ref, xs_hbm, buf, zbuf, sem, zsem, *, tm, tile, n_tiles):
    i = pl.program_id(0)
    nb = pl.num_programs(0)
    k = TOP_K_EXPERTS
    slot = i % 2

    def row_copy(dst, slot, r):
        return pltpu.make_async_copy(buf.at[slot, pl.ds(r, 1), :], xs_hbm.at[pl.ds(dst, 1), :], sem.at[slot])

    def wait_all(slot):
        def body(r, carry):
            for _ in range(k):
                row_copy(0, slot, r).wait()
            return carry
        lax.fori_loop(0, tm, body, 0, unroll=4)

    def zero_fill(start):
        return pltpu.make_async_copy(zbuf, xs_hbm.at[pl.ds(start, tile), :], zsem)

    @pl.when(i == 0)
    def _():
        zbuf[...] = jnp.zeros_like(zbuf)
        fills = [(zoff_ref[e] >= 0, pl.multiple_of(jnp.maximum(zoff_ref[e], 0), MOE_BLOCK)) for e in range(N_EXPERTS)]
        fills += [(t >= nused_ref[0], t * tile) for t in range(max(n_tiles - N_EXPERTS - 1, 0), n_tiles)]
        for on, start in fills:
            @pl.when(on)
            def _():
                zero_fill(start).start()
        for on, start in fills:
            @pl.when(on)
            def _():
                zero_fill(start).wait()

    @pl.when(i >= 2)
    def _():
        wait_all(slot)

    buf[slot] = h_ref[...]
    for r in range(tm):
        for j in range(k):
            row_copy(dest_ref[0, r * k + j], slot, r).start()

    @pl.when(i == nb - 1)
    def _():
        wait_all(slot)

        @pl.when(nb >= 2)
        def _():
            wait_all(1 - slot)


def _dispatch_call(zoff, nused, dest, h2, n_rows, tile):
    n, d = h2.shape
    tm = min(MOE_BLOCK, n)
    nb = n // tm
    dest3 = dest.reshape(nb, 1, tm * TOP_K_EXPERTS)
    grid_spec = pltpu.PrefetchScalarGridSpec(
        num_scalar_prefetch=2,
        grid=(nb,),
        in_specs=[pl.BlockSpec((None, 1, tm * TOP_K_EXPERTS), lambda i, zo, nu: (i, 0, 0), memory_space=pltpu.SMEM),
                  pl.BlockSpec((tm, d), lambda i, zo, nu: (i, 0))],
        out_specs=pl.BlockSpec(memory_space=pl.ANY),
        scratch_shapes=[pltpu.VMEM((2, tm, d), F32), pltpu.VMEM((tile, d), F32),
                        pltpu.SemaphoreType.DMA((2,)), pltpu.SemaphoreType.DMA(())],
    )
    return pl.pallas_call(
        functools.partial(_dispatch_kernel, tm=tm, tile=tile, n_tiles=n_rows // tile),
        grid_spec=grid_spec,
        out_shape=jax.ShapeDtypeStruct((n_rows, d), F32),
        compiler_params=_cparams(("arbitrary",)),
        name="moe_dispatch",
    )(zoff, nused, dest3, h2)


def _ffn_kernel(be_ref, x_ref, wup_ref, bup_ref, wdn_ref, bdn_ref, o_ref):
    u = _dot(x_ref[...].astype(BF16), wup_ref[...]) + bup_ref[...]
    acts = []
    for t in range(2 * D_FF // SWIGLU_TILE):
        a = t * SWIGLU_TILE
        glu = jnp.minimum(u[:, a:a + LANES], SWIGLU_LIMIT)
        lin = jnp.clip(u[:, a + LANES:a + SWIGLU_TILE], -SWIGLU_LIMIT, SWIGLU_LIMIT)
        acts.append((glu * jax.nn.sigmoid(SWIGLU_ALPHA * glu) * (lin + 1.0)).astype(BF16))
    o_ref[...] = _dot(jnp.concatenate(acts, axis=1), wdn_ref[...]) + bdn_ref[...]


def _deinterleave_kernel(w_ref, o_ref):
    j = lax.broadcasted_iota(I32, (SWIGLU_TILE, SWIGLU_TILE), 0)
    s = lax.broadcasted_iota(I32, (SWIGLU_TILE, SWIGLU_TILE), 1)
    src = jnp.where(s < LANES, 2 * s, 2 * (s - LANES) + 1)
    perm = jnp.where(j == src, 1.0, 0.0).astype(BF16)
    for t in range(w_ref.shape[1] // SWIGLU_TILE):
        a = t * SWIGLU_TILE
        o_ref[:, a:a + SWIGLU_TILE] = _dot(w_ref[:, a:a + SWIGLU_TILE].astype(BF16), perm).astype(BF16)


def _deinterleave_call(w_up):
    e, d, f = w_up.shape
    tr = 512
    return pl.pallas_call(
        _deinterleave_kernel,
        grid=(e, d // tr),
        in_specs=[pl.BlockSpec((None, tr, f), lambda a, b: (a, b, 0))],
        out_specs=pl.BlockSpec((None, tr, f), lambda a, b: (a, b, 0)),
        out_shape=jax.ShapeDtypeStruct((e, d, f), BF16),
        compiler_params=_cparams(("arbitrary", "arbitrary")),
        name="w_up_tiles",
    )(w_up)


def _ffn_call(block_expert, xs, wup, bup, wdn, bdn, tile):
    n_rows, d = xs.shape
    n_blocks = n_rows // tile
    grid_spec = pltpu.PrefetchScalarGridSpec(
        num_scalar_prefetch=1,
        grid=(n_blocks,),
        in_specs=[pl.BlockSpec((tile, d), lambda i, be: (i, 0)),
                  pl.BlockSpec((None, d, 2 * D_FF), lambda i, be: (be[i], 0, 0)),
                  pl.BlockSpec((None, 1, 2 * D_FF), lambda i, be: (be[i], 0, 0)),
                  pl.BlockSpec((None, D_FF, d), lambda i, be: (be[i], 0, 0)),
                  pl.BlockSpec((None, 1, d), lambda i, be: (be[i], 0, 0))],
        out_specs=pl.BlockSpec((tile, d), lambda i, be: (i, 0)),
    )
    return pl.pallas_call(
        _ffn_kernel,
        grid_spec=grid_spec,
        out_shape=jax.ShapeDtypeStruct((n_rows, d), F32),
        compiler_params=_cparams(("arbitrary",)),
        name="expert_ffn",
    )(block_expert, xs, wup, bup, wdn, bdn)


def _combine_kernel(pos0_ref, posn_ref, rows_hbm, x1_ref, tg_ref, mod_ref, g_ref, b_ref, o_ref, buf, sem, *, tm):
    i = pl.program_id(0)
    nb = pl.num_programs(0)
    k = TOP_K_EXPERTS

    def row_copy(src, slot, r, j):
        return pltpu.make_async_copy(rows_hbm.at[pl.ds(src, 1), :], buf.at[slot, j, pl.ds(r, 1), :], sem.at[slot])

    def issue(pos_ref, slot):
        def body(r, carry):
            for j in range(k):
                row_copy(pos_ref[0, r * k + j], slot, r, j).start()
            return carry
        lax.fori_loop(0, tm, body, 0, unroll=4)

    slot = i % 2

    @pl.when(i == 0)
    def _():
        issue(pos0_ref, 0)

    @pl.when(i + 1 < nb)
    def _():
        issue(posn_ref, 1 - slot)

    def wait_body(r, carry):
        for j in range(k):
            row_copy(0, slot, r, j).wait()
        return carry
    lax.fori_loop(0, tm, wait_body, 0, unroll=4)

    gated = [buf[slot, j] * tg_ref[:, j:j + 1] for j in range(k)]
    y = (gated[0] + gated[1]) + (gated[2] + gated[3])
    o_ref[...] = _layer_norm(DEEPNORM_ALPHA * x1_ref[...] + mod_ref[5:6, :] * y, g_ref[...], b_ref[...])


def _combine_call(pos, rows, x1, tg, mod, seq_len, ln_g, ln_b, tm):
    n, d = x1.shape
    nb = n // tm
    per_seq = seq_len // tm
    pos3 = pos.reshape(nb, 1, tm * TOP_K_EXPERTS)
    return pl.pallas_call(
        functools.partial(_combine_kernel, tm=tm),
        grid=(nb,),
        in_specs=[pl.BlockSpec((None, 1, tm * TOP_K_EXPERTS), lambda i: (0, 0, 0), memory_space=pltpu.SMEM),
                  pl.BlockSpec((None, 1, tm * TOP_K_EXPERTS), lambda i: (jnp.minimum(i + 1, nb - 1), 0, 0),
                               memory_space=pltpu.SMEM),
                  pl.BlockSpec(memory_space=pl.ANY),
                  pl.BlockSpec((tm, d), lambda i: (i, 0)),
                  pl.BlockSpec((tm, LANES), lambda i: (i, 0)),
                  pl.BlockSpec((None, 6, d), lambda i: (i // per_seq, 0, 0)),
                  pl.BlockSpec((1, d), lambda i: (0, 0)),
                  pl.BlockSpec((1, d), lambda i: (0, 0))],
        out_specs=pl.BlockSpec((tm, d), lambda i: (i, 0)),
        out_shape=jax.ShapeDtypeStruct((n, d), F32),
        scratch_shapes=[pltpu.VMEM((2, TOP_K_EXPERTS, tm, d), F32), pltpu.SemaphoreType.DMA((2,))],
        compiler_params=_cparams(("arbitrary",)),
        name="moe_combine_ln2",
    )(pos3, pos3, rows, x1, tg, mod, ln_g, ln_b)


def _rank_kernel(ti_ref, rank_ref, cnt_ref, carry_sc):
    tb = ti_ref.shape[0]

    @pl.when(pl.program_id(0) == 0)
    def _():
        carry_sc[...] = jnp.zeros_like(carry_sc)

    lane = lax.broadcasted_iota(I32, (tb, LANES), 1)
    ti = ti_ref[...]
    hits = [ti[:, k:k + 1] == lane for k in range(TOP_K_EXPERTS)]
    tot = jnp.zeros((tb, LANES), F32)
    for hit in hits:
        tot = tot + jnp.where(hit, 1.0, 0.0)
    earlier = jnp.where(lax.broadcasted_iota(I32, (tb, tb), 1) < lax.broadcasted_iota(I32, (tb, tb), 0),
                        1.0, 0.0).astype(BF16)
    before = _dot(earlier, tot.astype(BF16)) + carry_sc[0:1, :]
    out = jnp.zeros((tb, LANES), F32)
    for k, hit in enumerate(hits):
        out = jnp.where(lane == k, jnp.sum(jnp.where(hit, before, 0.0), axis=1, keepdims=True), out)
    rank_ref[...] = out.astype(I32)
    carry_sc[...] = carry_sc[...] + jnp.sum(tot, axis=0, keepdims=True)
    cnt_ref[...] = carry_sc[...]


def _rank_call(ti):
    n = ti.shape[0]
    tb = min(512, n)
    return pl.pallas_call(
        _rank_kernel,
        grid=(n // tb,),
        in_specs=[pl.BlockSpec((tb, LANES), lambda i: (i, 0))],
        out_specs=[pl.BlockSpec((tb, LANES), lambda i: (i, 0)), pl.BlockSpec((8, LANES), lambda i: (0, 0))],
        out_shape=[jax.ShapeDtypeStruct((n, LANES), I32), jax.ShapeDtypeStruct((8, LANES), F32)],
        scratch_shapes=[pltpu.VMEM((8, LANES), F32)],
        compiler_params=_cparams(("arbitrary",)),
        name="moe_rank",
    )(ti)


def _routing(ti, tile):
    n_tok = ti.shape[0]
    n_assign = n_tok * TOP_K_EXPERTS
    e_flat = ti[:, :TOP_K_EXPERTS].reshape(-1)
    rank_tile, cnt = _rank_call(ti)
    rank = rank_tile[:, :TOP_K_EXPERTS].reshape(-1)
    counts = cnt[0, :N_EXPERTS].astype(I32)
    padded = (counts + tile - 1) // tile * tile
    pend = jnp.cumsum(padded)
    pstart = pend - padded
    dest = (pstart[e_flat] + rank).astype(I32)
    n_rows = (n_assign + N_EXPERTS * (tile - 1) + tile - 1) // tile * tile
    n_blocks = n_rows // tile
    tile_start = jnp.arange(n_blocks, dtype=I32) * tile
    block_expert = jnp.minimum(jnp.sum((pend[None, :] <= tile_start[:, None]).astype(I32), axis=1),
                               N_EXPERTS - 1).astype(I32)
    last_tile = jnp.where(padded > 0, pend - tile, -1).astype(I32)
    nused = (pend[-1] // tile).astype(I32).reshape(1)
    return dest, block_expert, last_tile, nused, n_rows


def _stream(x, mod, kv_cache, weights, btiles, tm, tq):
    (w_packed, b_packed, wsb, wdsa, wout, ln1_g, ln1_b, wr_hi, wr_lo, br,
     wup, bup, wdn, bdn, ln2_g, ln2_b) = weights
    s, t, d = x.shape
    (qa, ka32, va32, ka16, va16, qb, kb32, vb32, kbd, vx, qi, ki32, kid, wi, sg) = _proj_call(x, mod, w_packed, b_packed, tm)

    if kv_cache is None:
        k_sb, v_sb, k_id, k_bd, v_x = ka16, va16, kid, kbd, vx
        total = t
        assert tq % ATT_BLOCK == 0
        diag_fn = lambda i: i * (tq // ATT_BLOCK)
        adm_fn = lambda i, r, key_pos: key_pos < i * tq + (r // CHUNK + 1) * CHUNK
    else:
        past = kv_cache[0].shape[1]
        total = past + t
        pad = (-total) % (2 * ATT_BLOCK)
        cat = lambda cache, new: jnp.concatenate(
            [cache, new, jnp.zeros((s, pad, new.shape[2]), new.dtype)], axis=1)
        k_sb, v_sb, k_id, k_bd, v_x = [cat(c_, n_) for c_, n_ in zip(kv_cache, (ka16, va16, kid, kbd, vx))]
        assert past % ATT_BLOCK == 0 and t <= ATT_BLOCK
        diag_fn = lambda i: i * 0 + past // ATT_BLOCK
        adm_fn = lambda i, r, key_pos: key_pos < total
    topk = max(1, min(TOPK_MAX, total // 4))

    if kv_cache is None:
        oa = _sb_call(qa, k_sb, v_sb, min(SB_BLOCK, t), lambda i: i)
    else:
        assert kv_cache[0].shape[1] % SB_BLOCK == 0 and t <= SB_BLOCK
        oa = _sb_call(qa, k_sb, v_sb, t, lambda i: i * 0 + kv_cache[0].shape[1] // SB_BLOCK)
    ob = _dsa_call(qi, wi, qb, k_id, k_bd, v_x, *btiles, tq, topk, diag_fn, adm_fn)
    x1, h2, ti, tg = _post_call(oa, ob, sg, x, mod, wsb, wdsa, wout, ln1_g, ln1_b, wr_hi, wr_lo, br,
                                min(2 * POST_SUB_ROWS, t))

    n = s * t
    tile = 4 * MOE_BLOCK if n * TOP_K_EXPERTS >= N_EXPERTS * 8 * MOE_BLOCK else MOE_BLOCK
    dest, block_expert, last_tile, nused, n_rows = _routing(ti.reshape(n, LANES), tile)
    xs = _dispatch_call(last_tile, nused, dest, h2.reshape(n, d), n_rows, tile)
    rows = _ffn_call(block_expert, xs, wup, bup, wdn, bdn, tile)
    y = _combine_call(dest, rows, x1.reshape(n, d), tg.reshape(n, LANES), mod, t, ln2_g, ln2_b, min(128, t))
    new_rows = (ka32.reshape(1, s, t, SB_HEADS, HEAD_DIM), va32.reshape(1, s, t, SB_HEADS, HEAD_DIM),
                kb32.reshape(1, s, t, DSA_KV_HEADS, HEAD_DIM), vb32.reshape(1, s, t, DSA_KV_HEADS, HEAD_DIM),
                ki32.reshape(1, s, t, IDX_DIM))
    return y.reshape(s, t, d), new_rows


def kernel(x_prompt, x_sample, cache_sb_k, cache_sb_v, cache_dsa_k, cache_dsa_v, cache_idx_k, c_prompt, c_sample, rel_bias, w_ada, b_ada, w_in, b_in, w_o_sb, w_o_dsa, w_out, ln1_g, ln1_b, w_router, b_router, w_up, b_up, w_down, b_down, ln2_g, ln2_b):
    d = D_MODEL
    nb, ns = x_prompt.shape[0], x_sample.shape[0]
    past = cache_sb_k.shape[2]

    mod = _mod_call(jnp.concatenate([c_prompt, c_sample], axis=0), w_ada[0], b_ada[0]).reshape(nb + ns, 6, d)
    btiles = _bias_call(rel_bias)

    w_packed, b_packed = _pack_w_in(w_in[0], b_in[0])
    wr = jnp.concatenate([w_router[0], jnp.zeros((d, LANES - N_EXPERTS), F32)], axis=1)
    wr_hi = wr.astype(BF16)
    wr_lo = (wr - wr_hi.astype(F32)).astype(BF16)
    br = jnp.concatenate([b_router[0], jnp.zeros((LANES - N_EXPERTS,), F32)]).reshape(1, LANES)
    wup = _deinterleave_call(w_up[0])
    bup = b_up[0].reshape(N_EXPERTS, 2 * D_FF // SWIGLU_TILE, LANES, 2).swapaxes(2, 3).reshape(N_EXPERTS, 1, 2 * D_FF)
    weights = (w_packed, b_packed, w_o_sb[0].astype(BF16), w_o_dsa[0].astype(BF16), w_out[0].astype(BF16),
               ln1_g[0].reshape(1, d), ln1_b[0].reshape(1, d), wr_hi, wr_lo, br,
               wup, bup, w_down[0].astype(BF16), b_down[0].reshape(N_EXPERTS, 1, d),
               ln2_g[0].reshape(1, d), ln2_b[0].reshape(1, d))

    dup = lambda a: jnp.concatenate([a[..., :64], a[..., :64], a[..., 64:], a[..., 64:]], axis=-1)
    idx_c = cache_idx_k[0]
    caches = (cache_sb_k[0].reshape(ns, past, SB_HEADS * HEAD_DIM).astype(BF16),
              cache_sb_v[0].reshape(ns, past, SB_HEADS * HEAD_DIM).astype(BF16),
              jnp.concatenate([idx_c, idx_c], axis=-1).astype(BF16),
              dup(cache_dsa_k[0].reshape(ns, past, DSA_KV_HEADS * HEAD_DIM)).astype(BF16),
              _with_ones(cache_dsa_v[0].reshape(ns, past, DSA_KV_HEADS * HEAD_DIM)).astype(BF16))

    t_p, t_s = x_prompt.shape[1], x_sample.shape[1]
    y_p, new_p = _stream(x_prompt, mod[:nb], None, weights, btiles, min(256, t_p), min(DSA_Q_BLOCK, t_p))
    y_s, new_s = _stream(x_sample, mod[nb:], caches, weights, btiles, t_s, t_s)
    return (y_p, y_s) + new_p + new_s
```

```python
import functools

import jax
import jax.numpy as jnp
import numpy as np
from jax import lax
from jax.experimental import pallas as pl
from jax.experimental.pallas import tpu as pltpu

F32 = jnp.float32
BF16 = jnp.bfloat16
I32 = jnp.int32

D_MODEL = 1024
CHUNK = 64
SB_HEADS = 8
HEAD_DIM = 64
DSA_HEADS = 8
DSA_KV_HEADS = 2
IDX_HEADS = 8
IDX_DIM = 64
TOPK_MAX = 256
N_BUCKETS = 32
N_EXPERTS = 32
TOP_K_EXPERTS = 4
D_FF = 1024
SWIGLU_LIMIT = 7.0
SWIGLU_ALPHA = 1.702
MOE_BLOCK = 128
LN_EPS = 1e-5
DEPTH = 1
DEEPNORM_ALPHA = (2.0 * DEPTH) ** 0.25

LANES = 128
ATT_BLOCK = 128
DSA_Q_BLOCK = 256
SB_BLOCK = 256
VMEM_LIMIT = 56 * 1024 * 1024

EXP_ZERO_BELOW = -104.0
NEG_INF_KEY = -2139095041
INT32_MAX = 2147483647
MASKED_LOGIT = -1e30
POST_SUB_ROWS = 256
SOFTMAX_DEN_FLOOR = 1e-30
SWIGLU_TILE = 2 * LANES

_SEG = {}
_off = 0
for _name, _w in (("qa", 512), ("ka", 512), ("va", 512), ("qb", 512), ("kb", 128), ("vb", 128),
                  ("kbd", 256), ("vx", 256), ("qi", 512), ("kid", 128), ("wi", 128),
                  ("ga", 1024), ("gb", 1024)):
    _SEG[_name] = (_off, _off + _w)
    _off += _w
PACKED_COLS = _off


def _cparams(sem):
    return pltpu.CompilerParams(dimension_semantics=sem, vmem_limit_bytes=VMEM_LIMIT)


def _dot(a, b):
    return jnp.dot(a, b, preferred_element_type=F32)


def _dot_nt(a, b):
    return lax.dot_general(a, b, (((1,), (1,)), ((), ())), preferred_element_type=F32)


def _split_bf16(x):
    hi = x.astype(BF16)
    lo = (x - hi.astype(F32)).astype(BF16)
    return hi, lo


def _dot3(a, b_hi, b_lo):
    a_hi, a_lo = _split_bf16(a)
    return _dot(a_hi, b_hi) + (_dot(a_hi, b_lo) + _dot(a_lo, b_hi))


def _mod_kernel(c_ref, w_ref, b_ref, o_ref):
    c = c_ref[...]
    s = c * jax.nn.sigmoid(c)
    w_hi, w_lo = _split_bf16(w_ref[...])
    o_ref[...] = _dot3(s, w_hi, w_lo) + b_ref[...]


def _mod_call(c_all, w_ada, b_ada):
    n, d = c_all.shape
    cols = w_ada.shape[1]
    tn = 1024
    return pl.pallas_call(
        _mod_kernel,
        grid=(cols // tn,),
        in_specs=[pl.BlockSpec((n, d), lambda j: (0, 0)),
                  pl.BlockSpec((d, tn), lambda j: (0, j)),
                  pl.BlockSpec((1, tn), lambda j: (0, j))],
        out_specs=pl.BlockSpec((n, tn), lambda j: (0, j)),
        out_shape=jax.ShapeDtypeStruct((n, cols), F32),
        compiler_params=_cparams(("arbitrary",)),
        name="adaln_mod",
    )(c_all, w_ada, b_ada.reshape(1, cols))


def _proj_kernel(x_ref, mod_ref, w_ref, b_ref,
                 qa_ref, ka32_ref, va32_ref, ka16_ref, va16_ref,
                 qb_ref, kb32_ref, vb32_ref, kbd_ref, vx_ref,
                 qi_ref, ki32_ref, kid_ref, wi_ref, sg_ref):
    sh1 = mod_ref[0:1, :]
    sc1 = mod_ref[1:2, :]
    h = (x_ref[...] * (1.0 + sc1) + sh1).astype(BF16)

    def seg(name):
        a, b = _SEG[name]
        return _dot(h, w_ref[:, a:b]) + b_ref[:, a:b]

    qa_ref[...] = (seg("qa") * HEAD_DIM ** -0.5).astype(BF16)
    ka = seg("ka")
    ka32_ref[...] = ka
    ka16_ref[...] = ka.astype(BF16)
    va = seg("va")
    va32_ref[...] = va
    va16_ref[...] = va.astype(BF16)
    qb_ref[...] = (seg("qb") * HEAD_DIM ** -0.5).astype(BF16)
    kb32_ref[...] = seg("kb")
    vb32_ref[...] = seg("vb")
    kbd_ref[...] = seg("kbd").astype(BF16)
    vx_ref[...] = seg("vx").astype(BF16)
    qi_ref[...] = (seg("qi") * IDX_DIM ** -0.5).astype(BF16)
    kid = seg("kid")
    ki32_ref[...] = kid[:, :IDX_DIM]
    kid_ref[...] = kid.astype(BF16)
    wi_ref[...] = seg("wi") * IDX_HEADS ** -0.5
    a, _ = _SEG["ga"]
    _, b = _SEG["gb"]
    sg_ref[...] = jax.nn.sigmoid(_dot(h, w_ref[:, a:b]) + b_ref[:, a:b])


def _pack_w_in(w_in, b_in):
    offs = np.cumsum((0, 512, 512, 512, 512, 128, 128, 512, 8, 64, 1024, 1024))
    qa, ka, va, qb, kb, vb, qi, wi, ki, ga, gb = [slice(int(offs[i]), int(offs[i + 1])) for i in range(11)]

    def pack(m, fill):
        kb_m, vb_m = m[..., kb], m[..., vb]
        dup = lambda t: jnp.concatenate([t[..., :64], t[..., :64], t[..., 64:], t[..., 64:]], axis=-1)
        wi_m = jnp.concatenate([m[..., wi], jnp.zeros(m.shape[:-1] + (LANES - IDX_HEADS,), m.dtype)], axis=-1)
        return jnp.concatenate([m[..., qa], m[..., ka], m[..., va], m[..., qb], kb_m, vb_m, dup(kb_m),
                                _with_ones(vb_m, fill),
                                m[..., qi], m[..., ki], m[..., ki], wi_m, m[..., ga], m[..., gb]], axis=-1)

    return pack(w_in, 0.0).astype(BF16), pack(b_in.reshape(1, -1), 1.0)


def _with_ones(v, fill=1.0):
    f = jnp.full(v.shape[:-1] + (HEAD_DIM,), fill, v.dtype)
    return jnp.concatenate([v[..., :HEAD_DIM], f, v[..., HEAD_DIM:], f], axis=-1)


def _proj_call(x, mod, w_packed, b_packed, tm):
    s, t, d = x.shape
    nt = t // tm
    row = lambda w: pl.BlockSpec((None, tm, w), lambda b, i: (b, i, 0))
    shp = lambda w, dt: jax.ShapeDtypeStruct((s, t, w), dt)
    outs = [(512, BF16), (512, F32), (512, F32), (512, BF16), (512, BF16),
            (512, BF16), (128, F32), (128, F32), (256, BF16), (256, BF16),
            (512, BF16), (IDX_DIM, F32), (128, BF16), (128, F32), (2048, F32)]
    return pl.pallas_call(
        _proj_kernel,
        grid=(s, nt),
        in_specs=[row(d),
                  pl.BlockSpec((None, 6, d), lambda b, i: (b, 0, 0)),
                  pl.BlockSpec((d, PACKED_COLS), lambda b, i: (0, 0)),
                  pl.BlockSpec((1, PACKED_COLS), lambda b, i: (0, 0))],
        out_specs=[row(w) for w, _ in outs],
        out_shape=[shp(w, dt) for w, dt in outs],
        compiler_params=_cparams(("arbitrary", "arbitrary")),
        name="in_proj",
    )(x, mod, w_packed, b_packed)


def _bias_kernel(tab_ref, o_ref, max_ref):
    r = lax.broadcasted_iota(I32, (DSA_Q_BLOCK, ATT_BLOCK), 0)
    c = lax.broadcasted_iota(I32, (DSA_Q_BLOCK, ATT_BLOCK), 1)
    half = N_BUCKETS // 2
    max_exact = half // 2
    for j, off in enumerate((ATT_BLOCK, 0, -ATT_BLOCK, -(1 << 20))):
        rel = c - r + off
        n = jnp.abs(rel)
        large = jnp.full_like(n, max_exact)
        for thr in (12, 16, 23, 32, 46, 64, 91):
            large = large + (n >= thr).astype(I32)
        bucket = jnp.where(rel > 0, half, 0) + jnp.where(n < max_exact, n, large)
        for h in range(DSA_HEADS):
            acc = jnp.zeros((DSA_Q_BLOCK, ATT_BLOCK), F32)
            for b in range(N_BUCKETS):
                acc = jnp.where(bucket == b, tab_ref[b, h], acc)
            o_ref[h, j] = acc
    for h in range(DSA_HEADS):
        top = tab_ref[0, h]
        for b in range(1, N_BUCKETS):
            top = jnp.maximum(top, tab_ref[b, h])
        max_ref[h] = jnp.full((8, LANES), top, F32)


def _bias_call(rel_bias):
    return pl.pallas_call(
        _bias_kernel,
        in_specs=[pl.BlockSpec(memory_space=pltpu.SMEM)],
        out_specs=[pl.BlockSpec(memory_space=pltpu.VMEM), pl.BlockSpec(memory_space=pltpu.VMEM)],
        out_shape=[jax.ShapeDtypeStruct((DSA_HEADS, 4, DSA_Q_BLOCK, ATT_BLOCK), F32),
                   jax.ShapeDtypeStruct((DSA_HEADS, 8, LANES), F32)],
        name="t5_bias_tiles",
    )(rel_bias)


def _softplus(z):
    return jnp.maximum(z, 0.0) + jnp.log1p(jnp.exp(-jnp.abs(z)))


def _sb_kernel(q_ref, k_ref, v_ref, o_ref, *, tq, last_fn):
    tk = SB_BLOCK
    last = last_fn(pl.program_id(2))
    lane = lax.broadcasted_iota(I32, (1, LANES), 1)
    low = lane < HEAD_DIM
    q = q_ref[...]
    zero = jnp.zeros_like(q)
    qh = (jnp.where(low, q, zero), jnp.where(low, zero, q))
    uj = lax.broadcasted_iota(I32, (tk, tk), 0)
    us = lax.broadcasted_iota(I32, (tk, tk), 1)
    u_mat = jnp.where(uj > us, 1.0, 0.0).astype(BF16)

    def tile(kb, vis, carries):
        start = pl.multiple_of(kb * tk, tk)
        kblk = k_ref[pl.ds(start, tk), :]
        vblk = v_ref[pl.ds(start, tk), :]
        pv = []
        new_carries = []
        for h in range(2):
            z = _dot_nt(qh[h], kblk)
            sp = _softplus(z)
            lk = -sp if vis is None else jnp.where(vis, -sp, 0.0)
            hi, lo = _split_bf16(lk)
            after = _dot(hi, u_mat) + _dot(lo, u_mat)
            w = jnp.exp((z - sp) + (after + carries[h]))
            if vis is not None:
                w = jnp.where(vis, w, 0.0)
            pv.append(_dot(w.astype(BF16), vblk))
            new_carries.append(carries[h] + jnp.sum(lk, axis=1, keepdims=True))
        return jnp.where(low, pv[0], pv[1]), new_carries

    r = lax.broadcasted_iota(I32, (tq, tk), 0)
    c = lax.broadcasted_iota(I32, (tq, tk), 1)
    zc = jnp.zeros((tq, 1), F32)
    acc, carries = tile(last, c < r, [zc, zc])
    has_prev = jnp.full((tq, tk), last > 0)
    pv, carries = tile(jnp.maximum(last - 1, 0), has_prev, carries)
    acc = acc + pv

    def live(carries):
        return jnp.max(jnp.maximum(carries[0], carries[1])) > EXP_ZERO_BELOW

    def cond(st):
        kb, go, _, _, _ = st
        return jnp.logical_and(kb >= 0, go)

    def body(st):
        kb, _, acc, c0, c1 = st
        pv, nc = tile(kb, None, [c0, c1])
        return kb - 1, live(nc), acc + pv, nc[0], nc[1]

    st = lax.while_loop(cond, body, (last - 2, live(carries), acc, carries[0], carries[1]))
    o_ref[...] = st[2].astype(o_ref.dtype)


def _sb_call(q, k, v, tq, last_fn):
    s, t, _ = q.shape
    tkk = k.shape[1]
    return pl.pallas_call(
        functools.partial(_sb_kernel, tq=tq, last_fn=last_fn),
        grid=(s, SB_HEADS // 2, t // tq),
        in_specs=[pl.BlockSpec((None, tq, LANES), lambda b, p, i: (b, i, p)),
                  pl.BlockSpec((None, tkk, LANES), lambda b, p, i: (b, 0, p)),
                  pl.BlockSpec((None, tkk, LANES), lambda b, p, i: (b, 0, p))],
        out_specs=pl.BlockSpec((None, tq, LANES), lambda b, p, i: (b, i, p)),
        out_shape=jax.ShapeDtypeStruct(q.shape, BF16),
        compiler_params=_cparams(("arbitrary", "arbitrary", "arbitrary")),
        name="stick_breaking",
    )(q, k, v)


def _dsa_kernel(qi_ref, wi_ref, qb_ref, ki_ref, kb_ref, vx_ref, bt_ref, bmax_ref, o_ref,
                qis_sc, wb_sc, qbs_sc, key_sc, smax_sc, mx_sc, acc_sc, kn_sc, thr_sc, *, tq, topk, diag_fn, adm_fn):
    tk = ATT_BLOCK
    group = DSA_HEADS // DSA_KV_HEADS

    @pl.when(pl.program_id(1) == 0)
    def _():
        for n in range(DSA_KV_HEADS):
            def body(pi, best):
                start = pl.multiple_of(pi * (2 * tk), 2 * tk)
                x = kb_ref[pl.ds(start, 2 * tk), n * LANES:(n + 1) * LANES].astype(F32)
                return jnp.maximum(best, jnp.sum(x * x, axis=1, keepdims=True))
            best = lax.fori_loop(0, kb_ref.shape[0] // (2 * tk), body, jnp.zeros((2 * tk, 1), F32))
            kn_sc[n] = jnp.broadcast_to(jnp.sqrt(0.5 * jnp.max(best, axis=0, keepdims=True)), (8, LANES))

    diag = diag_fn(pl.program_id(1))
    last = diag + (tq + tk - 1) // tk - 1
    lane = lax.broadcasted_iota(I32, (1, LANES), 1)
    low = lane < HEAD_DIM

    for h in range(IDX_HEADS):
        p = h // 2
        t = qi_ref[:, p * LANES:(p + 1) * LANES]
        z = jnp.zeros_like(t)
        qis_sc[h * tq:(h + 1) * tq, :] = jnp.where(low, t, z) if h % 2 == 0 else jnp.where(low, z, t)
        t = qb_ref[:, p * LANES:(p + 1) * LANES]
        g = h % group
        qbs_sc[h // group, g * tq:(g + 1) * tq, :] = jnp.where(low, t, z) if h % 2 == 0 else jnp.where(low, z, t)
        wb_sc[h] = jnp.broadcast_to(wi_ref[:, h:h + 1], (tq, LANES))

    r = lax.broadcasted_iota(I32, (tq, tk), 0)
    c = lax.broadcasted_iota(I32, (tq, tk), 1)
    last_pair = last // 2
    smax_sc[...] = jnp.full((tq, tk), -jnp.inf, F32)

    def score_pair(pi, masked):
        start = pl.multiple_of(pi * (2 * tk), 2 * tk)
        d = _dot_nt(qis_sc[...], ki_ref[pl.ds(start, 2 * tk), :])
        for half in range(2):
            s = jnp.zeros((tq, tk), F32)
            for h in range(IDX_HEADS):
                s = s + wb_sc[h] * jnp.maximum(d[h * tq:(h + 1) * tq, half * tk:(half + 1) * tk], 0.0)
            bits = pltpu.bitcast(s, I32)
            key = jnp.where(bits < 0, bits ^ INT32_MAX, bits)
            if masked:
                adm = adm_fn(pl.program_id(1), r, c + (start + half * tk))
                key = jnp.where(adm, key, NEG_INF_KEY)
                s = jnp.where(adm, s, -jnp.inf)
            key_sc[2 * pi + half] = key
            smax_sc[...] = jnp.maximum(smax_sc[...], s)

    def score_body(pi, carry):
        score_pair(pi, False)
        return carry

    lax.fori_loop(0, last_pair, score_body, 0)
    score_pair(last_pair, True)

    rc = min(tq, ATT_BLOCK)
    ones_mat = jnp.ones((tk, tk), BF16)

    def row_total(x):
        return _dot(x.astype(BF16), ones_mat)

    def count_ge(*thrs):
        for j, t in enumerate(thrs):
            if hasattr(t, "shape"):
                thr_sc[j] = t
        chunks = list(range(0, tq, rc))

        def body(pi, accs):
            out = []
            for ci, r0 in enumerate(chunks):
                k0 = key_sc[2 * pi, r0:r0 + rc, :]
                k1 = key_sc[2 * pi + 1, r0:r0 + rc, :]
                for j, t in enumerate(thrs):
                    tt = thr_sc[j, r0:r0 + rc, :] if hasattr(t, "shape") else t
                    out.append(accs[ci * len(thrs) + j]
                               + (jnp.where(k0 >= tt, 1.0, 0.0) + jnp.where(k1 >= tt, 1.0, 0.0)))
            return tuple(out)

        accs = lax.fori_loop(0, last_pair + 1, body,
                             tuple(jnp.zeros((rc, tk), F32) for _ in range(len(chunks) * len(thrs))))
        totals = []
        for j in range(len(thrs)):
            parts = [row_total(accs[ci * len(thrs) + j]) for ci in range(len(chunks))]
            totals.append(parts[0] if len(parts) == 1 else jnp.concatenate(parts, axis=0))
        return totals

    def midpoint(lo, hi):
        return (lo >> 1) + (hi >> 1) + (lo & hi & 1)

    def float_to_key(v):
        bits = pltpu.bitcast(v, I32)
        return jnp.where(bits < 0, bits ^ INT32_MAX, bits)

    kf = float(topk)

    def bis_cond(st):
        return st[0]

    def key_to_float(k):
        return pltpu.bitcast(jnp.where(k < 0, k ^ INT32_MAX, k), F32)

    def is_open(piv, lo, cnt_lo, cnt_hi):
        return jnp.logical_and(jnp.logical_and(piv != lo, cnt_lo != kf), cnt_hi != kf - 1.0)

    def bis_body(st):
        _, lo, hi, cnt_lo, cnt_hi = st
        for _ in range(2):
            piv = midpoint(lo, hi)
            open_ = is_open(piv, lo, cnt_lo, cnt_hi)
            cnt, = count_ge(piv)
            ge = cnt >= kf
            up = jnp.logical_and(open_, ge)
            dn = jnp.logical_and(open_, jnp.logical_not(ge))
            lo = jnp.where(up, piv, lo)
            cnt_lo = jnp.where(up, cnt, cnt_lo)
            hi = jnp.where(dn, piv, hi)
            cnt_hi = jnp.where(dn, cnt, cnt_hi)
        go = jnp.max(jnp.where(is_open(midpoint(lo, hi), lo, cnt_lo, cnt_hi), 1.0, 0.0)) > 0.0
        return go, lo, hi, cnt_lo, cnt_hi

    def max_below(t):
        thr_sc[0] = t
        chunks = list(range(0, tq, rc))

        def body(pi, accs):
            out = []
            for ci, r0 in enumerate(chunks):
                tt = thr_sc[0, r0:r0 + rc, :]
                k0 = key_sc[2 * pi, r0:r0 + rc, :]
                k1 = key_sc[2 * pi + 1, r0:r0 + rc, :]
                below = jnp.maximum(jnp.where(k0 < tt, k0, NEG_INF_KEY), jnp.where(k1 < tt, k1, NEG_INF_KEY))
                out.append(jnp.maximum(accs[ci], below))
            return tuple(out)

        accs = lax.fori_loop(0, last_pair + 1, body,
                             tuple(jnp.full((rc, tk), NEG_INF_KEY, I32) for _ in chunks))
        parts = [jnp.broadcast_to(float_to_key(jnp.max(key_to_float(a), axis=1, keepdims=True)), (rc, tk))
                 for a in accs]
        return parts[0] if len(parts) == 1 else jnp.concatenate(parts, axis=0)

    c_zero, c_pos = count_ge(0, 1)
    pos = c_pos >= kf
    zer = jnp.logical_and(c_zero >= kf, jnp.logical_not(pos))
    hi_max = jnp.broadcast_to(float_to_key(jnp.max(smax_sc[...], axis=1, keepdims=True)) + 1, (tq, tk))
    lo0 = jnp.where(pos, 1, jnp.where(zer, 0, NEG_INF_KEY))
    hi0 = jnp.where(pos, hi_max, jnp.where(zer, 1, 0))
    cnt_lo0 = jnp.where(pos, c_pos, jnp.where(zer, c_zero, -1.0))
    cnt_hi0 = jnp.where(pos, 0.0, jnp.where(zer, c_pos, c_zero))
    go0 = jnp.max(jnp.where(is_open(midpoint(lo0, hi0), lo0, cnt_lo0, cnt_hi0), 1.0, 0.0)) > 0.0
    _, lo, hi, cnt_lo, cnt_hi = lax.while_loop(bis_cond, bis_body, (go0, lo0, hi0, cnt_lo0, cnt_hi0))

    by_max = jnp.logical_and(jnp.logical_and(cnt_hi == kf - 1.0, cnt_lo != kf), midpoint(lo, hi) != lo)
    thr = jnp.where(by_max, max_below(hi), lo)
    cnt_thr, = count_ge(thr)

    tied = jnp.where(by_max, cnt_thr, cnt_lo) > kf

    @pl.when(jnp.max(jnp.where(tied, 1.0, 0.0)) > 0.0)
    def _():
        above, = count_ge(thr + 1)
        need = kf - above
        prefix_mat = jnp.where(lax.broadcasted_iota(I32, (tk, tk), 0) <= lax.broadcasted_iota(I32, (tk, tk), 1),
                               1.0, 0.0).astype(BF16)

        def demote(kb, before):
            key = key_sc[kb]
            eq = jnp.logical_and(tied, key == thr)
            ind = jnp.where(eq, 1.0, 0.0)
            rank = _dot(ind.astype(BF16), prefix_mat) + before
            key_sc[kb] = jnp.where(jnp.logical_and(eq, rank > need), NEG_INF_KEY, key)
            return before + row_total(ind)

        lax.fori_loop(0, last + 1, demote, jnp.zeros((tq, tk), F32))

    thr_eff = jnp.maximum(thr, NEG_INF_KEY + 1)

    def masked_logits(pi, n):
        start = pl.multiple_of(pi * (2 * tk), 2 * tk)
        lg = _dot_nt(qbs_sc[n], kb_ref[pl.ds(start, 2 * tk), n * LANES:(n + 1) * LANES])
        sel = [key_sc[2 * pi + half] >= thr_eff for half in range(2)]
        j = [jnp.clip(diag - (2 * pi + half) + 1, 0, 3) for half in range(2)]
        out = []
        for g in range(group):
            h = n * group + g
            out.append([jnp.where(sel[half],
                                  lg[g * tq:(g + 1) * tq, half * tk:(half + 1) * tk] + bt_ref[h, j[half], 0:tq, :],
                                  MASKED_LOGIT) for half in range(2)])
        return out

    def max_body(pi, carry):
        for n in range(DSA_KV_HEADS):
            for g, (la, lb) in enumerate(masked_logits(pi, n)):
                h = n * group + g
                mx_sc[h] = jnp.maximum(mx_sc[h], jnp.maximum(la, lb))
        return carry

    def exact_row_maxima():
        for h in range(DSA_HEADS):
            mx_sc[h] = jnp.full((tq, LANES), MASKED_LOGIT, F32)
        lax.fori_loop(0, last_pair + 1, max_body, 0)
        for h in range(DSA_HEADS):
            mx_sc[h] = jnp.broadcast_to(jnp.max(mx_sc[h], axis=1, keepdims=True), (tq, LANES))

    def bounded_row_maxima():
        for h in range(DSA_HEADS):
            n, g = h // group, h % group
            q = qbs_sc[n, g * tq:(g + 1) * tq, :].astype(F32)
            q_norm = jnp.sqrt(jnp.sum(q * q, axis=1, keepdims=True))
            mx_sc[h] = q_norm * kn_sc[n, 0:1, :] + bmax_ref[h, 0:1, :]

    def pv_body(pi, carry):
        start = pl.multiple_of(pi * (2 * tk), 2 * tk)
        for n in range(DSA_KV_HEADS):
            ps = []
            for g, (la, lb) in enumerate(masked_logits(pi, n)):
                m = mx_sc[n * group + g]
                ps.append(jnp.concatenate([jnp.exp(la - m).astype(BF16), jnp.exp(lb - m).astype(BF16)], axis=1))
            pv = _dot(jnp.concatenate(ps, axis=0), vx_ref[pl.ds(start, 2 * tk), n * LANES:(n + 1) * LANES])
            for g in range(group):
                h = n * group + g
                acc_sc[h] = acc_sc[h] + pv[g * tq:(g + 1) * tq]
        return carry

    def weights_sweep():
        for h in range(DSA_HEADS):
            acc_sc[h] = jnp.zeros((tq, LANES), F32)
        lax.fori_loop(0, last_pair + 1, pv_body, 0)

    bounded_row_maxima()
    weights_sweep()
    den = acc_sc[0][:, HEAD_DIM:HEAD_DIM + 1]
    for h in range(1, DSA_HEADS):
        den = jnp.minimum(den, acc_sc[h][:, HEAD_DIM:HEAD_DIM + 1])
    healthy = jnp.min(jnp.where(den >= SOFTMAX_DEN_FLOOR, 1.0, 0.0)) > 0.0

    @pl.when(jnp.logical_not(healthy))
    def _():
        exact_row_maxima()
        weights_sweep()

    def normalised(h):
        a = acc_sc[h]
        return a / pltpu.roll(a, HEAD_DIM, axis=1)

    for p in range(DSA_HEADS // 2):
        o1 = pltpu.roll(normalised(2 * p + 1), HEAD_DIM, axis=1)
        o_ref[:, p * LANES:(p + 1) * LANES] = jnp.where(low, normalised(2 * p), o1).astype(o_ref.dtype)


def _dsa_call(qi, wi, qb, kid, kbd, vx, btiles, bmax, tq, topk, diag_fn, adm_fn):
    s, t, _ = qi.shape
    tkk = kid.shape[1]
    assert tkk % (2 * ATT_BLOCK) == 0, "keys must come in whole pairs of blocks"
    nkb = tkk // ATT_BLOCK
    rowq = lambda w: pl.BlockSpec((None, tq, w), lambda b, i: (b, i, 0))
    full = lambda w: pl.BlockSpec((None, tkk, w), lambda b, i: (b, 0, 0))
    return pl.pallas_call(
        functools.partial(_dsa_kernel, tq=tq, topk=topk, diag_fn=diag_fn, adm_fn=adm_fn),
        grid=(s, t // tq),
        in_specs=[rowq(512), rowq(LANES), rowq(512), full(LANES), full(2 * LANES), full(2 * LANES),
                  pl.BlockSpec(btiles.shape, lambda b, i: (0, 0, 0, 0)),
                  pl.BlockSpec(bmax.shape, lambda b, i: (0, 0, 0))],
        out_specs=rowq(512),
        out_shape=jax.ShapeDtypeStruct((s, t, 512), BF16),
        scratch_shapes=[pltpu.VMEM((IDX_HEADS * tq, LANES), BF16),
                        pltpu.VMEM((IDX_HEADS, tq, LANES), F32),
                        pltpu.VMEM((DSA_KV_HEADS, DSA_HEADS // DSA_KV_HEADS * tq, LANES), BF16),
                        pltpu.VMEM((nkb, tq, ATT_BLOCK), I32),
                        pltpu.VMEM((tq, ATT_BLOCK), F32),
                        pltpu.VMEM((DSA_HEADS, tq, LANES), F32),
                        pltpu.VMEM((DSA_HEADS, tq, LANES), F32),
                        pltpu.VMEM((DSA_KV_HEADS, 8, LANES), F32),
                        pltpu.VMEM((2, tq, ATT_BLOCK), I32)],
        compiler_params=_cparams(("arbitrary", "arbitrary")),
        name="dsa_attention",
    )(qi, wi, qb, kid, kbd, vx, btiles, bmax)


def _layer_norm(x, g, b):
    mu = jnp.mean(x, axis=-1, keepdims=True)
    xc = x - mu
    var = jnp.mean(xc * xc, axis=-1, keepdims=True)
    return xc * lax.rsqrt(var + LN_EPS) * g + b


def _post_kernel(oa_ref, ob_ref, sg_ref, x_ref, mod_ref, wsb_ref, wdsa_ref, wout_ref, g_ref, b_ref,
                 wrh_ref, wrl_ref, br_ref, x1_ref, h2_ref, ti_ref, tg_ref):
    tm = x_ref.shape[0]
    sub = min(tm, POST_SUB_ROWS)
    for r0 in range(0, tm, sub):
        _post_rows(slice(r0, r0 + sub), oa_ref, ob_ref, sg_ref, x_ref, mod_ref, wsb_ref, wdsa_ref, wout_ref,
                   g_ref, b_ref, wrh_ref, wrl_ref, br_ref, x1_ref, h2_ref, ti_ref, tg_ref)


def _post_rows(rows, oa_ref, ob_ref, sg_ref, x_ref, mod_ref, wsb_ref, wdsa_ref, wout_ref, g_ref, b_ref,
               wrh_ref, wrl_ref, br_ref, x1_ref, h2_ref, ti_ref, tg_ref):
    d = D_MODEL
    ya = _dot(oa_ref[rows, :], wsb_ref[...])
    yb = _dot(ob_ref[rows, :], wdsa_ref[...])
    merged = sg_ref[rows, :d] * ya + sg_ref[rows, d:] * yb
    mix = _dot(merged.astype(BF16), wout_ref[...])
    g1 = mod_ref[2:3, :]
    x1 = _layer_norm(DEEPNORM_ALPHA * x_ref[rows, :] + g1 * mix, g_ref[...], b_ref[...])
    x1_ref[rows, :] = x1
    h2 = x1 * (1.0 + mod_ref[4:5, :]) + mod_ref[3:4, :]
    h2_ref[rows, :] = h2
    logits = _dot3(h2, wrh_ref[...], wrl_ref[...]) + br_ref[...]
    lane = lax.broadcasted_iota(I32, logits.shape, 1).astype(F32)
    neg = -jnp.inf
    cur = jnp.where(lane < N_EXPERTS, logits, neg)
    vals, idxs = [], []
    for _ in range(TOP_K_EXPERTS):
        m = jnp.max(cur, axis=1, keepdims=True)
        idx = jnp.min(jnp.where(cur == m, lane, float(LANES)), axis=1, keepdims=True)
        vals.append(m)
        idxs.append(idx)
        cur = jnp.where(lane == idx, neg, cur)
    es = [jnp.exp(v - vals[0]) for v in vals]
    tot = es[0] + es[1] + es[2] + es[3]
    ti = jnp.zeros(logits.shape, F32)
    tg = jnp.zeros(logits.shape, F32)
    for k in range(TOP_K_EXPERTS):
        ti = jnp.where(lane == k, idxs[k], ti)
        tg = jnp.where(lane == k, es[k] / tot, tg)
    ti_ref[rows, :] = ti.astype(I32)
    tg_ref[rows, :] = tg


def _post_call(oa, ob, sg, x, mod, wsb, wdsa, wout, ln_g, ln_b, wr_hi, wr_lo, br, tm):
    s, t, d = x.shape
    row = lambda w: pl.BlockSpec((None, tm, w), lambda b, i: (b, i, 0))
    const = lambda a: pl.BlockSpec(a.shape, lambda b, i: (0,) * a.ndim)
    shp = lambda w, dt: jax.ShapeDtypeStruct((s, t, w), dt)
    return pl.pallas_call(
        _post_kernel,
        grid=(s, t // tm),
        in_specs=[row(512), row(512), row(2 * d), row(d),
                  pl.BlockSpec((None, 6, d), lambda b, i: (b, 0, 0)),
                  const(wsb), const(wdsa), const(wout), const(ln_g), const(ln_b),
                  const(wr_hi), const(wr_lo), const(br)],
        out_specs=[row(d), row(d), row(LANES), row(LANES)],
        out_shape=[shp(d, F32), shp(d, F32), shp(LANES, I32), shp(LANES, F32)],
        compiler_params=_cparams(("arbitrary", "arbitrary")),
        name="post_attention_router",
    )(oa, ob, sg, x, mod, wsb, wdsa, wout, ln_g, ln_b, wr_hi, wr_lo, br)


def _dispatch_kernel(zoff_ref, nused_ref, dest_ref, h_ref, xs_hbm, buf, zbuf, sem, zsem, *, tm, tile, n_tiles):
    i = pl.program_id(0)
    nb = pl.num_programs(0)
    k = TOP_K_EXPERTS
    slot = i % 2

    def row_copy(dst, slot, r):
        return pltpu.make_async_copy(buf.at[slot, pl.ds(r, 1), :], xs_hbm.at[pl.ds(dst, 1), :], sem.at[slot])

    def wait_all(slot):
        def body(r, carry):
            for _ in range(k):
                row_copy(0, slot, r).wait()
            return carry
        lax.fori_loop(0, tm, body, 0, unroll=4)

    def zero_fill(start):
        return pltpu.make_async_copy(zbuf, xs_hbm.at[pl.ds(start, tile), :], zsem)

    @pl.when(i == 0)
    def _():
        zbuf[...] = jnp.zeros_like(zbuf)
        fills = [(zoff_ref[e] >= 0, pl.multiple_of(jnp.maximum(zoff_ref[e], 0), MOE_BLOCK)) for e in range(N_EXPERTS)]
        fills += [(t >= nused_ref[0], t * tile) for t in range(max(n_tiles - N_EXPERTS - 1, 0), n_tiles)]
        for on, start in fills:
            @pl.when(on)
            def _():
                zero_fill(start).start()
        for on, start in fills:
            @pl.when(on)
            def _():
                zero_fill(start).wait()

    @pl.when(i >= 2)
    def _():
        wait_all(slot)

    buf[slot] = h_ref[...]
    for r in range(tm):
        for j in range(k):
            row_copy(dest_ref[0, r * k + j], slot, r).start()

    @pl.when(i == nb - 1)
    def _():
        wait_all(slot)

        @pl.when(nb >= 2)
        def _():
            wait_all(1 - slot)


def _dispatch_call(zoff, nused, dest, h2, n_rows, tile):
    n, d = h2.shape
    tm = min(MOE_BLOCK, n)
    nb = n // tm
    dest3 = dest.reshape(nb, 1, tm * TOP_K_EXPERTS)
    grid_spec = pltpu.PrefetchScalarGridSpec(
        num_scalar_prefetch=2,
        grid=(nb,),
        in_specs=[pl.BlockSpec((None, 1, tm * TOP_K_EXPERTS), lambda i, zo, nu: (i, 0, 0), memory_space=pltpu.SMEM),
                  pl.BlockSpec((tm, d), lambda i, zo, nu: (i, 0))],
        out_specs=pl.BlockSpec(memory_space=pl.ANY),
        scratch_shapes=[pltpu.VMEM((2, tm, d), F32), pltpu.VMEM((tile, d), F32),
                        pltpu.SemaphoreType.DMA((2,)), pltpu.SemaphoreType.DMA(())],
    )
    return pl.pallas_call(
        functools.partial(_dispatch_kernel, tm=tm, tile=tile, n_tiles=n_rows // tile),
        grid_spec=grid_spec,
        out_shape=jax.ShapeDtypeStruct((n_rows, d), F32),
        compiler_params=_cparams(("arbitrary",)),
        name="moe_dispatch",
    )(zoff, nused, dest3, h2)


def _ffn_kernel(be_ref, x_ref, wup_ref, bup_ref, wdn_ref, bdn_ref, o_ref):
    u = _dot(x_ref[...].astype(BF16), wup_ref[...]) + bup_ref[...]
    acts = []
    for t in range(2 * D_FF // SWIGLU_TILE):
        a = t * SWIGLU_TILE
        glu = jnp.minimum(u[:, a:a + LANES], SWIGLU_LIMIT)
        lin = jnp.clip(u[:, a + LANES:a + SWIGLU_TILE], -SWIGLU_LIMIT, SWIGLU_LIMIT)
        acts.append((glu * jax.nn.sigmoid(SWIGLU_ALPHA * glu) * (lin + 1.0)).astype(BF16))
    o_ref[...] = _dot(jnp.concatenate(acts, axis=1), wdn_ref[...]) + bdn_ref[...]


def _deinterleave_kernel(w_ref, o_ref):
    j = lax.broadcasted_iota(I32, (SWIGLU_TILE, SWIGLU_TILE), 0)
    s = lax.broadcasted_iota(I32, (SWIGLU_TILE, SWIGLU_TILE), 1)
    src = jnp.where(s < LANES, 2 * s, 2 * (s - LANES) + 1)
    perm = jnp.where(j == src, 1.0, 0.0).astype(BF16)
    for t in range(w_ref.shape[1] // SWIGLU_TILE):
        a = t * SWIGLU_TILE
        o_ref[:, a:a + SWIGLU_TILE] = _dot(w_ref[:, a:a + SWIGLU_TILE].astype(BF16), perm).astype(BF16)


def _deinterleave_call(w_up):
    e, d, f = w_up.shape
    tr = 512
    return pl.pallas_call(
        _deinterleave_kernel,
        grid=(e, d // tr),
        in_specs=[pl.BlockSpec((None, tr, f), lambda a, b: (a, b, 0))],
        out_specs=pl.BlockSpec((None, tr, f), lambda a, b: (a, b, 0)),
        out_shape=jax.ShapeDtypeStruct((e, d, f), BF16),
        compiler_params=_cparams(("arbitrary", "arbitrary")),
        name="w_up_tiles",
    )(w_up)


def _ffn_call(block_expert, xs, wup, bup, wdn, bdn, tile):
    n_rows, d = xs.shape
    n_blocks = n_rows // tile
    grid_spec = pltpu.PrefetchScalarGridSpec(
        num_scalar_prefetch=1,
        grid=(n_blocks,),
        in_specs=[pl.BlockSpec((tile, d), lambda i, be: (i, 0)),
                  pl.BlockSpec((None, d, 2 * D_FF), lambda i, be: (be[i], 0, 0)),
                  pl.BlockSpec((None, 1, 2 * D_FF), lambda i, be: (be[i], 0, 0)),
                  pl.BlockSpec((None, D_FF, d), lambda i, be: (be[i], 0, 0)),
                  pl.BlockSpec((None, 1, d), lambda i, be: (be[i], 0, 0))],
        out_specs=pl.BlockSpec((tile, d), lambda i, be: (i, 0)),
    )
    return pl.pallas_call(
        _ffn_kernel,
        grid_spec=grid_spec,
        out_shape=jax.ShapeDtypeStruct((n_rows, d), F32),
        compiler_params=_cparams(("arbitrary",)),
        name="expert_ffn",
    )(block_expert, xs, wup, bup, wdn, bdn)


def _combine_kernel(pos0_ref, posn_ref, rows_hbm, x1_ref, tg_ref, mod_ref, g_ref, b_ref, o_ref, buf, sem, *, tm):
    i = pl.program_id(0)
    nb = pl.num_programs(0)
    k = TOP_K_EXPERTS

    def row_copy(src, slot, r, j):
        return pltpu.make_async_copy(rows_hbm.at[pl.ds(src, 1), :], buf.at[slot, j, pl.ds(r, 1), :], sem.at[slot])

    def issue(pos_ref, slot):
        def body(r, carry):
            for j in range(k):
                row_copy(pos_ref[0, r * k + j], slot, r, j).start()
            return carry
        lax.fori_loop(0, tm, body, 0, unroll=4)

    slot = i % 2

    @pl.when(i == 0)
    def _():
        issue(pos0_ref, 0)

    @pl.when(i + 1 < nb)
    def _():
        issue(posn_ref, 1 - slot)

    def wait_body(r, carry):
        for j in range(k):
            row_copy(0, slot, r, j).wait()
        return carry
    lax.fori_loop(0, tm, wait_body, 0, unroll=4)

    gated = [buf[slot, j] * tg_ref[:, j:j + 1] for j in range(k)]
    y = (gated[0] + gated[1]) + (gated[2] + gated[3])
    o_ref[...] = _layer_norm(DEEPNORM_ALPHA * x1_ref[...] + mod_ref[5:6, :] * y, g_ref[...], b_ref[...])


def _combine_call(pos, rows, x1, tg, mod, seq_len, ln_g, ln_b, tm):
    n, d = x1.shape
    nb = n // tm
    per_seq = seq_len // tm
    pos3 = pos.reshape(nb, 1, tm * TOP_K_EXPERTS)
    return pl.pallas_call(
        functools.partial(_combine_kernel, tm=tm),
        grid=(nb,),
        in_specs=[pl.BlockSpec((None, 1, tm * TOP_K_EXPERTS), lambda i: (0, 0, 0), memory_space=pltpu.SMEM),
                  pl.BlockSpec((None, 1, tm * TOP_K_EXPERTS), lambda i: (jnp.minimum(i + 1, nb - 1), 0, 0),
                               memory_space=pltpu.SMEM),
                  pl.BlockSpec(memory_space=pl.ANY),
                  pl.BlockSpec((tm, d), lambda i: (i, 0)),
                  pl.BlockSpec((tm, LANES), lambda i: (i, 0)),
                  pl.BlockSpec((None, 6, d), lambda i: (i // per_seq, 0, 0)),
                  pl.BlockSpec((1, d), lambda i: (0, 0)),
                  pl.BlockSpec((1, d), lambda i: (0, 0))],
        out_specs=pl.BlockSpec((tm, d), lambda i: (i, 0)),
        out_shape=jax.ShapeDtypeStruct((n, d), F32),
        scratch_shapes=[pltpu.VMEM((2, TOP_K_EXPERTS, tm, d), F32), pltpu.SemaphoreType.DMA((2,))],
        compiler_params=_cparams(("arbitrary",)),
        name="moe_combine_ln2",
    )(pos3, pos3, rows, x1, tg, mod, ln_g, ln_b)


def _rank_kernel(ti_ref, rank_ref, cnt_ref, carry_sc):
    tb = ti_ref.shape[0]

    @pl.when(pl.program_id(0) == 0)
    def _():
        carry_sc[...] = jnp.zeros_like(carry_sc)

    lane = lax.broadcasted_iota(I32, (tb, LANES), 1)
    ti = ti_ref[...]
    hits = [ti[:, k:k + 1] == lane for k in range(TOP_K_EXPERTS)]
    tot = jnp.zeros((tb, LANES), F32)
    for hit in hits:
        tot = tot + jnp.where(hit, 1.0, 0.0)
    earlier = jnp.where(lax.broadcasted_iota(I32, (tb, tb), 1) < lax.broadcasted_iota(I32, (tb, tb), 0),
                        1.0, 0.0).astype(BF16)
    before = _dot(earlier, tot.astype(BF16)) + carry_sc[0:1, :]
    out = jnp.zeros((tb, LANES), F32)
    for k, hit in enumerate(hits):
        out = jnp.where(lane == k, jnp.sum(jnp.where(hit, before, 0.0), axis=1, keepdims=True), out)
    rank_ref[...] = out.astype(I32)
    carry_sc[...] = carry_sc[...] + jnp.sum(tot, axis=0, keepdims=True)
    cnt_ref[...] = carry_sc[...]


def _rank_call(ti):
    n = ti.shape[0]
    tb = min(512, n)
    return pl.pallas_call(
        _rank_kernel,
        grid=(n // tb,),
        in_specs=[pl.BlockSpec((tb, LANES), lambda i: (i, 0))],
        out_specs=[pl.BlockSpec((tb, LANES), lambda i: (i, 0)), pl.BlockSpec((8, LANES), lambda i: (0, 0))],
        out_shape=[jax.ShapeDtypeStruct((n, LANES), I32), jax.ShapeDtypeStruct((8, LANES), F32)],
        scratch_shapes=[pltpu.VMEM((8, LANES), F32)],
        compiler_params=_cparams(("arbitrary",)),
        name="moe_rank",
    )(ti)


def _routing(ti, tile):
    n_tok = ti.shape[0]
    n_assign = n_tok * TOP_K_EXPERTS
    e_flat = ti[:, :TOP_K_EXPERTS].reshape(-1)
    rank_tile, cnt = _rank_call(ti)
    rank = rank_tile[:, :TOP_K_EXPERTS].reshape(-1)
    counts = cnt[0, :N_EXPERTS].astype(I32)
    padded = (counts + tile - 1) // tile * tile
    pend = jnp.cumsum(padded)
    pstart = pend - padded
    dest = (pstart[e_flat] + rank).astype(I32)
    n_rows = (n_assign + N_EXPERTS * (tile - 1) + tile - 1) // tile * tile
    n_blocks = n_rows // tile
    tile_start = jnp.arange(n_blocks, dtype=I32) * tile
    block_expert = jnp.minimum(jnp.sum((pend[None, :] <= tile_start[:, None]).astype(I32), axis=1),
                               N_EXPERTS - 1).astype(I32)
    last_tile = jnp.where(padded > 0, pend - tile, -1).astype(I32)
    nused = (pend[-1] // tile).astype(I32).reshape(1)
    return dest, block_expert, last_tile, nused, n_rows


def _stream(x, mod, kv_cache, weights, btiles, tm, tq):
    (w_packed, b_packed, wsb, wdsa, wout, ln1_g, ln1_b, wr_hi, wr_lo, br,
     wup, bup, wdn, bdn, ln2_g, ln2_b) = weights
    s, t, d = x.shape
    (qa, ka32, va32, ka16, va16, qb, kb32, vb32, kbd, vx, qi, ki32, kid, wi, sg) = _proj_call(x, mod, w_packed, b_packed, tm)

    if kv_cache is None:
        k_sb, v_sb, k_id, k_bd, v_x = ka16, va16, kid, kbd, vx
        total = t
        assert tq % ATT_BLOCK == 0
        diag_fn = lambda i: i * (tq // ATT_BLOCK)
        adm_fn = lambda i, r, key_pos: key_pos < i * tq + (r // CHUNK + 1) * CHUNK
    else:
        past = kv_cache[0].shape[1]
        total = past + t
        pad = (-total) % (2 * ATT_BLOCK)
        cat = lambda cache, new: jnp.concatenate(
            [cache, new, jnp.zeros((s, pad, new.shape[2]), new.dtype)], axis=1)
        k_sb, v_sb, k_id, k_bd, v_x = [cat(c_, n_) for c_, n_ in zip(kv_cache, (ka16, va16, kid, kbd, vx))]
        assert past % ATT_BLOCK == 0 and t <= ATT_BLOCK
        diag_fn = lambda i: i * 0 + past // ATT_BLOCK
        adm_fn = lambda i, r, key_pos: key_pos < total
    topk = max(1, min(TOPK_MAX, total // 4))

    if kv_cache is None:
        oa = _sb_call(qa, k_sb, v_sb, min(SB_BLOCK, t), lambda i: i)
    else:
        assert kv_cache[0].shape[1] % SB_BLOCK == 0 and t <= SB_BLOCK
        oa = _sb_call(qa, k_sb, v_sb, t, lambda i: i * 0 + kv_cache[0].shape[1] // SB_BLOCK)
    ob = _dsa_call(qi, wi, qb, k_id, k_bd, v_x, *btiles, tq, topk, diag_fn, adm_fn)
    x1, h2, ti, tg = _post_call(oa, ob, sg, x, mod, wsb, wdsa, wout, ln1_g, ln1_b, wr_hi, wr_lo, br,
                                min(2 * POST_SUB_ROWS, t))

    n = s * t
    tile = 4 * MOE_BLOCK if n * TOP_K_EXPERTS >= N_EXPERTS * 8 * MOE_BLOCK else MOE_BLOCK
    dest, block_expert, last_tile, nused, n_rows = _routing(ti.reshape(n, LANES), tile)
    xs = _dispatch_call(last_tile, nused, dest, h2.reshape(n, d), n_rows, tile)
    rows = _ffn_call(block_expert, xs, wup, bup, wdn, bdn, tile)
    y = _combine_call(dest, rows, x1.reshape(n, d), tg.reshape(n, LANES), mod, t, ln2_g, ln2_b, min(128, t))
    new_rows = (ka32.reshape(1, s, t, SB_HEADS, HEAD_DIM), va32.reshape(1, s, t, SB_HEADS, HEAD_DIM),
                kb32.reshape(1, s, t, DSA_KV_HEADS, HEAD_DIM), vb32.reshape(1, s, t, DSA_KV_HEADS, HEAD_DIM),
                ki32.reshape(1, s, t, IDX_DIM))
    return y.reshape(s, t, d), new_rows


def kernel(x_prompt, x_sample, cache_sb_k, cache_sb_v, cache_dsa_k, cache_dsa_v, cache_idx_k, c_prompt, c_sample, rel_bias, w_ada, b_ada, w_in, b_in, w_o_sb, w_o_dsa, w_out, ln1_g, ln1_b, w_router, b_router, w_up, b_up, w_down, b_down, ln2_g, ln2_b):
    d = D_MODEL
    nb, ns = x_prompt.shape[0], x_sample.shape[0]
    past = cache_sb_k.shape[2]

    mod = _mod_call(jnp.concatenate([c_prompt, c_sample], axis=0), w_ada[0], b_ada[0]).reshape(nb + ns, 6, d)
    btiles = _bias_call(rel_bias)

    w_packed, b_packed = _pack_w_in(w_in[0], b_in[0])
    wr = jnp.concatenate([w_router[0], jnp.zeros((d, LANES - N_EXPERTS), F32)], axis=1)
    wr_hi = wr.astype(BF16)
    wr_lo = (wr - wr_hi.astype(F32)).astype(BF16)
    br = jnp.concatenate([b_router[0], jnp.zeros((LANES - N_EXPERTS,), F32)]).reshape(1, LANES)
    wup = _deinterleave_call(w_up[0])
    bup = b_up[0].reshape(N_EXPERTS, 2 * D_FF // SWIGLU_TILE, LANES, 2).swapaxes(2, 3).reshape(N_EXPERTS, 1, 2 * D_FF)
    weights = (w_packed, b_packed, w_o_sb[0].astype(BF16), w_o_dsa[0].astype(BF16), w_out[0].astype(BF16),
               ln1_g[0].reshape(1, d), ln1_b[0].reshape(1, d), wr_hi, wr_lo, br,
               wup, bup, w_down[0].astype(BF16), b_down[0].reshape(N_EXPERTS, 1, d),
               ln2_g[0].reshape(1, d), ln2_b[0].reshape(1, d))

    dup = lambda a: jnp.concatenate([a[..., :64], a[..., :64], a[..., 64:], a[..., 64:]], axis=-1)
    idx_c = cache_idx_k[0]
    caches = (cache_sb_k[0].reshape(ns, past, SB_HEADS * HEAD_DIM).astype(BF16),
              cache_sb_v[0].reshape(ns, past, SB_HEADS * HEAD_DIM).astype(BF16),
              jnp.concatenate([idx_c, idx_c], axis=-1).astype(BF16),
              dup(cache_dsa_k[0].reshape(ns, past, DSA_KV_HEADS * HEAD_DIM)).astype(BF16),
              _with_ones(cache_dsa_v[0].reshape(ns, past, DSA_KV_HEADS * HEAD_DIM)).astype(BF16))

    t_p, t_s = x_prompt.shape[1], x_sample.shape[1]
    y_p, new_p = _stream(x_prompt, mod[:nb], None, weights, btiles, min(256, t_p), min(DSA_Q_BLOCK, t_p))
    y_s, new_s = _stream(x_sample, mod[nb:], caches, weights, btiles, t_s, t_s)
    return (y_p, y_s) + new_p + new_s
```
